```python
import math
import jax, jax.numpy as jnp
from jax import lax
import numpy as np

D_MODEL = 1024
BATCH = 32
SEQ = 2048
DEPTH = 1
DEC_BATCH = 32
DEC_SEQ = 32
PAST_LEN = 2048

CHUNK = 64
Q_BLOCK = 128
EPS = 1e-6
NEG_INF = -1e30
MLA_HEADS = 8
QK_NOPE = 64
QK_ROPE = 32
V_HEAD = 64
Q_LORA = 384
KV_LORA = 256
ROPE_BASE = 10000.0
MLA_WIDTH = MLA_HEADS * V_HEAD
ATTN_SCALE = (QK_NOPE + QK_ROPE) ** -0.5
S5_WIDTH = D_MODEL - MLA_WIDTH
S5_GROUP_CH = 16
S5_GROUPS = S5_WIDTH // S5_GROUP_CH
S5_STATE = 64
MIX_WIDTH = MLA_WIDTH + S5_WIDTH
IN_WIDTH = Q_LORA + KV_LORA + QK_ROPE + S5_WIDTH
D_FF = ((8 * D_MODEL + 3 * 256 - 1) // (3 * 256)) * 256

kernel_name = "hymba_mla_s5_streaming_step"


def _rmsnorm(x, g):
    xf = x.astype(jnp.float32)
    y = xf * lax.rsqrt(jnp.mean(xf * xf, axis=-1, keepdims=True) + EPS)
    return (y * g.astype(jnp.float32)).astype(x.dtype)


def _rope_tables(pos):
    inv = 1.0 / (ROPE_BASE ** (jnp.arange(0, QK_ROPE, 2, dtype=jnp.float32) / QK_ROPE))
    ang = pos.astype(jnp.float32)[:, None] * inv[None, :]
    return jnp.cos(ang), jnp.sin(ang)


def _apply_rope(x, cos, sin):
    half = QK_ROPE // 2
    x1 = x[..., :half].astype(jnp.float32)
    x2 = x[..., half:].astype(jnp.float32)
    return jnp.concatenate([x1 * cos - x2 * sin, x2 * cos + x1 * sin], axis=-1).astype(x.dtype)


def _attend(q_nope, q_rope, k_nope, k_rope, v, q_pos, k_pos):
    s = (jnp.einsum("bqhd,bkhd->bhqk", q_nope, k_nope)
         + jnp.einsum("bqhr,bkr->bhqk", q_rope, k_rope))
    s = s.astype(jnp.float32) * ATTN_SCALE
    visible = (k_pos[None, :] // CHUNK) <= (q_pos[:, None] // CHUNK)
    s = jnp.where(visible[None, None], s, NEG_INF)
    p = jax.nn.softmax(s, axis=-1).astype(v.dtype)
    return jnp.einsum("bhqk,bkhd->bqhd", p, v)


def _attend_blocked(q_nope, q_rope, k_nope, k_rope, v, q_pos, k_pos):
    bsz, lq = q_nope.shape[:2]
    if lq <= Q_BLOCK:
        return _attend(q_nope, q_rope, k_nope, k_rope, v, q_pos, k_pos)
    nb = lq // Q_BLOCK
    qn = jnp.moveaxis(q_nope.reshape(bsz, nb, Q_BLOCK, MLA_HEADS, QK_NOPE), 1, 0)
    qr = jnp.moveaxis(q_rope.reshape(bsz, nb, Q_BLOCK, MLA_HEADS, QK_ROPE), 1, 0)
    qp = q_pos.reshape(nb, Q_BLOCK)
    out = lax.map(lambda blk: _attend(blk[0], blk[1], k_nope, k_rope, v, blk[2], k_pos), (qn, qr, qp))
    return jnp.moveaxis(out, 0, 1).reshape(bsz, lq, MLA_HEADS, V_HEAD)


def _cmul(ar, ai, br, bi):
    return ar * br - ai * bi, ar * bi + ai * br


def _scan_combine(e1, e2):
    a1r, a1i, b1r, b1i = e1
    a2r, a2i, b2r, b2i = e2
    ar, ai = _cmul(a2r, a2i, a1r, a1i)
    br, bi = _cmul(a2r, a2i, b1r, b1i)
    return ar, ai, br + b2r, bi + b2i


def _s5(u, x0_re, x0_im, a_re, a_im, log_dt, b_re, b_im, c_re, c_im, d_skip, w_glu, b_glu):
    f32 = jnp.float32
    bsz, length, _ = u.shape
    uf = u.astype(f32).reshape(bsz, length, S5_GROUPS, S5_GROUP_CH)
    dt = jnp.exp(log_dt.astype(f32))[:, None]
    lr, li = a_re.astype(f32), a_im.astype(f32)
    mag, ang = jnp.exp(lr * dt), li * dt
    abr, abi = mag * jnp.cos(ang), mag * jnp.sin(ang)
    den = lr * lr + li * li
    zr, zi = _cmul(abr - 1.0, abi, lr / den, -li / den)
    bbr, bbi = _cmul(zr[..., None], zi[..., None], b_re.astype(f32), b_im.astype(f32))
    bur = jnp.einsum("blgc,gpc->blgp", uf, bbr)
    bui = jnp.einsum("blgc,gpc->blgp", uf, bbi)
    x0r, x0i = _cmul(abr, abi, x0_re.astype(f32), x0_im.astype(f32))
    bur = bur.at[:, 0].add(x0r)
    bui = bui.at[:, 0].add(x0i)
    ar_seq = jnp.broadcast_to(abr, (1, length, S5_GROUPS, S5_STATE))
    ai_seq = jnp.broadcast_to(abi, (1, length, S5_GROUPS, S5_STATE))
    _, _, xr, xi = lax.associative_scan(_scan_combine, (ar_seq, ai_seq, bur, bui), axis=1)
    y = (jnp.einsum("blgp,gcp->blgc", xr, c_re.astype(f32))
         - jnp.einsum("blgp,gcp->blgc", xi, c_im.astype(f32))
         + uf * d_skip.astype(f32).reshape(S5_GROUPS, S5_GROUP_CH))
    z = jax.nn.gelu(y.reshape(bsz, length, S5_WIDTH))
    out = z * jax.nn.sigmoid(z @ w_glu.astype(f32) + b_glu.astype(f32))
    return out.astype(u.dtype), xr[:, -1], xi[:, -1]


def _block(x, pos, past_ckv, past_krope, x0_re, x0_im,
           g_mix, w_in, g_q, w_uq, g_kv, w_ukv,
           s5_a_re, s5_a_im, s5_log_dt, s5_b_re, s5_b_im, s5_c_re, s5_c_im, s5_d, w_glu, b_glu,
           g_out_mla, g_out_s5, w_out, g_ffn, w_gate, w_up, w_down):
    bsz, length, _ = x.shape
    h = _rmsnorm(x, g_mix)
    proj = h @ w_in
    o1 = Q_LORA
    o2 = o1 + KV_LORA
    o3 = o2 + QK_ROPE
    c_q, c_kv, k_r, u = proj[..., :o1], proj[..., o1:o2], proj[..., o2:o3], proj[..., o3:]
    cos, sin = _rope_tables(pos)
    q = (_rmsnorm(c_q, g_q) @ w_uq).reshape(bsz, length, MLA_HEADS, QK_NOPE + QK_ROPE)
    q_nope = q[..., :QK_NOPE]
    q_rope = _apply_rope(q[..., QK_NOPE:], cos[:, None, :], sin[:, None, :])
    ckv = _rmsnorm(c_kv, g_kv)
    krope = _apply_rope(k_r, cos, sin)
    if past_ckv is None:
        all_ckv, all_kr, k_pos = ckv, krope, pos
    else:
        all_ckv = jnp.concatenate([past_ckv.astype(ckv.dtype), ckv], axis=1)
        all_kr = jnp.concatenate([past_krope.astype(krope.dtype), krope], axis=1)
        k_pos = jnp.arange(past_ckv.shape[1] + length)
    kv = (all_ckv @ w_ukv).reshape(bsz, all_ckv.shape[1], MLA_HEADS, QK_NOPE + V_HEAD)
    attn = _attend_blocked(q_nope, q_rope, kv[..., :QK_NOPE], all_kr, kv[..., QK_NOPE:], pos, k_pos)
    ssm, xr, xi = _s5(u, x0_re, x0_im, s5_a_re, s5_a_im, s5_log_dt, s5_b_re, s5_b_im,
                      s5_c_re, s5_c_im, s5_d, w_glu, b_glu)
    mixed = jnp.concatenate([_rmsnorm(attn.reshape(bsz, length, MLA_WIDTH), g_out_mla),
                             _rmsnorm(ssm, g_out_s5)], axis=-1)
    x = x + mixed @ w_out
    h2 = _rmsnorm(x, g_ffn)
    x = x + (jax.nn.silu(h2 @ w_gate) * (h2 @ w_up)) @ w_down
    return x, ckv, krope, xr.astype(x.dtype), xi.astype(x.dtype)


def setup_inputs(seed: int = 0) -> dict:
    key = jax.random.key(seed)
    ks = jax.random.split(key, 32)
    f32 = jnp.float32

    def nrm(k, shape, scale):
        return jax.random.normal(k, shape, f32) * scale

    L = DEPTH
    return {
        "x_prompt": nrm(ks[0], (BATCH, SEQ, D_MODEL), 1.0),
        "x_sample": nrm(ks[1], (DEC_BATCH, DEC_SEQ, D_MODEL), 1.0),
        "cache_mla_ckv": nrm(ks[2], (L, DEC_BATCH, PAST_LEN, KV_LORA), 1.0),
        "cache_mla_krope": nrm(ks[3], (L, DEC_BATCH, PAST_LEN, QK_ROPE), 1.0),
        "state_s5_re": nrm(ks[4], (L, DEC_BATCH, S5_GROUPS, S5_STATE), 0.1),
        "state_s5_im": nrm(ks[5], (L, DEC_BATCH, S5_GROUPS, S5_STATE), 0.1),
        "g_mix": 1.0 + nrm(ks[6], (L, D_MODEL), 0.01),
        "w_in": nrm(ks[7], (L, D_MODEL, IN_WIDTH), D_MODEL ** -0.5),
        "g_q": 1.0 + nrm(ks[8], (L, Q_LORA), 0.01),
        "w_uq": nrm(ks[9], (L, Q_LORA, MLA_HEADS * (QK_NOPE + QK_ROPE)), Q_LORA ** -0.5),
        "g_kv": 1.0 + nrm(ks[10], (L, KV_LORA), 0.01),
        "w_ukv": nrm(ks[11], (L, KV_LORA, MLA_HEADS * (QK_NOPE + V_HEAD)), KV_LORA ** -0.5),
        "s5_a_re": -0.5 + nrm(ks[12], (L, S5_GROUPS, S5_STATE), 0.01),
        "s5_a_im": math.pi * jnp.arange(S5_STATE, dtype=f32)[None, None, :] + nrm(ks[13], (L, S5_GROUPS, S5_STATE), 0.01),
        "s5_log_dt": jax.random.uniform(ks[14], (L, S5_GROUPS), f32, math.log(1e-3), math.log(1e-1)),
        "s5_b_re": nrm(ks[15], (L, S5_GROUPS, S5_STATE, S5_GROUP_CH), (2 * S5_GROUP_CH) ** -0.5),
        "s5_b_im": nrm(ks[16], (L, S5_GROUPS, S5_STATE, S5_GROUP_CH), (2 * S5_GROUP_CH) ** -0.5),
        "s5_c_re": nrm(ks[17], (L, S5_GROUPS, S5_GROUP_CH, S5_STATE), S5_STATE ** -0.5),
        "s5_c_im": nrm(ks[18], (L, S5_GROUPS, S5_GROUP_CH, S5_STATE), S5_STATE ** -0.5),
        "s5_d": nrm(ks[19], (L, S5_WIDTH), 1.0),
        "w_glu": nrm(ks[20], (L, S5_WIDTH, S5_WIDTH), S5_WIDTH ** -0.5),
        "b_glu": nrm(ks[21], (L, S5_WIDTH), 0.01),
        "g_out_mla": 1.0 + nrm(ks[22], (L, MLA_WIDTH), 0.01),
        "g_out_s5": 1.0 + nrm(ks[23], (L, S5_WIDTH), 0.01),
        "w_out": nrm(ks[24], (L, MIX_WIDTH, D_MODEL), MIX_WIDTH ** -0.5),
        "g_ffn": 1.0 + nrm(ks[25], (L, D_MODEL), 0.01),
        "w_gate": nrm(ks[26], (L, D_MODEL, D_FF), D_MODEL ** -0.5),
        "w_up": nrm(ks[27], (L, D_MODEL, D_FF), D_MODEL ** -0.5),
        "w_down": nrm(ks[28], (L, D_FF, D_MODEL), D_FF ** -0.5),
        "g_final": 1.0 + nrm(ks[29], (D_MODEL,), 0.01),
    }


def reference(x_prompt, x_sample, cache_mla_ckv, cache_mla_krope, state_s5_re, state_s5_im,
              g_mix, w_in, g_q, w_uq, g_kv, w_ukv,
              s5_a_re, s5_a_im, s5_log_dt, s5_b_re, s5_b_im, s5_c_re, s5_c_im, s5_d, w_glu, b_glu,
              g_out_mla, g_out_s5, w_out, g_ffn, w_gate, w_up, w_down, g_final):
    past_len = cache_mla_ckv.shape[2]
    p_pos = jnp.arange(x_prompt.shape[1])
    s_pos = past_len + jnp.arange(x_sample.shape[1])
    xp, xs = x_prompt, x_sample
    p_ckv, p_kr, p_re, p_im = [], [], [], []
    s_ckv, s_kr, s_re, s_im = [], [], [], []
    for l in range(DEPTH):
        lw = (g_mix[l], w_in[l], g_q[l], w_uq[l], g_kv[l], w_ukv[l],
              s5_a_re[l], s5_a_im[l], s5_log_dt[l], s5_b_re[l], s5_b_im[l], s5_c_re[l], s5_c_im[l],
              s5_d[l], w_glu[l], b_glu[l], g_out_mla[l], g_out_s5[l], w_out[l],
              g_ffn[l], w_gate[l], w_up[l], w_down[l])
        zeros = jnp.zeros((xp.shape[0], S5_GROUPS, S5_STATE), jnp.float32)
        xp, ck, kr, sr, si = _block(xp, p_pos, None, None, zeros, zeros, *lw)
        p_ckv.append(ck); p_kr.append(kr); p_re.append(sr); p_im.append(si)
        xs, ck, kr, sr, si = _block(xs, s_pos, cache_mla_ckv[l], cache_mla_krope[l],
                                    state_s5_re[l], state_s5_im[l], *lw)
        s_ckv.append(ck); s_kr.append(kr); s_re.append(sr); s_im.append(si)
    y_prompt = _rmsnorm(xp, g_final)
    y_sample = _rmsnorm(xs, g_final)
    return (y_prompt, y_sample,
            jnp.stack(p_ckv), jnp.stack(p_kr), jnp.stack(p_re), jnp.stack(p_im),
            jnp.stack(s_ckv), jnp.stack(s_kr), jnp.stack(s_re), jnp.stack(s_im))
```

```python
import functools
import math

import jax
import jax.numpy as jnp
from jax import lax
from jax.experimental import pallas as pl
from jax.experimental.pallas import tpu as pltpu

F32 = jnp.float32
BF16 = jnp.bfloat16

EPS = 1e-6
NEG_INF = -1e30
CHUNK_SHIFT = 6
ATTN_TQ = 512
HEADS = 8
QK_NOPE = 64
QK_ROPE = 32
V_HEAD = 64
ROPE_BASE = 10000.0
ATTN_SCALE = (QK_NOPE + QK_ROPE) ** -0.5
S5_CH = 16
S5_STATE = 64
S5_TC = 16
LANES = 128
VMEM_LIMIT = 56 * 1024 * 1024


def _rms(x, g):
    return x * lax.rsqrt(jnp.mean(x * x, axis=-1, keepdims=True) + EPS) * g


def _dot(a, b):
    return jnp.dot(a, b, preferred_element_type=F32)


def _dot_t(a, b):
    return lax.dot_general(a, b, (((1,), (1,)), ((), ())), preferred_element_type=F32)


def _const_spec(shape):
    return pl.BlockSpec(shape, lambda *_: (0,) * len(shape))


def _rope(t, c, s1, s2):
    return t * c + pltpu.roll(t, QK_ROPE // 2, 1) * s1 + pltpu.roll(t, LANES - QK_ROPE // 2, 1) * s2


def _proj_kernel(x_ref, c_ref, s1_ref, s2_ref, gmix_ref, win_ref, gq_ref, wuq_ref, gkv_ref, wukv_ref,
                 q_ref, k_ref, v_ref, ckv_ref, kr_ref, u_ref, *, q_lora, kv_lora):
    h = _rms(x_ref[...], gmix_ref[...]).astype(BF16)
    proj = _dot(h, win_ref[...])
    o1 = q_lora
    o2 = o1 + kv_lora
    o3 = o2 + LANES
    c, s1, s2 = c_ref[...], s1_ref[...], s2_ref[...]

    cq = _rms(proj[:, :o1], gq_ref[...]).astype(BF16)
    q = _dot(cq, wuq_ref[...])
    ckv = _rms(proj[:, o1:o2], gkv_ref[...])
    ckv_ref[...] = ckv
    kv = _dot(ckv.astype(BF16), wukv_ref[...])
    kr = _rope(proj[:, o2:o3], c, s1, s2)
    kr_ref[...] = kr[:, QK_NOPE:QK_NOPE + QK_ROPE]
    for hh in range(HEADS):
        sl = slice(hh * LANES, (hh + 1) * LANES)
        q_ref[:, sl] = (_rope(q[:, sl], c, s1, s2) * ATTN_SCALE).astype(BF16)
        k_ref[:, sl] = (kv[:, sl] + kr).astype(BF16)
    v_ref[...] = kv[:, HEADS * LANES:].astype(BF16)
    u_ref[...] = proj[:, o3:].astype(BF16)


def _proj(x2d, tabs, gmix, win, gq, wuq, gkv, wukv, *, seq, tm):
    t, d = x2d.shape
    q_lora, kv_lora = gq.shape[1], gkv.shape[1]
    s5w = win.shape[1] - q_lora - kv_lora - LANES
    ntab = tabs[0].shape[0] // tm
    tab_spec = pl.BlockSpec((tm, LANES), lambda i: (i % ntab, 0))
    row = lambda w: pl.BlockSpec((tm, w), lambda i: (i, 0))
    return pl.pallas_call(
        functools.partial(_proj_kernel, q_lora=q_lora, kv_lora=kv_lora),
        grid=(t // tm,),
        in_specs=[row(d), tab_spec, tab_spec, tab_spec,
                  _const_spec(gmix.shape), _const_spec(win.shape), _const_spec(gq.shape),
                  _const_spec(wuq.shape), _const_spec(gkv.shape), _const_spec(wukv.shape)],
        out_specs=[row(HEADS * LANES), row(HEADS * LANES), row(HEADS * V_HEAD),
                   row(kv_lora), row(QK_ROPE), row(s5w)],
        out_shape=[jax.ShapeDtypeStruct((t, HEADS * LANES), BF16),
                   jax.ShapeDtypeStruct((t, HEADS * LANES), BF16),
                   jax.ShapeDtypeStruct((t, HEADS * V_HEAD), BF16),
                   jax.ShapeDtypeStruct((t, kv_lora), F32),
                   jax.ShapeDtypeStruct((t, QK_ROPE), F32),
                   jax.ShapeDtypeStruct((t, s5w), BF16)],
        compiler_params=pltpu.CompilerParams(dimension_semantics=("arbitrary",),
                                             vmem_limit_bytes=VMEM_LIMIT),
        name="proj",
    )(x2d, *tabs, gmix, win, gq, wuq, gkv, wukv)


def _attn_p_kernel(q_ref, k_ref, v_ref, o_ref, *, tq, tk):
    qi = pl.program_id(2)
    nkv = (qi + 1) * (tq // tk)
    qchunk = (qi * tq + lax.broadcasted_iota(jnp.int32, (tq, tk), 0)) >> CHUNK_SHIFT
    kcol = lax.broadcasted_iota(jnp.int32, (tq, tk), 1)

    def body(kt, carry):
        ks = pl.multiple_of(kt * tk, tk)
        kblk = k_ref[pl.ds(ks, tk), :]
        vblk = v_ref[pl.ds(ks, tk), :]
        vis = ((ks + kcol) >> CHUNK_SHIFT) <= qchunk
        out = []
        for hh in range(2):
            m, l, a = carry[hh]
            sl = slice(hh * LANES, (hh + 1) * LANES)
            s = jnp.where(vis, _dot_t(q_ref[:, sl], kblk[:, sl]), NEG_INF)
            mn = jnp.maximum(m, jnp.max(s, axis=-1, keepdims=True))
            p = jnp.exp(s - mn)
            alpha = jnp.exp(m - mn)
            out.append((mn, alpha * l + jnp.sum(p, axis=-1, keepdims=True),
                        alpha * a + _dot(p.astype(BF16), vblk)))
        return tuple(out)

    init = (jnp.full((tq, 1), NEG_INF, F32), jnp.zeros((tq, 1), F32), jnp.zeros((tq, LANES), F32))
    (_, l0, a0), (_, l1, a1) = lax.fori_loop(0, nkv, body, (init, init))
    lane = lax.broadcasted_iota(jnp.int32, (tq, LANES), 1)
    o_ref[...] = jnp.where(lane < V_HEAD, a0 / l0, a1 / l1).astype(BF16)


def _attn_prompt(q, k, v, *, batch, seq, tq, tk):
    nq = seq // tq
    return pl.pallas_call(
        functools.partial(_attn_p_kernel, tq=tq, tk=tk),
        grid=(batch, HEADS // 2, nq),
        in_specs=[pl.BlockSpec((tq, 2 * LANES), lambda b, h, i: (b * nq + i, h)),
                  pl.BlockSpec((seq, 2 * LANES), lambda b, h, i: (b, h)),
                  pl.BlockSpec((seq, 2 * V_HEAD), lambda b, h, i: (b, h))],
        out_specs=pl.BlockSpec((tq, 2 * V_HEAD), lambda b, h, i: (b * nq + i, h)),
        out_shape=jax.ShapeDtypeStruct((batch * seq, HEADS * V_HEAD), BF16),
        compiler_params=pltpu.CompilerParams(
            dimension_semantics=("arbitrary", "arbitrary", "arbitrary"),
            vmem_limit_bytes=VMEM_LIMIT),
        name="attn_p",
    )(q, k, v)


def _attn_s_kernel(q_ref, pc_ref, pk_ref, nc_ref, nk_ref, wabs_ref, wuv_ref, o_ref, *,
                   dec, past, kv_lora):
    q = q_ref[...]
    qf = jnp.concatenate(
        [_dot(q[:, hh * LANES:(hh + 1) * LANES], wabs_ref[hh]) for hh in range(HEADS)],
        axis=0).astype(BF16)
    qa, qr = qf[:, :kv_lora], qf[:, kv_lora:kv_lora + QK_ROPE]
    pc = pc_ref[...].astype(BF16)
    nc = nc_ref[...].astype(BF16)
    s_p = _dot_t(qa, pc) + _dot_t(qr, pk_ref[...].astype(BF16))
    s_n = _dot_t(qa, nc) + _dot_t(qr, nk_ref[...].astype(BF16))
    rows = HEADS * dec
    qpos = past + jnp.concatenate([lax.broadcasted_iota(jnp.int32, (dec, 1), 0)] * HEADS, axis=0)
    qchunk = qpos >> CHUNK_SHIFT
    kchunk_p = lax.broadcasted_iota(jnp.int32, (rows, past), 1) >> CHUNK_SHIFT
    kchunk_n = (past + lax.broadcasted_iota(jnp.int32, (rows, dec), 1)) >> CHUNK_SHIFT
    s_p = jnp.where(kchunk_p <= qchunk, s_p, NEG_INF)
    s_n = jnp.where(kchunk_n <= qchunk, s_n, NEG_INF)
    m = jnp.maximum(jnp.max(s_p, axis=-1, keepdims=True), jnp.max(s_n, axis=-1, keepdims=True))
    p_p = jnp.exp(s_p - m)
    p_n = jnp.exp(s_n - m)
    l = jnp.sum(p_p, axis=-1, keepdims=True) + jnp.sum(p_n, axis=-1, keepdims=True)
    olat = ((_dot(p_p.astype(BF16), pc) + _dot(p_n.astype(BF16), nc)) / l).astype(BF16)
    acc = jnp.zeros((dec, HEADS * V_HEAD), F32)
    for hh in range(HEADS):
        acc = acc + _dot(olat[hh * dec:(hh + 1) * dec], wuv_ref[hh])
    o_ref[...] = acc.astype(BF16)


def _attn_sample(q, past_ckv, past_kr, new_ckv, new_kr, wabs, wuv, *, batch, dec, past):
    kv_lora = past_ckv.shape[-1]
    return pl.pallas_call(
        functools.partial(_attn_s_kernel, dec=dec, past=past, kv_lora=kv_lora),
        grid=(batch,),
        in_specs=[pl.BlockSpec((dec, HEADS * LANES), lambda b: (b, 0)),
                  pl.BlockSpec((past, kv_lora), lambda b: (b, 0)),
                  pl.BlockSpec((past, QK_ROPE), lambda b: (b, 0)),
                  pl.BlockSpec((dec, kv_lora), lambda b: (b, 0)),
                  pl.BlockSpec((dec, QK_ROPE), lambda b: (b, 0)),
                  _const_spec(wabs.shape), _const_spec(wuv.shape)],
        out_specs=pl.BlockSpec((dec, HEADS * V_HEAD), lambda b: (b, 0)),
        out_shape=jax.ShapeDtypeStruct((batch * dec, HEADS * V_HEAD), BF16),
        compiler_params=pltpu.CompilerParams(dimension_semantics=("arbitrary",),
                                             vmem_limit_bytes=VMEM_LIMIT),
        name="attn_s",
    )(q, past_ckv, past_kr, new_ckv, new_kr, wabs, wuv)


def _s5_kernel(x_ref, w1_ref, q_ref, a_ref, d_ref, x0re_ref, x0im_ref,
               y_ref, sre_ref, sim_ref, e_scr, xs_scr, st_scr, *, nb, rows):
    @pl.when(pl.program_id(1) == 0)
    def _():
        st_scr[0] = x0re_ref[...]
        st_scr[1] = x0im_ref[...]

    x0, x1 = x_ref[0], x_ref[1]
    r0 = _dot(x0, w1_ref[0])
    r1 = _dot(x1, w1_ref[1])
    e_scr[...] = r0[:, 256:] + r1[:, 256:]
    ar, ai = a_ref[0:1, :], a_ref[1:2, :]

    def step(j, carry):
        re, im = carry
        r = pl.multiple_of(j * nb, nb)
        xs_scr[pl.ds(r, nb), 0:LANES] = re
        xs_scr[pl.ds(r, nb), LANES:2 * LANES] = im
        e_re = e_scr[pl.ds(r, nb), 0:LANES]
        e_im = e_scr[pl.ds(r, nb), LANES:2 * LANES]
        return ar * re - ai * im + e_re, ar * im + ai * re + e_im

    re, im = lax.fori_loop(0, rows // nb, step, (st_scr[0], st_scr[1]))
    st_scr[0] = re
    st_scr[1] = im
    sre_ref[...] = re
    sim_ref[...] = im
    xs = xs_scr[...].astype(BF16)
    y_ref[0] = (r0[:, :256] + _dot(xs, q_ref[0]) + x0.astype(F32) * d_ref[0]).astype(BF16)
    y_ref[1] = (r1[:, :256] + _dot(xs, q_ref[1]) + x1.astype(F32) * d_ref[1]).astype(BF16)


def _s5(u2, w1, qm, am, d2, x0re, x0im, *, nb, rows):
    g, total, kw = u2.shape
    npair = g // 2
    nrb = total // rows
    return pl.pallas_call(
        functools.partial(_s5_kernel, nb=nb, rows=rows),
        grid=(npair, nrb),
        in_specs=[pl.BlockSpec((2, rows, kw), lambda p, r: (p, r, 0)),
                  pl.BlockSpec((2, kw, 2 * kw), lambda p, r: (p, 0, 0)),
                  pl.BlockSpec((2, kw, kw), lambda p, r: (p, 0, 0)),
                  pl.BlockSpec((None, 8, LANES), lambda p, r: (p, 0, 0)),
                  pl.BlockSpec((2, 1, kw), lambda p, r: (p, 0, 0)),
                  pl.BlockSpec((nb, LANES), lambda p, r: (0, p)),
                  pl.BlockSpec((nb, LANES), lambda p, r: (0, p))],
        out_specs=[pl.BlockSpec((2, rows, kw), lambda p, r: (p, r, 0)),
                   pl.BlockSpec((nb, LANES), lambda p, r: (0, p)),
                   pl.BlockSpec((nb, LANES), lambda p, r: (0, p))],
        out_shape=[jax.ShapeDtypeStruct((g, total, kw), BF16),
                   jax.ShapeDtypeStruct(x0re.shape, F32),
                   jax.ShapeDtypeStruct(x0im.shape, F32)],
        scratch_shapes=[pltpu.VMEM((rows, 2 * LANES), F32),
                        pltpu.VMEM((rows, 2 * LANES), F32),
                        pltpu.VMEM((2, nb, LANES), F32)],
        compiler_params=pltpu.CompilerParams(dimension_semantics=("arbitrary", "arbitrary"),
                                             vmem_limit_bytes=VMEM_LIMIT),
        name="s5",
    )(u2, w1, qm, am, d2, x0re, x0im)


def _s5_operators(a_re, a_im, log_dt, b_re, b_im, c_re, c_im, d_skip):
    hp = lax.Precision.HIGHEST
    g, p = a_re.shape
    tc, ch = S5_TC, S5_CH
    dt = jnp.exp(log_dt)[:, None]
    lr, li = a_re, a_im
    mag, ang = jnp.exp(lr * dt), li * dt
    abr, abi = mag * jnp.cos(ang), mag * jnp.sin(ang)
    den = lr * lr + li * li
    ir, ii = lr / den, -li / den
    zr, zi = (abr - 1.0) * ir - abi * ii, (abr - 1.0) * ii + abi * ir
    bbr = zr[..., None] * b_re - zi[..., None] * b_im
    bbi = zr[..., None] * b_im + zi[..., None] * b_re
    ks = jnp.arange(tc + 1, dtype=F32)[:, None, None]
    pm, pa = jnp.exp(ks * (lr * dt)), ks * ang
    pr, pi = pm * jnp.cos(pa), pm * jnp.sin(pa)
    wr = pr[..., None] * bbr - pi[..., None] * bbi
    wi = pr[..., None] * bbi + pi[..., None] * bbr
    kt = (jnp.einsum("gcp,kgpd->kgcd", c_re, wr[:tc], precision=hp)
          - jnp.einsum("gcp,kgpd->kgcd", c_im, wi[:tc], precision=hp))
    s_idx = jnp.arange(tc)[:, None]
    t_idx = jnp.arange(tc)[None, :]
    lag = jnp.clip(t_idx - s_idx, 0, tc - 1)
    m = jnp.where((t_idx >= s_idx)[:, :, None, None, None], kt[lag], 0.0)
    m = m.transpose(2, 0, 4, 1, 3).reshape(g, tc * ch, tc * ch)
    rev = wr[tc - 1 - jnp.arange(tc)], wi[tc - 1 - jnp.arange(tc)]
    p_re = rev[0].transpose(1, 0, 3, 2).reshape(g, tc * ch, p)
    p_im = rev[1].transpose(1, 0, 3, 2).reshape(g, tc * ch, p)
    q_re = (c_re[None] * pr[1:, :, None, :] - c_im[None] * pi[1:, :, None, :])
    q_im = -(c_re[None] * pi[1:, :, None, :] + c_im[None] * pr[1:, :, None, :])
    q_re = q_re.transpose(1, 3, 0, 2).reshape(g, p, tc * ch)
    q_im = q_im.transpose(1, 3, 0, 2).reshape(g, p, tc * ch)
    odd = (jnp.arange(g) % 2 == 1)[:, None, None]
    zp = jnp.zeros_like(p_re)
    w1 = jnp.concatenate([m, jnp.where(odd, zp, p_re), jnp.where(odd, p_re, zp),
                          jnp.where(odd, zp, p_im), jnp.where(odd, p_im, zp)], axis=2)
    zq = jnp.zeros_like(q_re)
    qm = jnp.concatenate([jnp.where(odd, zq, q_re), jnp.where(odd, q_re, zq),
                          jnp.where(odd, zq, q_im), jnp.where(odd, q_im, zq)], axis=1)
    am = jnp.zeros((g // 2, 8, 2 * p), F32)
    am = am.at[:, 0].set(pr[tc].reshape(g // 2, 2 * p)).at[:, 1].set(pi[tc].reshape(g // 2, 2 * p))
    d2 = jnp.tile(d_skip.reshape(g, 1, ch), (1, 1, tc))
    return w1.astype(BF16), qm.astype(BF16), am, d2


def _s5_layer(u, ops, x0_re, x0_im, *, batch, seq):
    w1, qm, am, d2 = ops
    g = w1.shape[0]
    tc, ch = S5_TC, S5_CH
    nchunk = seq // tc
    u2 = u.reshape(batch, nchunk, tc, g, ch).transpose(3, 1, 0, 2, 4).reshape(g, nchunk * batch, tc * ch)
    rows = min(nchunk, 32) * batch
    y2, sre, sim = _s5(u2, w1, qm, am, d2, x0_re, x0_im, nb=batch, rows=rows)
    y = y2.reshape(g, nchunk, batch, tc, ch).transpose(2, 1, 3, 0, 4).reshape(batch * seq, g * ch)
    return y, sre, sim


def _gelu(x):
    return 0.5 * x * (1.0 + jnp.tanh(math.sqrt(2.0 / math.pi) * (x + 0.044715 * (x * x * x))))


def _sigmoid(x):
    return 1.0 / (1.0 + jnp.exp(-x))


def _ffn_kernel(x_ref, a_ref, y_ref, wglu_ref, bglu_ref, gmla_ref, gs5_ref, wout_ref, gffn_ref,
                wgate_ref, wup_ref, wdown_ref, gfin_ref, o_ref):
    z = _gelu(y_ref[...].astype(F32))
    ssm = z * _sigmoid(_dot(z.astype(BF16), wglu_ref[...]) + bglu_ref[...])
    mixed = jnp.concatenate([_rms(a_ref[...].astype(F32), gmla_ref[...]),
                             _rms(ssm, gs5_ref[...])], axis=-1).astype(BF16)
    x1 = x_ref[...] + _dot(mixed, wout_ref[...])
    h2 = _rms(x1, gffn_ref[...]).astype(BF16)
    gate = _dot(h2, wgate_ref[...])
    act = (gate * _sigmoid(gate) * _dot(h2, wup_ref[...])).astype(BF16)
    x2 = x1 + _dot(act, wdown_ref[...])
    o_ref[...] = _rms(x2, gfin_ref[...])


def _ffn(x2d, attn, y, wglu, bglu, gmla, gs5, wout, gffn, wgate, wup, wdown, gfin, *, tm):
    t, d = x2d.shape
    row = lambda w: pl.BlockSpec((tm, w), lambda i: (i, 0))
    resident = lambda a: pl.BlockSpec(a.shape, lambda i: (0,) * a.ndim, pipeline_mode=pl.Buffered(1))
    consts = (wglu, bglu, gmla, gs5, wout, gffn, wgate, wup, wdown, gfin)
    return pl.pallas_call(
        _ffn_kernel,
        grid=(t // tm,),
        in_specs=[row(d), row(attn.shape[1]), row(y.shape[1])] + [resident(a) for a in consts],
        out_specs=row(d),
        out_shape=jax.ShapeDtypeStruct((t, d), F32),
        compiler_params=pltpu.CompilerParams(dimension_semantics=("arbitrary",),
                                             vmem_limit_bytes=VMEM_LIMIT),
        name="ffn",
    )(x2d, attn, y, *consts)


def _rope_tabs(pos, tm):
    inv = 1.0 / (ROPE_BASE ** (jnp.arange(0, QK_ROPE, 2, dtype=F32) / QK_ROPE))
    ang = pos.astype(F32)[:, None] * inv[None, :]
    cos, sin = jnp.cos(ang), jnp.sin(ang)
    n = pos.shape[0]
    half = QK_ROPE // 2
    z = lambda w: jnp.zeros((n, w), F32)
    c = jnp.concatenate([jnp.ones((n, QK_NOPE), F32), cos, cos, z(LANES - QK_NOPE - QK_ROPE)], axis=1)
    s1 = jnp.concatenate([z(QK_NOPE + half), sin, z(LANES - QK_NOPE - QK_ROPE)], axis=1)
    s2 = jnp.concatenate([z(QK_NOPE), -sin, z(LANES - QK_NOPE - half)], axis=1)
    reps = max(1, tm // n)
    return tuple(jnp.tile(t, (reps, 1)) for t in (c, s1, s2))


def _pad_heads(w, width):
    r = w.shape[0]
    return jnp.pad(w.reshape(r, HEADS, width), ((0, 0), (0, 0), (0, LANES - width))).reshape(r, HEADS * LANES)


def _layer_weights(w_in, w_uq, w_ukv, q_lora, kv_lora):
    o1, o2, o3 = q_lora, q_lora + kv_lora, q_lora + kv_lora + QK_ROPE
    d = w_in.shape[0]
    kr_cols = jnp.concatenate([jnp.zeros((d, QK_NOPE), F32), w_in[:, o2:o3],
                               jnp.zeros((d, LANES - QK_NOPE - QK_ROPE), F32)], axis=1)
    win = jnp.concatenate([w_in[:, :o2], kr_cols, w_in[:, o3:]], axis=1).astype(BF16)
    wuq = _pad_heads(w_uq, QK_NOPE + QK_ROPE).astype(BF16)
    ukv = w_ukv.reshape(kv_lora, HEADS, QK_NOPE + V_HEAD)
    w_uk, w_uv = ukv[..., :QK_NOPE], ukv[..., QK_NOPE:]
    wukv = jnp.concatenate([_pad_heads(w_uk.reshape(kv_lora, HEADS * QK_NOPE), QK_NOPE),
                            w_uv.reshape(kv_lora, HEADS * V_HEAD)], axis=1).astype(BF16)
    wabs = jnp.zeros((HEADS, LANES, kv_lora + LANES), F32)
    wabs = wabs.at[:, :QK_NOPE, :kv_lora].set(w_uk.transpose(1, 2, 0))
    wabs = wabs.at[:, QK_NOPE:QK_NOPE + QK_ROPE, kv_lora:kv_lora + QK_ROPE].set(
        jnp.broadcast_to(jnp.eye(QK_ROPE, dtype=F32), (HEADS, QK_ROPE, QK_ROPE)))
    wuv = jnp.zeros((HEADS, kv_lora, HEADS, V_HEAD), F32)
    wuv = wuv.at[jnp.arange(HEADS), :, jnp.arange(HEADS), :].set(w_uv.transpose(1, 0, 2))
    wuv = wuv.reshape(HEADS, kv_lora, HEADS * V_HEAD)
    return win, wuq, wukv, wabs.astype(BF16), wuv.astype(BF16)


def _token_tile(batch, seq):
    for tm in (512, 256, 128, 64, 32, 16):
        if (batch * seq) % tm == 0 and (seq % tm == 0 or tm % seq == 0):
            return tm
    raise ValueError(f"no token tile for batch={batch} seq={seq}")


def kernel(x_prompt, x_sample, cache_mla_ckv, cache_mla_krope, state_s5_re, state_s5_im,
           g_mix, w_in, g_q, w_uq, g_kv, w_ukv,
           s5_a_re, s5_a_im, s5_log_dt, s5_b_re, s5_b_im, s5_c_re, s5_c_im, s5_d, w_glu, b_glu,
           g_out_mla, g_out_s5, w_out, g_ffn, w_gate, w_up, w_down, g_final):
    bp, lp, d = x_prompt.shape
    bs, ls, _ = x_sample.shape
    past = cache_mla_ckv.shape[2]
    q_lora, kv_lora = g_q.shape[1], g_kv.shape[1]
    groups, state = s5_a_re.shape[1], s5_a_re.shape[2]
    assert g_mix.shape[0] == 1, "single-layer model"
    assert state == S5_STATE and s5_b_re.shape[-1] == S5_CH and groups % 2 == 0
    assert lp % S5_TC == 0 and ls % S5_TC == 0 and bp % 8 == 0 and bs % 8 == 0

    xp = x_prompt.reshape(bp * lp, d)
    xs = x_sample.reshape(bs * ls, d)
    tmp, tms = _token_tile(bp, lp), _token_tile(bs, ls)
    tabs_p = _rope_tabs(jnp.arange(lp), tmp)
    tabs_s = _rope_tabs(past + jnp.arange(ls), tms)
    tq = min(lp, ATTN_TQ)
    row2 = lambda a: a.reshape(1, -1)

    win, wuq, wukv, wabs, wuv = _layer_weights(w_in[0], w_uq[0], w_ukv[0], q_lora, kv_lora)
    ops = _s5_operators(s5_a_re[0], s5_a_im[0], s5_log_dt[0], s5_b_re[0], s5_b_im[0],
                        s5_c_re[0], s5_c_im[0], s5_d[0])
    ffn_w = (w_glu[0].astype(BF16), row2(b_glu[0]), row2(g_out_mla[0]), row2(g_out_s5[0]),
             w_out[0].astype(BF16), row2(g_ffn[0]), w_gate[0].astype(BF16),
             w_up[0].astype(BF16), w_down[0].astype(BF16), row2(g_final))
    proj_w = (row2(g_mix[0]), win, row2(g_q[0]), wuq, row2(g_kv[0]), wukv)

    q, k, v, p_ckv, p_kr, u = _proj(xp, tabs_p, *proj_w, seq=lp, tm=tmp)
    attn = _attn_prompt(q, k, v, batch=bp, seq=lp, tq=tq, tk=tq)
    zeros = jnp.zeros((bp, groups * state), F32)
    y, p_re, p_im = _s5_layer(u, ops, zeros, zeros, batch=bp, seq=lp)
    yp = _ffn(xp, attn, y, *ffn_w, tm=tmp)

    q, _, _, s_ckv, s_kr, u = _proj(xs, tabs_s, *proj_w, seq=ls, tm=tms)
    attn = _attn_sample(q, cache_mla_ckv[0].reshape(bs * past, kv_lora),
                        cache_mla_krope[0].reshape(bs * past, QK_ROPE), s_ckv, s_kr, wabs, wuv,
                        batch=bs, dec=ls, past=past)
    y, s_re, s_im = _s5_layer(u, ops, state_s5_re[0].reshape(bs, groups * state),
                              state_s5_im[0].reshape(bs, groups * state), batch=bs, seq=ls)
    ys = _ffn(xs, attn, y, *ffn_w, tm=tms)

    return (yp.reshape(bp, lp, d), ys.reshape(bs, ls, d),
            p_ckv.reshape(1, bp, lp, kv_lora), p_kr.reshape(1, bp, lp, QK_ROPE),
            p_re.reshape(1, bp, groups, state), p_im.reshape(1, bp, groups, state),
            s_ckv.reshape(1, bs, ls, kv_lora), s_kr.reshape(1, bs, ls, QK_ROPE),
            s_re.reshape(1, bs, groups, state), s_im.reshape(1, bs, groups, state))
```

```python
import functools
import math

import jax
import jax.numpy as jnp
from jax import lax
from jax.experimental import pallas as pl
from jax.experimental.pallas import tpu as pltpu

F32 = jnp.float32
BF16 = jnp.bfloat16

EPS = 1e-6
NEG_INF = -1e30
CHUNK_SHIFT = 6
ATTN_TQ = 512
HEADS = 8
QK_NOPE = 64
QK_ROPE = 32
V_HEAD = 64
ROPE_BASE = 10000.0
Q_SCALE = (QK_NOPE + QK_ROPE) ** -0.5 * math.log2(math.e)
S5_CH = 16
S5_STATE = 64
S5_TC = 16
LANES = 128
VMEM_LIMIT = 56 * 1024 * 1024


def _rms(x, g):
    return x * lax.rsqrt(jnp.mean(x * x, axis=-1, keepdims=True) + EPS) * g


def _dot(a, b):
    return jnp.dot(a, b, preferred_element_type=F32)


def _dot_t(a, b):
    return lax.dot_general(a, b, (((1,), (1,)), ((), ())), preferred_element_type=F32)


def _const_spec(shape):
    return pl.BlockSpec(shape, lambda *_: (0,) * len(shape))


def _rope(t, c, s1, s2):
    return t * c + pltpu.roll(t, QK_ROPE // 2, 1) * s1 + pltpu.roll(t, LANES - QK_ROPE // 2, 1) * s2


def _proj_kernel(x_ref, c_ref, s1_ref, s2_ref, gmix_ref, win_ref, gq_ref, wuq_ref, gkv_ref, *rest,
                 q_lora, kv_lora, emit_kv, emit_u):
    rest = list(rest)
    wukv_ref = rest.pop(0) if emit_kv else None
    q_ref = rest.pop(0)
    k_ref, v_ref = (rest.pop(0), rest.pop(0)) if emit_kv else (None, None)
    ckv_ref, kr_ref = rest.pop(0), rest.pop(0)
    u_ref = rest.pop(0) if emit_u else None

    h = _rms(x_ref[...], gmix_ref[...]).astype(BF16)
    proj = _dot(h, win_ref[...])
    o1 = q_lora
    o2 = o1 + kv_lora
    o3 = o2 + LANES
    c, s1, s2 = c_ref[...], s1_ref[...], s2_ref[...]

    cq = _rms(proj[:, :o1], gq_ref[...]).astype(BF16)
    q = _dot(cq, wuq_ref[...])
    ckv = _rms(proj[:, o1:o2], gkv_ref[...])
    ckv_ref[...] = ckv
    kr = _rope(proj[:, o2:o3], c, s1, s2)
    kr_ref[...] = kr[:, QK_NOPE:QK_NOPE + QK_ROPE]
    if emit_kv:
        kv = _dot(ckv.astype(BF16), wukv_ref[...])
        v_ref[...] = kv[:, HEADS * LANES:].astype(BF16)
    for hh in range(HEADS):
        sl = slice(hh * LANES, (hh + 1) * LANES)
        q_ref[:, sl] = (_rope(q[:, sl], c, s1, s2) * Q_SCALE).astype(BF16)
        if emit_kv:
            k_ref[:, sl] = (kv[:, sl] + kr).astype(BF16)
    if emit_u:
        u_ref[...] = proj[:, o3:].astype(BF16)


def _proj(x2d, tabs, gmix, win, gq, wuq, gkv, wukv, *, tm, emit_kv, emit_u):
    t, d = x2d.shape
    q_lora, kv_lora = gq.shape[1], gkv.shape[1]
    s5w = win.shape[1] - q_lora - kv_lora - LANES
    assert (s5w > 0) == emit_u
    ntab = tabs[0].shape[0] // tm
    tab_spec = pl.BlockSpec((tm, LANES), lambda i: (i % ntab, 0))
    row = lambda w: pl.BlockSpec((tm, w), lambda i: (i, 0))
    outs = [(HEADS * LANES, BF16)]
    if emit_kv:
        outs += [(HEADS * LANES, BF16), (HEADS * V_HEAD, BF16)]
    outs += [(kv_lora, F32), (QK_ROPE, F32)]
    if emit_u:
        outs += [(s5w, BF16)]
    weights = (gmix, win, gq, wuq, gkv) + ((wukv,) if emit_kv else ())
    return pl.pallas_call(
        functools.partial(_proj_kernel, q_lora=q_lora, kv_lora=kv_lora, emit_kv=emit_kv, emit_u=emit_u),
        grid=(t // tm,),
        in_specs=[row(d), tab_spec, tab_spec, tab_spec] + [_const_spec(w.shape) for w in weights],
        out_specs=[row(w) for w, _ in outs],
        out_shape=[jax.ShapeDtypeStruct((t, w), dt) for w, dt in outs],
        compiler_params=pltpu.CompilerParams(dimension_semantics=("arbitrary",),
                                             vmem_limit_bytes=VMEM_LIMIT),
        name="proj",
    )(x2d, *tabs, *weights)


def _attn_p_kernel(q_ref, k_ref, v_ref, o_ref, *, tq):
    qi = pl.program_id(2)

    def tile(kt, carry, vis):
        ks = pl.multiple_of(kt * tq, tq)
        kblk = k_ref[pl.ds(ks, tq), :]
        vblk = v_ref[pl.ds(ks, tq), :]
        out = []
        for hh in range(2):
            m, l, a = carry[hh]
            sl = slice(hh * LANES, (hh + 1) * LANES)
            s = _dot_t(q_ref[:, sl], kblk[:, sl])
            if vis is not None:
                s = jnp.where(vis, s, NEG_INF)
            mn = jnp.maximum(m, jnp.max(s, axis=-1, keepdims=True))
            p = jnp.exp2(s - mn)
            alpha = jnp.exp2(m - mn)
            out.append((mn, alpha * l + jnp.sum(p, axis=-1, keepdims=True),
                        alpha * a + _dot(p.astype(BF16), vblk)))
        return tuple(out)

    init = (jnp.full((tq, 1), NEG_INF, F32), jnp.zeros((tq, 1), F32), jnp.zeros((tq, LANES), F32))
    carry = lax.fori_loop(0, qi, lambda kt, c: tile(kt, c, None), (init, init))
    qchunk = lax.broadcasted_iota(jnp.int32, (tq, tq), 0) >> CHUNK_SHIFT
    kchunk = lax.broadcasted_iota(jnp.int32, (tq, tq), 1) >> CHUNK_SHIFT
    (_, l0, a0), (_, l1, a1) = tile(qi, carry, kchunk <= qchunk)
    lane = lax.broadcasted_iota(jnp.int32, (tq, LANES), 1)
    o_ref[...] = jnp.where(lane < V_HEAD, a0 / l0, a1 / l1).astype(BF16)


def _attn_prompt(q, k, v, *, batch, seq, tq):
    assert seq % tq == 0 and tq % (1 << CHUNK_SHIFT) == 0
    nq = seq // tq
    return pl.pallas_call(
        functools.partial(_attn_p_kernel, tq=tq),
        grid=(batch, HEADS // 2, nq),
        in_specs=[pl.BlockSpec((tq, 2 * LANES), lambda b, h, i: (b * nq + i, h)),
                  pl.BlockSpec((seq, 2 * LANES), lambda b, h, i: (b, h)),
                  pl.BlockSpec((seq, 2 * V_HEAD), lambda b, h, i: (b, h))],
        out_specs=pl.BlockSpec((tq, 2 * V_HEAD), lambda b, h, i: (b * nq + i, h)),
        out_shape=jax.ShapeDtypeStruct((batch * seq, HEADS * V_HEAD), BF16),
        compiler_params=pltpu.CompilerParams(
            dimension_semantics=("arbitrary", "arbitrary", "arbitrary"),
            vmem_limit_bytes=VMEM_LIMIT),
        name="attn_p",
    )(q, k, v)


def _attn_s_kernel(q_ref, pc_ref, pk_ref, nc_ref, nk_ref, wabs_ref, wuv_ref, o_ref, *,
                   dec, past, kv_lora):
    q = q_ref[...]
    qf = jnp.concatenate(
        [_dot(q[:, hh * LANES:(hh + 1) * LANES], wabs_ref[hh]) for hh in range(HEADS)],
        axis=0).astype(BF16)
    qa, qr = qf[:, :kv_lora], qf[:, kv_lora:kv_lora + QK_ROPE]
    pc = pc_ref[...].astype(BF16)
    nc = nc_ref[...].astype(BF16)
    s_p = _dot_t(qa, pc) + _dot_t(qr, pk_ref[...].astype(BF16))
    s_n = _dot_t(qa, nc) + _dot_t(qr, nk_ref[...].astype(BF16))
    rows = HEADS * dec
    qpos = past + jnp.concatenate([lax.broadcasted_iota(jnp.int32, (dec, 1), 0)] * HEADS, axis=0)
    qchunk = qpos >> CHUNK_SHIFT
    kchunk_p = lax.broadcasted_iota(jnp.int32, (rows, past), 1) >> CHUNK_SHIFT
    kchunk_n = (past + lax.broadcasted_iota(jnp.int32, (rows, dec), 1)) >> CHUNK_SHIFT
    s_p = jnp.where(kchunk_p <= qchunk, s_p, NEG_INF)
    s_n = jnp.where(kchunk_n <= qchunk, s_n, NEG_INF)
    m = jnp.maximum(jnp.max(s_p, axis=-1, keepdims=True), jnp.max(s_n, axis=-1, keepdims=True))
    p_p = jnp.exp2(s_p - m)
    p_n = jnp.exp2(s_n - m)
    l = jnp.sum(p_p, axis=-1, keepdims=True) + jnp.sum(p_n, axis=-1, keepdims=True)
    olat = ((_dot(p_p.astype(BF16), pc) + _dot(p_n.astype(BF16), nc)) / l).astype(BF16)
    acc = jnp.zeros((dec, HEADS * V_HEAD), F32)
    for hh in range(HEADS):
        acc = acc + _dot(olat[hh * dec:(hh + 1) * dec], wuv_ref[hh])
    o_ref[...] = acc.astype(BF16)


def _attn_sample(q, past_ckv, past_kr, new_ckv, new_kr, wabs, wuv, *, batch, dec, past):
    kv_lora = past_ckv.shape[-1]
    return pl.pallas_call(
        functools.partial(_attn_s_kernel, dec=dec, past=past, kv_lora=kv_lora),
        grid=(batch,),
        in_specs=[pl.BlockSpec((dec, HEADS * LANES), lambda b: (b, 0)),
                  pl.BlockSpec((past, kv_lora), lambda b: (b, 0)),
                  pl.BlockSpec((past, QK_ROPE), lambda b: (b, 0)),
                  pl.BlockSpec((dec, kv_lora), lambda b: (b, 0)),
                  pl.BlockSpec((dec, QK_ROPE), lambda b: (b, 0)),
                  _const_spec(wabs.shape), _const_spec(wuv.shape)],
        out_specs=pl.BlockSpec((dec, HEADS * V_HEAD), lambda b: (b, 0)),
        out_shape=jax.ShapeDtypeStruct((batch * dec, HEADS * V_HEAD), BF16),
        compiler_params=pltpu.CompilerParams(dimension_semantics=("arbitrary",),
                                             vmem_limit_bytes=VMEM_LIMIT),
        name="attn_s",
    )(q, past_ckv, past_kr, new_ckv, new_kr, wabs, wuv)


def _s5_kernel(x_ref, w1_ref, q_ref, a_ref, d_ref, x0re_ref, x0im_ref,
               y_ref, sre_ref, sim_ref, e_scr, xs_scr, st_scr, *, nb, rows):
    @pl.when(pl.program_id(1) == 0)
    def _():
        st_scr[0] = x0re_ref[...]
        st_scr[1] = x0im_ref[...]

    x0, x1 = x_ref[0], x_ref[1]
    r0 = _dot(x0, w1_ref[0])
    r1 = _dot(x1, w1_ref[1])
    e_scr[...] = r0[:, 256:] + r1[:, 256:]
    ar, ai = a_ref[0:1, :], a_ref[1:2, :]

    def step(j, carry):
        re, im = carry
        r = pl.multiple_of(j * nb, nb)
        xs_scr[pl.ds(r, nb), 0:LANES] = re
        xs_scr[pl.ds(r, nb), LANES:2 * LANES] = im
        e_re = e_scr[pl.ds(r, nb), 0:LANES]
        e_im = e_scr[pl.ds(r, nb), LANES:2 * LANES]
        return ar * re - ai * im + e_re, ar * im + ai * re + e_im

    re, im = lax.fori_loop(0, rows // nb, step, (st_scr[0], st_scr[1]))
    st_scr[0] = re
    st_scr[1] = im
    sre_ref[...] = re
    sim_ref[...] = im
    xs = xs_scr[...].astype(BF16)
    y_ref[0] = (r0[:, :256] + _dot(xs, q_ref[0]) + x0.astype(F32) * d_ref[0]).astype(BF16)
    y_ref[1] = (r1[:, :256] + _dot(xs, q_ref[1]) + x1.astype(F32) * d_ref[1]).astype(BF16)


def _s5(u2, w1, qm, am, d2, x0re, x0im, *, nb, rows):
    g, total, kw = u2.shape
    npair = g // 2
    nrb = total // rows
    return pl.pallas_call(
        functools.partial(_s5_kernel, nb=nb, rows=rows),
        grid=(npair, nrb),
        in_specs=[pl.BlockSpec((2, rows, kw), lambda p, r: (p, r, 0)),
                  pl.BlockSpec((2, kw, 2 * kw), lambda p, r: (p, 0, 0)),
                  pl.BlockSpec((2, kw, kw), lambda p, r: (p, 0, 0)),
                  pl.BlockSpec((None, 8, LANES), lambda p, r: (p, 0, 0)),
                  pl.BlockSpec((2, 1, kw), lambda p, r: (p, 0, 0)),
                  pl.BlockSpec((nb, LANES), lambda p, r: (0, p)),
                  pl.BlockSpec((nb, LANES), lambda p, r: (0, p))],
        out_specs=[pl.BlockSpec((2, rows, kw), lambda p, r: (p, r, 0)),
                   pl.BlockSpec((nb, LANES), lambda p, r: (0, p)),
                   pl.BlockSpec((nb, LANES), lambda p, r: (0, p))],
        out_shape=[jax.ShapeDtypeStruct((g, total, kw), BF16),
                   jax.ShapeDtypeStruct(x0re.shape, F32),
                   jax.ShapeDtypeStruct(x0im.shape, F32)],
        scratch_shapes=[pltpu.VMEM((rows, 2 * LANES), F32),
                        pltpu.VMEM((rows, 2 * LANES), F32),
                        pltpu.VMEM((2, nb, LANES), F32)],
        compiler_params=pltpu.CompilerParams(dimension_semantics=("arbitrary", "arbitrary"),
                                             vmem_limit_bytes=VMEM_LIMIT),
        name="s5",
    )(u2, w1, qm, am, d2, x0re, x0im)


S5_SB = 4


def _s5in_kernel(x_ref, gmix_ref, wu_ref, o_ref, *, nchunk, groups, d):
    h = jnp.concatenate([_rms(x_ref[:, k * d:(k + 1) * d], gmix_ref[...]).astype(BF16)
                         for k in range(S5_SB)], axis=0)
    ut = _dot_t(wu_ref[...], h)
    for k in range(S5_SB):
        for g in range(groups):
            o_ref[g, k] = ut[g * S5_CH:(g + 1) * S5_CH, k * nchunk:(k + 1) * nchunk].astype(BF16)


def _s5in(x2d, gmix, wu_t, *, batch, seq):
    d = x2d.shape[1]
    width = wu_t.shape[0]
    groups = width // S5_CH
    nchunk = seq // S5_TC
    xr = x2d.reshape(batch, nchunk, S5_TC * d)
    return pl.pallas_call(
        functools.partial(_s5in_kernel, nchunk=nchunk, groups=groups, d=d),
        grid=(batch, S5_TC // S5_SB),
        in_specs=[pl.BlockSpec((None, nchunk, S5_SB * d), lambda b, s: (b, 0, s)),
                  _const_spec(gmix.shape), _const_spec(wu_t.shape)],
        out_specs=pl.BlockSpec((groups, S5_SB, S5_CH, nchunk), lambda b, s: (0, s, 0, b)),
        out_shape=jax.ShapeDtypeStruct((groups, S5_TC, S5_CH, batch * nchunk), BF16),
        compiler_params=pltpu.CompilerParams(dimension_semantics=("arbitrary", "arbitrary"),
                                             vmem_limit_bytes=VMEM_LIMIT),
        name="s5in",
    )(xr, gmix, wu_t)


def _s5t_kernel(x_ref, mt_ref, p_ref, qt_ref, a_ref, d_ref, x0re_ref, x0im_ref,
                y_ref, sre_ref, sim_ref, ere_scr, eim_scr, xre_scr, xim_scr, *, nb, nchunk):
    tdims = (((0,), (0,)), ((), ()))
    e = (lax.dot_general(x_ref[0], p_ref[0], tdims, preferred_element_type=F32)
         + lax.dot_general(x_ref[1], p_ref[1], tdims, preferred_element_type=F32))
    ere_scr[...] = e[:, :LANES]
    eim_scr[...] = e[:, LANES:]
    ar, ai = a_ref[0:1, :], a_ref[1:2, :]

    def step(j, carry):
        re, im = carry
        rows = pl.ds(j, nb, stride=nchunk)
        xre_scr[rows, :] = re
        xim_scr[rows, :] = im
        return ar * re - ai * im + ere_scr[rows, :], ar * im + ai * re + eim_scr[rows, :]

    re, im = lax.fori_loop(0, nchunk, step, (x0re_ref[...], x0im_ref[...]))
    sre_ref[...] = re
    sim_ref[...] = im
    xs = jnp.concatenate([xre_scr[...], xim_scr[...]], axis=1).astype(BF16)
    for g in range(2):
        y_ref[g] = (_dot(mt_ref[g], x_ref[g]) + _dot_t(qt_ref[g], xs)
                    + x_ref[g].astype(F32) * d_ref[g]).astype(BF16)


def _s5t(u2t, mt, pp, qt, am, dcol, x0re, x0im, *, nb, nchunk):
    g, kw, total = u2t.shape
    cols = nb * nchunk
    opspec = pl.BlockSpec((2, kw, kw), lambda p, r: (p, 0, 0))
    stspec = pl.BlockSpec((nb, LANES), lambda p, r: (r, p))
    return pl.pallas_call(
        functools.partial(_s5t_kernel, nb=nb, nchunk=nchunk),
        grid=(g // 2, total // cols),
        in_specs=[pl.BlockSpec((2, kw, cols), lambda p, r: (p, 0, r)),
                  opspec, opspec, opspec,
                  pl.BlockSpec((None, 8, LANES), lambda p, r: (p, 0, 0)),
                  pl.BlockSpec((2, kw, 1), lambda p, r: (p, 0, 0)),
                  stspec, stspec],
        out_specs=[pl.BlockSpec((2, kw, cols), lambda p, r: (p, 0, r)), stspec, stspec],
        out_shape=[jax.ShapeDtypeStruct(u2t.shape, BF16),
                   jax.ShapeDtypeStruct(x0re.shape, F32),
                   jax.ShapeDtypeStruct(x0im.shape, F32)],
        scratch_shapes=[pltpu.VMEM((cols, LANES), F32)] * 4,
        compiler_params=pltpu.CompilerParams(dimension_semantics=("arbitrary", "arbitrary"),
                                             vmem_limit_bytes=VMEM_LIMIT),
        name="s5t",
    )(u2t, mt, pp, qt, am, dcol, x0re, x0im)


def _s5out_kernel(y_ref, wglut_ref, b_ref, g_ref, o_ref, z_scr, *, nchunk, groups):
    width = groups * S5_CH
    for k in range(S5_SB):
        for g in range(groups):
            z_scr[g * S5_CH:(g + 1) * S5_CH, k * nchunk:(k + 1) * nchunk] = y_ref[g, k].astype(F32)
    z = _gelu(z_scr[...])
    ssm = z * _sigmoid(_dot(wglut_ref[...], z.astype(BF16)) + b_ref[...])
    sn = ssm * lax.rsqrt(jnp.mean(ssm * ssm, axis=0, keepdims=True) + EPS) * g_ref[...]
    for k in range(S5_SB):
        o_ref[:, k * width:(k + 1) * width] = sn[:, k * nchunk:(k + 1) * nchunk].T.astype(BF16)


def _s5out(y2t, wglu_t, bcol, gcol, *, batch, seq):
    groups = y2t.shape[0]
    width = groups * S5_CH
    nchunk = seq // S5_TC
    out = pl.pallas_call(
        functools.partial(_s5out_kernel, nchunk=nchunk, groups=groups),
        grid=(batch, S5_TC // S5_SB),
        in_specs=[pl.BlockSpec((groups, S5_SB, S5_CH, nchunk), lambda b, t: (0, t, 0, b)),
                  _const_spec(wglu_t.shape), _const_spec(bcol.shape), _const_spec(gcol.shape)],
        out_specs=pl.BlockSpec((None, nchunk, S5_SB * width), lambda b, t: (b, 0, t)),
        out_shape=jax.ShapeDtypeStruct((batch, nchunk, S5_TC * width), BF16),
        scratch_shapes=[pltpu.VMEM((width, S5_SB * nchunk), F32)],
        compiler_params=pltpu.CompilerParams(dimension_semantics=("arbitrary", "arbitrary"),
                                             vmem_limit_bytes=VMEM_LIMIT),
        name="s5out",
    )(y2t, wglu_t, bcol, gcol)
    return out.reshape(batch * seq, width)


def _s5_operators(a_re, a_im, log_dt, b_re, b_im, c_re, c_im, d_skip):
    hp = lax.Precision.HIGHEST
    g, p = a_re.shape
    tc, ch = S5_TC, S5_CH
    dt = jnp.exp(log_dt)[:, None]
    lr, li = a_re, a_im
    mag, ang = jnp.exp(lr * dt), li * dt
    abr, abi = mag * jnp.cos(ang), mag * jnp.sin(ang)
    den = lr * lr + li * li
    ir, ii = lr / den, -li / den
    zr, zi = (abr - 1.0) * ir - abi * ii, (abr - 1.0) * ii + abi * ir
    bbr = zr[..., None] * b_re - zi[..., None] * b_im
    bbi = zr[..., None] * b_im + zi[..., None] * b_re
    ks = jnp.arange(tc + 1, dtype=F32)[:, None, None]
    pm, pa = jnp.exp(ks * (lr * dt)), ks * ang
    pr, pi = pm * jnp.cos(pa), pm * jnp.sin(pa)
    wr = pr[..., None] * bbr - pi[..., None] * bbi
    wi = pr[..., None] * bbi + pi[..., None] * bbr
    kt = (jnp.einsum("gcp,kgpd->kgcd", c_re, wr[:tc], precision=hp)
          - jnp.einsum("gcp,kgpd->kgcd", c_im, wi[:tc], precision=hp))
    s_idx = jnp.arange(tc)[:, None]
    t_idx = jnp.arange(tc)[None, :]
    lag = jnp.clip(t_idx - s_idx, 0, tc - 1)
    m = jnp.where((t_idx >= s_idx)[:, :, None, None, None], kt[lag], 0.0)
    m = m.transpose(2, 0, 4, 1, 3).reshape(g, tc * ch, tc * ch)
    rev = wr[tc - 1 - jnp.arange(tc)], wi[tc - 1 - jnp.arange(tc)]
    p_re = rev[0].transpose(1, 0, 3, 2).reshape(g, tc * ch, p)
    p_im = rev[1].transpose(1, 0, 3, 2).reshape(g, tc * ch, p)
    q_re = (c_re[None] * pr[1:, :, None, :] - c_im[None] * pi[1:, :, None, :])
    q_im = -(c_re[None] * pi[1:, :, None, :] + c_im[None] * pr[1:, :, None, :])
    q_re = q_re.transpose(1, 3, 0, 2).reshape(g, p, tc * ch)
    q_im = q_im.transpose(1, 3, 0, 2).reshape(g, p, tc * ch)
    odd = (jnp.arange(g) % 2 == 1)[:, None, None]
    zp = jnp.zeros_like(p_re)
    w1 = jnp.concatenate([m, jnp.where(odd, zp, p_re), jnp.where(odd, p_re, zp),
                          jnp.where(odd, zp, p_im), jnp.where(odd, p_im, zp)], axis=2)
    zq = jnp.zeros_like(q_re)
    qm = jnp.concatenate([jnp.where(odd, zq, q_re), jnp.where(odd, q_re, zq),
                          jnp.where(odd, zq, q_im), jnp.where(odd, q_im, zq)], axis=1)
    am = jnp.zeros((g // 2, 8, 2 * p), F32)
    am = am.at[:, 0].set(pr[tc].reshape(g // 2, 2 * p)).at[:, 1].set(pi[tc].reshape(g // 2, 2 * p))
    d2 = jnp.tile(d_skip.reshape(g, 1, ch), (1, 1, tc))
    row_form = (w1.astype(BF16), qm.astype(BF16), am, d2)
    col_form = (m.transpose(0, 2, 1).astype(BF16), w1[:, :, tc * ch:].astype(BF16),
                qm.transpose(0, 2, 1).astype(BF16), am, d2.reshape(g, tc * ch, 1))
    return row_form, col_form


def _s5_layer(u, ops, x0_re, x0_im, *, batch, seq):
    w1, qm, am, d2 = ops
    g = w1.shape[0]
    tc, ch = S5_TC, S5_CH
    nchunk = seq // tc
    u2 = u.reshape(batch, nchunk, tc, g, ch).transpose(3, 1, 0, 2, 4).reshape(g, nchunk * batch, tc * ch)
    rows = min(nchunk, 32) * batch
    y2, sre, sim = _s5(u2, w1, qm, am, d2, x0_re, x0_im, nb=batch, rows=rows)
    y = y2.reshape(g, nchunk, batch, tc, ch).transpose(2, 1, 3, 0, 4).reshape(batch * seq, g * ch)
    return y, sre, sim


def _gelu(x):
    return 0.5 * x * (1.0 + jnp.tanh(math.sqrt(2.0 / math.pi) * (x + 0.044715 * (x * x * x))))


def _sigmoid(x):
    return 1.0 / (1.0 + jnp.exp(-x))


def _ffn_kernel(x_ref, a_ref, y_ref, wglu_ref, bglu_ref, gmla_ref, gs5_ref, wout_ref, gffn_ref,
                wgate_ref, wup_ref, wdown_ref, gfin_ref, o_ref, *, s5_activated):
    if s5_activated:
        s5n = y_ref[...]
    else:
        z = _gelu(y_ref[...].astype(F32))
        ssm = z * _sigmoid(_dot(z.astype(BF16), wglu_ref[...]) + bglu_ref[...])
        s5n = _rms(ssm, gs5_ref[...]).astype(BF16)
    mixed = jnp.concatenate([_rms(a_ref[...].astype(F32), gmla_ref[...]).astype(BF16), s5n], axis=-1)
    x1 = x_ref[...] + _dot(mixed, wout_ref[...])
    h2 = _rms(x1, gffn_ref[...]).astype(BF16)
    gate = _dot(h2, wgate_ref[...])
    act = (gate * _sigmoid(gate) * _dot(h2, wup_ref[...])).astype(BF16)
    x2 = x1 + _dot(act, wdown_ref[...])
    o_ref[...] = _rms(x2, gfin_ref[...])


def _ffn(x2d, attn, y, wglu, bglu, gmla, gs5, wout, gffn, wgate, wup, wdown, gfin, *, tm, s5_activated):
    t, d = x2d.shape
    row = lambda w: pl.BlockSpec((tm, w), lambda i: (i, 0))
    resident = lambda a: pl.BlockSpec(a.shape, lambda i: (0,) * a.ndim, pipeline_mode=pl.Buffered(1))
    consts = (wglu, bglu, gmla, gs5, wout, gffn, wgate, wup, wdown, gfin)
    return pl.pallas_call(
        functools.partial(_ffn_kernel, s5_activated=s5_activated),
        grid=(t // tm,),
        in_specs=[row(d), row(attn.shape[1]), row(y.shape[1])] + [resident(a) for a in consts],
        out_specs=row(d),
        out_shape=jax.ShapeDtypeStruct((t, d), F32),
        compiler_params=pltpu.CompilerParams(dimension_semantics=("arbitrary",),
                                             vmem_limit_bytes=VMEM_LIMIT),
        name="ffn",
    )(x2d, attn, y, *consts)


def _rope_tabs(pos, tm):
    inv = 1.0 / (ROPE_BASE ** (jnp.arange(0, QK_ROPE, 2, dtype=F32) / QK_ROPE))
    ang = pos.astype(F32)[:, None] * inv[None, :]
    cos, sin = jnp.cos(ang), jnp.sin(ang)
    n = pos.shape[0]
    half = QK_ROPE // 2
    z = lambda w: jnp.zeros((n, w), F32)
    c = jnp.concatenate([jnp.ones((n, QK_NOPE), F32), cos, cos, z(LANES - QK_NOPE - QK_ROPE)], axis=1)
    s1 = jnp.concatenate([z(QK_NOPE + half), sin, z(LANES - QK_NOPE - QK_ROPE)], axis=1)
    s2 = jnp.concatenate([z(QK_NOPE), -sin, z(LANES - QK_NOPE - half)], axis=1)
    reps = max(1, tm // n)
    return tuple(jnp.tile(t, (reps, 1)) for t in (c, s1, s2))


def _pad_heads(w, width):
    r = w.shape[0]
    return jnp.pad(w.reshape(r, HEADS, width), ((0, 0), (0, 0), (0, LANES - width))).reshape(r, HEADS * LANES)


def _layer_weights(w_in, w_uq, w_ukv, q_lora, kv_lora):
    o1, o2, o3 = q_lora, q_lora + kv_lora, q_lora + kv_lora + QK_ROPE
    d = w_in.shape[0]
    kr_cols = jnp.concatenate([jnp.zeros((d, QK_NOPE), F32), w_in[:, o2:o3],
                               jnp.zeros((d, LANES - QK_NOPE - QK_ROPE), F32)], axis=1)
    win = jnp.concatenate([w_in[:, :o2], kr_cols, w_in[:, o3:]], axis=1).astype(BF16)
    wuq = _pad_heads(w_uq, QK_NOPE + QK_ROPE).astype(BF16)
    ukv = w_ukv.reshape(kv_lora, HEADS, QK_NOPE + V_HEAD)
    w_uk, w_uv = ukv[..., :QK_NOPE], ukv[..., QK_NOPE:]
    wukv = jnp.concatenate([_pad_heads(w_uk.reshape(kv_lora, HEADS * QK_NOPE), QK_NOPE),
                            w_uv.reshape(kv_lora, HEADS * V_HEAD)], axis=1).astype(BF16)
    wabs = jnp.zeros((HEADS, LANES, kv_lora + LANES), F32)
    wabs = wabs.at[:, :QK_NOPE, :kv_lora].set(w_uk.transpose(1, 2, 0))
    wabs = wabs.at[:, QK_NOPE:QK_NOPE + QK_ROPE, kv_lora:kv_lora + QK_ROPE].set(
        jnp.broadcast_to(jnp.eye(QK_ROPE, dtype=F32), (HEADS, QK_ROPE, QK_ROPE)))
    wuv = jnp.zeros((HEADS, kv_lora, HEADS, V_HEAD), F32)
    wuv = wuv.at[jnp.arange(HEADS), :, jnp.arange(HEADS), :].set(w_uv.transpose(1, 0, 2))
    wuv = wuv.reshape(HEADS, kv_lora, HEADS * V_HEAD)
    return win, wuq, wukv, wabs.astype(BF16), wuv.astype(BF16)


def _token_tile(batch, seq):
    for tm in (512, 256, 128, 64, 32, 16):
        if (batch * seq) % tm == 0 and (seq % tm == 0 or tm % seq == 0):
            return tm
    raise ValueError(f"no token tile for batch={batch} seq={seq}")


def kernel(x_prompt, x_sample, cache_mla_ckv, cache_mla_krope, state_s5_re, state_s5_im,
           g_mix, w_in, g_q, w_uq, g_kv, w_ukv,
           s5_a_re, s5_a_im, s5_log_dt, s5_b_re, s5_b_im, s5_c_re, s5_c_im, s5_d, w_glu, b_glu,
           g_out_mla, g_out_s5, w_out, g_ffn, w_gate, w_up, w_down, g_final):
    bp, lp, d = x_prompt.shape
    bs, ls, _ = x_sample.shape
    past = cache_mla_ckv.shape[2]
    q_lora, kv_lora = g_q.shape[1], g_kv.shape[1]
    groups, state = s5_a_re.shape[1], s5_a_re.shape[2]
    assert g_mix.shape[0] == 1, "single-layer model"
    assert state == S5_STATE and s5_b_re.shape[-1] == S5_CH and groups % 2 == 0
    assert lp % S5_TC == 0 and ls % S5_TC == 0 and bp % 8 == 0 and bs % 8 == 0

    xp = x_prompt.reshape(bp * lp, d)
    xs = x_sample.reshape(bs * ls, d)
    tmp, tms = _token_tile(bp, lp), _token_tile(bs, ls)
    tabs_p = _rope_tabs(jnp.arange(lp), tmp)
    tabs_s = _rope_tabs(past + jnp.arange(ls), tms)
    tq = min(lp, ATTN_TQ)
    row2 = lambda a: a.reshape(1, -1)

    win, wuq, wukv, wabs, wuv = _layer_weights(w_in[0], w_uq[0], w_ukv[0], q_lora, kv_lora)
    mla_cols = q_lora + kv_lora + LANES
    ops_row, ops_col = _s5_operators(s5_a_re[0], s5_a_im[0], s5_log_dt[0], s5_b_re[0], s5_b_im[0],
                                     s5_c_re[0], s5_c_im[0], s5_d[0])
    wglu = w_glu[0].astype(BF16)
    ffn_w = (wglu, row2(b_glu[0]), row2(g_out_mla[0]), row2(g_out_s5[0]),
             w_out[0].astype(BF16), row2(g_ffn[0]), w_gate[0].astype(BF16),
             w_up[0].astype(BF16), w_down[0].astype(BF16), row2(g_final))
    norms = (row2(g_mix[0]), row2(g_q[0]), row2(g_kv[0]))

    q, k, v, p_ckv, p_kr = _proj(xp, tabs_p, norms[0], win[:, :mla_cols], norms[1], wuq, norms[2], wukv,
                                 tm=tmp, emit_kv=True, emit_u=False)
    attn = _attn_prompt(q, k, v, batch=bp, seq=lp, tq=tq)
    u2t = _s5in(xp, norms[0], win[:, mla_cols:].T, batch=bp, seq=lp)
    zeros = jnp.zeros((bp, groups * state), F32)
    nb = 16 if bp % 16 == 0 else 8
    y2t, p_re, p_im = _s5t(u2t.reshape(groups, S5_TC * S5_CH, -1), *ops_col, zeros, zeros,
                           nb=nb, nchunk=lp // S5_TC)
    s5n = _s5out(y2t.reshape(u2t.shape), wglu.T, b_glu[0].reshape(-1, 1), g_out_s5[0].reshape(-1, 1),
                 batch=bp, seq=lp)
    yp = _ffn(xp, attn, s5n, *ffn_w, tm=tmp, s5_activated=True)

    q, s_ckv, s_kr, u = _proj(xs, tabs_s, norms[0], win, norms[1], wuq, norms[2], wukv,
                              tm=tms, emit_kv=False, emit_u=True)
    attn = _attn_sample(q, cache_mla_ckv[0].reshape(bs * past, kv_lora),
                        cache_mla_krope[0].reshape(bs * past, QK_ROPE), s_ckv, s_kr, wabs, wuv,
                        batch=bs, dec=ls, past=past)
    y, s_re, s_im = _s5_layer(u, ops_row, state_s5_re[0].reshape(bs, groups * state),
                              state_s5_im[0].reshape(bs, groups * state), batch=bs, seq=ls)
    ys = _ffn(xs, attn, y, *ffn_w, tm=tms, s5_activated=False)

    return (yp.reshape(bp, lp, d), ys.reshape(bs, ls, d),
            p_ckv.reshape(1, bp, lp, kv_lora), p_kr.reshape(1, bp, lp, QK_ROPE),
            p_re.reshape(1, bp, groups, state), p_im.reshape(1, bp, groups, state),
            s_ckv.reshape(1, bs, ls, kv_lora), s_kr.reshape(1, bs, ls, QK_ROPE),
            s_re.reshape(1, bs, groups, state), s_im.reshape(1, bs, groups, state))
```

```python
import functools
import math

import jax
import jax.numpy as jnp
from jax import lax
from jax.experimental import pallas as pl
from jax.experimental.pallas import tpu as pltpu

F32 = jnp.float32
BF16 = jnp.bfloat16

EPS = 1e-6
NEG_INF = -1e30
CHUNK_SHIFT = 6
ATTN_TQ = 1024
ATTN_HB = 4
HEADS = 8
QK_NOPE = 64
QK_ROPE = 32
V_HEAD = 64
ROPE_BASE = 10000.0
Q_SCALE = (QK_NOPE + QK_ROPE) ** -0.5 * math.log2(math.e)
S5_CH = 16
S5_STATE = 64
S5_TC = 16
LANES = 128
VMEM_LIMIT = 56 * 1024 * 1024


def _rms(x, g):
    return x * lax.rsqrt(jnp.mean(x * x, axis=-1, keepdims=True) + EPS) * g


def _dot(a, b):
    return jnp.dot(a, b, preferred_element_type=F32)


def _dot_t(a, b):
    return lax.dot_general(a, b, (((1,), (1,)), ((), ())), preferred_element_type=F32)


def _const_spec(shape):
    return pl.BlockSpec(shape, lambda *_: (0,) * len(shape))


def _rope(t, c, s1, s2):
    return t * c + pltpu.roll(t, QK_ROPE // 2, 1) * s1 + pltpu.roll(t, LANES - QK_ROPE // 2, 1) * s2


def _proj_kernel(x_ref, c_ref, s1_ref, s2_ref, gmix_ref, win_ref, gq_ref, wuq_ref, gkv_ref, *rest,
                 q_lora, kv_lora, emit_kv, emit_u):
    rest = list(rest)
    wuk_ref, wuvt_ref = (rest.pop(0), rest.pop(0)) if emit_kv else (None, None)
    q_ref = rest.pop(0)
    k_ref, vt_ref = (rest.pop(0), rest.pop(0)) if emit_kv else (None, None)
    ckv_ref, kr_ref = rest.pop(0), rest.pop(0)
    u_ref = rest.pop(0) if emit_u else None
    o1 = q_lora
    o2 = o1 + kv_lora
    o3 = o2 + LANES
    tm = x_ref.shape[0]
    nsplit = 2 if tm % 256 == 0 else 1
    rs = tm // nsplit
    for part in range(nsplit):
        rows = slice(part * rs, (part + 1) * rs)
        h = _rms(x_ref[rows, :], gmix_ref[...]).astype(BF16)
        proj = _dot(h, win_ref[...])
        c, s1, s2 = c_ref[rows, :], s1_ref[rows, :], s2_ref[rows, :]
        cq = _rms(proj[:, :o1], gq_ref[...]).astype(BF16)
        q = _dot(cq, wuq_ref[...])
        ckv = _rms(proj[:, o1:o2], gkv_ref[...])
        ckv_ref[rows, :] = ckv
        kr = _rope(proj[:, o2:o3], c, s1, s2)
        kr_ref[rows, :] = kr[:, QK_NOPE:QK_NOPE + QK_ROPE]
        if emit_kv:
            ckv_b = ckv.astype(BF16)
            kn = _dot(ckv_b, wuk_ref[...])
            vt_ref[:, rows] = _dot_t(wuvt_ref[...], ckv_b).astype(BF16)
        for hh in range(HEADS):
            sl = slice(hh * LANES, (hh + 1) * LANES)
            q_ref[rows, sl] = (_rope(q[:, sl], c, s1, s2) * Q_SCALE).astype(BF16)
            if emit_kv:
                k_ref[rows, sl] = (kn[:, sl] + kr).astype(BF16)
        if emit_u:
            u_ref[rows, :] = proj[:, o3:].astype(BF16)


def _proj(x2d, tabs, gmix, win, gq, wuq, gkv, wuk, wuv_t, *, tm, emit_kv, emit_u):
    t, d = x2d.shape
    q_lora, kv_lora = gq.shape[1], gkv.shape[1]
    s5w = win.shape[1] - q_lora - kv_lora - LANES
    assert (s5w > 0) == emit_u
    ntab = tabs[0].shape[0] // tm
    tab_spec = pl.BlockSpec((tm, LANES), lambda i: (i % ntab, 0))
    row = lambda w: pl.BlockSpec((tm, w), lambda i: (i, 0))
    col = pl.BlockSpec((HEADS * V_HEAD, tm), lambda i: (0, i))
    specs = [row(HEADS * LANES)]
    shapes = [jax.ShapeDtypeStruct((t, HEADS * LANES), BF16)]
    if emit_kv:
        specs += [row(HEADS * LANES), col]
        shapes += [jax.ShapeDtypeStruct((t, HEADS * LANES), BF16),
                   jax.ShapeDtypeStruct((HEADS * V_HEAD, t), BF16)]
    specs += [row(kv_lora), row(QK_ROPE)]
    shapes += [jax.ShapeDtypeStruct((t, kv_lora), F32), jax.ShapeDtypeStruct((t, QK_ROPE), F32)]
    if emit_u:
        specs += [row(s5w)]
        shapes += [jax.ShapeDtypeStruct((t, s5w), BF16)]
    weights = (gmix, win, gq, wuq, gkv) + ((wuk, wuv_t) if emit_kv else ())
    return pl.pallas_call(
        functools.partial(_proj_kernel, q_lora=q_lora, kv_lora=kv_lora, emit_kv=emit_kv, emit_u=emit_u),
        grid=(t // tm,),
        in_specs=[row(d), tab_spec, tab_spec, tab_spec] + [_const_spec(w.shape) for w in weights],
        out_specs=specs,
        out_shape=shapes,
        compiler_params=pltpu.CompilerParams(dimension_semantics=("arbitrary",),
                                             vmem_limit_bytes=VMEM_LIMIT),
        name="proj",
    )(x2d, *tabs, *weights)


def _attn_p_kernel(q_ref, k_ref, vt_ref, o_ref, *, tq):
    qi = pl.program_id(2)

    def tile(kt, carry, vis):
        ks = pl.multiple_of(kt * tq, tq)
        out = []
        for hh in range(ATTN_HB):
            m, a = carry[hh]
            sl = slice(hh * LANES, (hh + 1) * LANES)
            st = _dot_t(k_ref[pl.ds(ks, tq), sl], q_ref[:, sl])
            if vis is not None:
                st = jnp.where(vis, st, NEG_INF)
            mn = jnp.maximum(m, jnp.max(st, axis=0, keepdims=True))
            p = jnp.exp2(st - mn).astype(BF16)
            alpha = jnp.exp2(m - mn)
            vt = jnp.concatenate([vt_ref[hh * V_HEAD:(hh + 1) * V_HEAD, pl.ds(ks, tq)], ones], axis=0)
            out.append((mn, alpha * a + _dot(vt, p)))
        return tuple(out)

    ones = jnp.ones((16, tq), BF16)
    init = (jnp.full((1, tq), NEG_INF, F32), jnp.zeros((V_HEAD + 16, tq), F32))
    carry = lax.fori_loop(0, qi, lambda kt, c: tile(kt, c, None), (init,) * ATTN_HB)
    kchunk = lax.broadcasted_iota(jnp.int32, (tq, tq), 0) >> CHUNK_SHIFT
    qchunk = lax.broadcasted_iota(jnp.int32, (tq, tq), 1) >> CHUNK_SHIFT
    carry = tile(qi, carry, kchunk <= qchunk)
    ot = jnp.concatenate([a[:V_HEAD] / a[V_HEAD:V_HEAD + 1] for _, a in carry], axis=0)
    o_ref[...] = ot.T.astype(BF16)


def _attn_prompt(q, k, vt, *, batch, seq, tq):
    assert seq % tq == 0 and tq % (1 << CHUNK_SHIFT) == 0 and HEADS % ATTN_HB == 0
    nq = seq // tq
    return pl.pallas_call(
        functools.partial(_attn_p_kernel, tq=tq),
        grid=(batch, HEADS // ATTN_HB, nq),
        in_specs=[pl.BlockSpec((tq, ATTN_HB * LANES), lambda b, h, i: (b * nq + i, h)),
                  pl.BlockSpec((seq, ATTN_HB * LANES), lambda b, h, i: (b, h)),
                  pl.BlockSpec((ATTN_HB * V_HEAD, seq), lambda b, h, i: (h, b))],
        out_specs=pl.BlockSpec((tq, ATTN_HB * V_HEAD), lambda b, h, i: (b * nq + i, h)),
        out_shape=jax.ShapeDtypeStruct((batch * seq, HEADS * V_HEAD), BF16),
        compiler_params=pltpu.CompilerParams(
            dimension_semantics=("arbitrary", "arbitrary", "arbitrary"),
            vmem_limit_bytes=VMEM_LIMIT),
        name="attn_p",
    )(q, k, vt)


def _attn_s_kernel(q_ref, pc_ref, pk_ref, nc_ref, nk_ref, wabs_ref, wuv_ref, o_ref, *,
                   dec, past, kv_lora):
    q = q_ref[...]
    qf = jnp.concatenate(
        [_dot(q[:, hh * LANES:(hh + 1) * LANES], wabs_ref[hh]) for hh in range(HEADS)],
        axis=0).astype(BF16)
    qa, qr = qf[:, :kv_lora], qf[:, kv_lora:kv_lora + QK_ROPE]
    pc = pc_ref[...].astype(BF16)
    nc = nc_ref[...].astype(BF16)
    s_p = _dot_t(qa, pc) + _dot_t(qr, pk_ref[...].astype(BF16))
    s_n = _dot_t(qa, nc) + _dot_t(qr, nk_ref[...].astype(BF16))
    rows = HEADS * dec
    qpos = past + jnp.concatenate([lax.broadcasted_iota(jnp.int32, (dec, 1), 0)] * HEADS, axis=0)
    qchunk = qpos >> CHUNK_SHIFT
    kchunk_p = lax.broadcasted_iota(jnp.int32, (rows, past), 1) >> CHUNK_SHIFT
    kchunk_n = (past + lax.broadcasted_iota(jnp.int32, (rows, dec), 1)) >> CHUNK_SHIFT
    s_p = jnp.where(kchunk_p <= qchunk, s_p, NEG_INF)
    s_n = jnp.where(kchunk_n <= qchunk, s_n, NEG_INF)
    m = jnp.maximum(jnp.max(s_p, axis=-1, keepdims=True), jnp.max(s_n, axis=-1, keepdims=True))
    p_p = jnp.exp2(s_p - m)
    p_n = jnp.exp2(s_n - m)
    l = jnp.sum(p_p, axis=-1, keepdims=True) + jnp.sum(p_n, axis=-1, keepdims=True)
    olat = ((_dot(p_p.astype(BF16), pc) + _dot(p_n.astype(BF16), nc)) / l).astype(BF16)
    acc = jnp.zeros((dec, HEADS * V_HEAD), F32)
    for hh in range(HEADS):
        acc = acc + _dot(olat[hh * dec:(hh + 1) * dec], wuv_ref[hh])
    o_ref[...] = acc.astype(BF16)


def _attn_sample(q, past_ckv, past_kr, new_ckv, new_kr, wabs, wuv, *, batch, dec, past):
    kv_lora = past_ckv.shape[-1]
    return pl.pallas_call(
        functools.partial(_attn_s_kernel, dec=dec, past=past, kv_lora=kv_lora),
        grid=(batch,),
        in_specs=[pl.BlockSpec((dec, HEADS * LANES), lambda b: (b, 0)),
                  pl.BlockSpec((past, kv_lora), lambda b: (b, 0)),
                  pl.BlockSpec((past, QK_ROPE), lambda b: (b, 0)),
                  pl.BlockSpec((dec, kv_lora), lambda b: (b, 0)),
                  pl.BlockSpec((dec, QK_ROPE), lambda b: (b, 0)),
                  _const_spec(wabs.shape), _const_spec(wuv.shape)],
        out_specs=pl.BlockSpec((dec, HEADS * V_HEAD), lambda b: (b, 0)),
        out_shape=jax.ShapeDtypeStruct((batch * dec, HEADS * V_HEAD), BF16),
        compiler_params=pltpu.CompilerParams(dimension_semantics=("arbitrary",),
                                             vmem_limit_bytes=VMEM_LIMIT),
        name="attn_s",
    )(q, past_ckv, past_kr, new_ckv, new_kr, wabs, wuv)


def _s5_kernel(x_ref, w1_ref, q_ref, a_ref, d_ref, x0re_ref, x0im_ref,
               y_ref, sre_ref, sim_ref, e_scr, xs_scr, st_scr, *, nb, rows):
    @pl.when(pl.program_id(1) == 0)
    def _():
        st_scr[0] = x0re_ref[...]
        st_scr[1] = x0im_ref[...]

    x0, x1 = x_ref[0], x_ref[1]
    r0 = _dot(x0, w1_ref[0])
    r1 = _dot(x1, w1_ref[1])
    e_scr[...] = r0[:, 256:] + r1[:, 256:]
    ar, ai = a_ref[0:1, :], a_ref[1:2, :]

    def step(j, carry):
        re, im = carry
        r = pl.multiple_of(j * nb, nb)
        xs_scr[pl.ds(r, nb), 0:LANES] = re
        xs_scr[pl.ds(r, nb), LANES:2 * LANES] = im
        e_re = e_scr[pl.ds(r, nb), 0:LANES]
        e_im = e_scr[pl.ds(r, nb), LANES:2 * LANES]
        return ar * re - ai * im + e_re, ar * im + ai * re + e_im

    re, im = lax.fori_loop(0, rows // nb, step, (st_scr[0], st_scr[1]))
    st_scr[0] = re
    st_scr[1] = im
    sre_ref[...] = re
    sim_ref[...] = im
    xs = xs_scr[...].astype(BF16)
    y_ref[0] = (r0[:, :256] + _dot(xs, q_ref[0]) + x0.astype(F32) * d_ref[0]).astype(BF16)
    y_ref[1] = (r1[:, :256] + _dot(xs, q_ref[1]) + x1.astype(F32) * d_ref[1]).astype(BF16)


def _s5(u2, w1, qm, am, d2, x0re, x0im, *, nb, rows):
    g, total, kw = u2.shape
    npair = g // 2
    nrb = total // rows
    return pl.pallas_call(
        functools.partial(_s5_kernel, nb=nb, rows=rows),
        grid=(npair, nrb),
        in_specs=[pl.BlockSpec((2, rows, kw), lambda p, r: (p, r, 0)),
                  pl.BlockSpec((2, kw, 2 * kw), lambda p, r: (p, 0, 0)),
                  pl.BlockSpec((2, kw, kw), lambda p, r: (p, 0, 0)),
                  pl.BlockSpec((None, 8, LANES), lambda p, r: (p, 0, 0)),
                  pl.BlockSpec((2, 1, kw), lambda p, r: (p, 0, 0)),
                  pl.BlockSpec((nb, LANES), lambda p, r: (0, p)),
                  pl.BlockSpec((nb, LANES), lambda p, r: (0, p))],
        out_specs=[pl.BlockSpec((2, rows, kw), lambda p, r: (p, r, 0)),
                   pl.BlockSpec((nb, LANES), lambda p, r: (0, p)),
                   pl.BlockSpec((nb, LANES), lambda p, r: (0, p))],
        out_shape=[jax.ShapeDtypeStruct((g, total, kw), BF16),
                   jax.ShapeDtypeStruct(x0re.shape, F32),
                   jax.ShapeDtypeStruct(x0im.shape, F32)],
        scratch_shapes=[pltpu.VMEM((rows, 2 * LANES), F32),
                        pltpu.VMEM((rows, 2 * LANES), F32),
                        pltpu.VMEM((2, nb, LANES), F32)],
        compiler_params=pltpu.CompilerParams(dimension_semantics=("arbitrary", "arbitrary"),
                                             vmem_limit_bytes=VMEM_LIMIT),
        name="s5",
    )(u2, w1, qm, am, d2, x0re, x0im)


S5_SB = 4


def _s5in_kernel(*refs, nchunk, groups, ncol):
    x_refs = refs[:ncol]
    gmix_ref, wu_ref, o_ref, hp_scr = refs[ncol:]
    for s in range(S5_TC):
        xs = jnp.concatenate([xr[pl.ds(s, nchunk, stride=S5_TC), :] for xr in x_refs], axis=1)
        hp_scr[s * nchunk:(s + 1) * nchunk, :] = _rms(xs, gmix_ref[...]).astype(BF16)
    for sb in range(S5_TC // S5_SB):
        rows = slice(sb * S5_SB * nchunk, (sb + 1) * S5_SB * nchunk)
        ut = _dot_t(wu_ref[...], hp_scr[rows, :])
        for k in range(S5_SB):
            for g in range(groups):
                o_ref[g, sb * S5_SB + k] = (
                    ut[g * S5_CH:(g + 1) * S5_CH, k * nchunk:(k + 1) * nchunk].astype(BF16))


def _s5in(x2d, gmix, wu_t, *, batch, seq):
    d = x2d.shape[1]
    width = wu_t.shape[0]
    groups = width // S5_CH
    nchunk = seq // S5_TC
    ncol = d // LANES
    x_specs = [pl.BlockSpec((seq, LANES), functools.partial(lambda b, c: (b, c), c=c)) for c in range(ncol)]
    return pl.pallas_call(
        functools.partial(_s5in_kernel, nchunk=nchunk, groups=groups, ncol=ncol),
        grid=(batch,),
        in_specs=x_specs + [_const_spec(gmix.shape), _const_spec(wu_t.shape)],
        out_specs=pl.BlockSpec((groups, S5_TC, S5_CH, nchunk), lambda b: (0, 0, 0, b)),
        out_shape=jax.ShapeDtypeStruct((groups, S5_TC, S5_CH, batch * nchunk), BF16),
        scratch_shapes=[pltpu.VMEM((seq, d), BF16)],
        compiler_params=pltpu.CompilerParams(dimension_semantics=("arbitrary",),
                                             vmem_limit_bytes=VMEM_LIMIT),
        name="s5in",
    )(*([x2d] * ncol), gmix, wu_t)


def _s5t_kernel(x_ref, mt_ref, p_ref, qt_ref, a_ref, d_ref, x0re_ref, x0im_ref,
                y_ref, sre_ref, sim_ref, ere_scr, eim_scr, xre_scr, xim_scr, *, nb, nchunk):
    tdims = (((0,), (0,)), ((), ()))
    e = (lax.dot_general(x_ref[0], p_ref[0], tdims, preferred_element_type=F32)
         + lax.dot_general(x_ref[1], p_ref[1], tdims, preferred_element_type=F32))
    ere_scr[...] = e[:, :LANES]
    eim_scr[...] = e[:, LANES:]
    ar, ai = a_ref[0:1, :], a_ref[1:2, :]

    def step(j, carry):
        re, im = carry
        rows = pl.ds(j, nb, stride=nchunk)
        xre_scr[rows, :] = re
        xim_scr[rows, :] = im
        return ar * re - ai * im + ere_scr[rows, :], ar * im + ai * re + eim_scr[rows, :]

    re, im = lax.fori_loop(0, nchunk, step, (x0re_ref[...], x0im_ref[...]),
                           unroll=8 if nchunk % 8 == 0 else 1)
    sre_ref[...] = re
    sim_ref[...] = im
    xs = jnp.concatenate([xre_scr[...], xim_scr[...]], axis=1).astype(BF16)
    for g in range(2):
        y_ref[g] = (_dot(mt_ref[g], x_ref[g]) + _dot_t(qt_ref[g], xs)
                    + x_ref[g].astype(F32) * d_ref[g]).astype(BF16)


def _s5t(u2t, mt, pp, qt, am, dcol, x0re, x0im, *, nb, nchunk):
    g, kw, total = u2t.shape
    cols = nb * nchunk
    opspec = pl.BlockSpec((2, kw, kw), lambda p, r: (p, 0, 0))
    stspec = pl.BlockSpec((nb, LANES), lambda p, r: (r, p))
    return pl.pallas_call(
        functools.partial(_s5t_kernel, nb=nb, nchunk=nchunk),
        grid=(g // 2, total // cols),
        in_specs=[pl.BlockSpec((2, kw, cols), lambda p, r: (p, 0, r)),
                  opspec, opspec, opspec,
                  pl.BlockSpec((None, 8, LANES), lambda p, r: (p, 0, 0)),
                  pl.BlockSpec((2, kw, 1), lambda p, r: (p, 0, 0)),
                  stspec, stspec],
        out_specs=[pl.BlockSpec((2, kw, cols), lambda p, r: (p, 0, r)), stspec, stspec],
        out_shape=[jax.ShapeDtypeStruct(u2t.shape, BF16),
                   jax.ShapeDtypeStruct(x0re.shape, F32),
                   jax.ShapeDtypeStruct(x0im.shape, F32)],
        scratch_shapes=[pltpu.VMEM((cols, LANES), F32)] * 4,
        compiler_params=pltpu.CompilerParams(dimension_semantics=("arbitrary", "arbitrary"),
                                             vmem_limit_bytes=VMEM_LIMIT),
        name="s5t",
    )(u2t, mt, pp, qt, am, dcol, x0re, x0im)


def _s5out_kernel(y_ref, wglut_ref, b_ref, g_ref, o_ref, z_scr, t_scr, *, nchunk, groups):
    width = groups * S5_CH
    for tb in range(S5_TC // S5_SB):
        for k in range(S5_SB):
            for g in range(groups):
                z_scr[g * S5_CH:(g + 1) * S5_CH, k * nchunk:(k + 1) * nchunk] = (
                    y_ref[g, tb * S5_SB + k].astype(F32))
        z = _gelu(z_scr[...])
        ssm = z * _sigmoid(_dot(wglut_ref[...], z.astype(BF16)) + b_ref[...])
        sn = ssm * lax.rsqrt(jnp.mean(ssm * ssm, axis=0, keepdims=True) + EPS) * g_ref[...]
        for k in range(S5_SB):
            snt = sn[:, k * nchunk:(k + 1) * nchunk].T
            for c in range(width // LANES):
                t_scr[c, pl.ds(tb * S5_SB + k, nchunk, stride=S5_TC), :] = snt[:, c * LANES:(c + 1) * LANES]
    for c in range(width // LANES):
        o_ref[:, c * LANES:(c + 1) * LANES] = t_scr[c].astype(BF16)


def _s5out(y2t, wglu_t, bcol, gcol, *, batch, seq):
    groups = y2t.shape[0]
    width = groups * S5_CH
    nchunk = seq // S5_TC
    return pl.pallas_call(
        functools.partial(_s5out_kernel, nchunk=nchunk, groups=groups),
        grid=(batch,),
        in_specs=[pl.BlockSpec((groups, S5_TC, S5_CH, nchunk), lambda b: (0, 0, 0, b)),
                  _const_spec(wglu_t.shape), _const_spec(bcol.shape), _const_spec(gcol.shape)],
        out_specs=pl.BlockSpec((seq, width), lambda b: (b, 0)),
        out_shape=jax.ShapeDtypeStruct((batch * seq, width), BF16),
        scratch_shapes=[pltpu.VMEM((width, S5_SB * nchunk), F32),
                        pltpu.VMEM((width // LANES, seq, LANES), F32)],
        compiler_params=pltpu.CompilerParams(dimension_semantics=("arbitrary",),
                                             vmem_limit_bytes=VMEM_LIMIT),
        name="s5out",
    )(y2t, wglu_t, bcol, gcol)


def _s5_operators(a_re, a_im, log_dt, b_re, b_im, c_re, c_im, d_skip):
    hp = lax.Precision.HIGHEST
    g, p = a_re.shape
    tc, ch = S5_TC, S5_CH
    dt = jnp.exp(log_dt)[:, None]
    lr, li = a_re, a_im
    mag, ang = jnp.exp(lr * dt), li * dt
    abr, abi = mag * jnp.cos(ang), mag * jnp.sin(ang)
    den = lr * lr + li * li
    ir, ii = lr / den, -li / den
    zr, zi = (abr - 1.0) * ir - abi * ii, (abr - 1.0) * ii + abi * ir
    bbr = zr[..., None] * b_re - zi[..., None] * b_im
    bbi = zr[..., None] * b_im + zi[..., None] * b_re
    ks = jnp.arange(tc + 1, dtype=F32)[:, None, None]
    pm, pa = jnp.exp(ks * (lr * dt)), ks * ang
    pr, pi = pm * jnp.cos(pa), pm * jnp.sin(pa)
    wr = pr[..., None] * bbr - pi[..., None] * bbi
    wi = pr[..., None] * bbi + pi[..., None] * bbr
    kt = (jnp.einsum("gcp,kgpd->kgcd", c_re, wr[:tc], precision=hp)
          - jnp.einsum("gcp,kgpd->kgcd", c_im, wi[:tc], precision=hp))
    s_idx = jnp.arange(tc)[:, None]
    t_idx = jnp.arange(tc)[None, :]
    lag = jnp.clip(t_idx - s_idx, 0, tc - 1)
    m = jnp.where((t_idx >= s_idx)[:, :, None, None, None], kt[lag], 0.0)
    m = m.transpose(2, 0, 4, 1, 3).reshape(g, tc * ch, tc * ch)
    rev = wr[tc - 1 - jnp.arange(tc)], wi[tc - 1 - jnp.arange(tc)]
    p_re = rev[0].transpose(1, 0, 3, 2).reshape(g, tc * ch, p)
    p_im = rev[1].transpose(1, 0, 3, 2).reshape(g, tc * ch, p)
    q_re = (c_re[None] * pr[1:, :, None, :] - c_im[None] * pi[1:, :, None, :])
    q_im = -(c_re[None] * pi[1:, :, None, :] + c_im[None] * pr[1:, :, None, :])
    q_re = q_re.transpose(1, 3, 0, 2).reshape(g, p, tc * ch)
    q_im = q_im.transpose(1, 3, 0, 2).reshape(g, p, tc * ch)
    odd = (jnp.arange(g) % 2 == 1)[:, None, None]
    zp = jnp.zeros_like(p_re)
    w1 = jnp.concatenate([m, jnp.where(odd, zp, p_re), jnp.where(odd, p_re, zp),
                          jnp.where(odd, zp, p_im), jnp.where(odd, p_im, zp)], axis=2)
    zq = jnp.zeros_like(q_re)
    qm = jnp.concatenate([jnp.where(odd, zq, q_re), jnp.where(odd, q_re, zq),
                          jnp.where(odd, zq, q_im), jnp.where(odd, q_im, zq)], axis=1)
    am = jnp.zeros((g // 2, 8, 2 * p), F32)
    am = am.at[:, 0].set(pr[tc].reshape(g // 2, 2 * p)).at[:, 1].set(pi[tc].reshape(g // 2, 2 * p))
    d2 = jnp.tile(d_skip.reshape(g, 1, ch), (1, 1, tc))
    row_form = (w1.astype(BF16), qm.astype(BF16), am, d2)
    col_form = (m.transpose(0, 2, 1).astype(BF16), w1[:, :, tc * ch:].astype(BF16),
                qm.transpose(0, 2, 1).astype(BF16), am, d2.reshape(g, tc * ch, 1))
    return row_form, col_form


def _s5_layer(u, ops, x0_re, x0_im, *, batch, seq):
    w1, qm, am, d2 = ops
    g = w1.shape[0]
    tc, ch = S5_TC, S5_CH
    nchunk = seq // tc
    u2 = u.reshape(batch, nchunk, tc, g, ch).transpose(3, 1, 0, 2, 4).reshape(g, nchunk * batch, tc * ch)
    rows = min(nchunk, 32) * batch
    y2, sre, sim = _s5(u2, w1, qm, am, d2, x0_re, x0_im, nb=batch, rows=rows)
    y = y2.reshape(g, nchunk, batch, tc, ch).transpose(2, 1, 3, 0, 4).reshape(batch * seq, g * ch)
    return y, sre, sim


def _gelu(x):
    return 0.5 * x * (1.0 + jnp.tanh(math.sqrt(2.0 / math.pi) * (x + 0.044715 * (x * x * x))))


def _sigmoid(x):
    return 1.0 / (1.0 + jnp.exp(-x))


def _ffn_kernel(x_ref, a_ref, y_ref, wglu_ref, bglu_ref, gmla_ref, gs5_ref, wout_ref, gffn_ref,
                wgate_ref, wup_ref, wdown_ref, gfin_ref, o_ref, *, s5_activated):
    if s5_activated:
        s5n = y_ref[...]
    else:
        z = _gelu(y_ref[...].astype(F32))
        ssm = z * _sigmoid(_dot(z.astype(BF16), wglu_ref[...]) + bglu_ref[...])
        s5n = _rms(ssm, gs5_ref[...]).astype(BF16)
    mixed = jnp.concatenate([_rms(a_ref[...].astype(F32), gmla_ref[...]).astype(BF16), s5n], axis=-1)
    x1 = x_ref[...] + _dot(mixed, wout_ref[...])
    h2 = _rms(x1, gffn_ref[...]).astype(BF16)
    gate = _dot(h2, wgate_ref[...])
    act = (gate * _sigmoid(gate) * _dot(h2, wup_ref[...])).astype(BF16)
    x2 = x1 + _dot(act, wdown_ref[...])
    o_ref[...] = _rms(x2, gfin_ref[...])


def _ffn(x2d, attn, y, wglu, bglu, gmla, gs5, wout, gffn, wgate, wup, wdown, gfin, *, tm, s5_activated):
    t, d = x2d.shape
    row = lambda w: pl.BlockSpec((tm, w), lambda i: (i, 0))
    resident = lambda a: pl.BlockSpec(a.shape, lambda i: (0,) * a.ndim, pipeline_mode=pl.Buffered(1))
    consts = (wglu, bglu, gmla, gs5, wout, gffn, wgate, wup, wdown, gfin)
    return pl.pallas_call(
        functools.partial(_ffn_kernel, s5_activated=s5_activated),
        grid=(t // tm,),
        in_specs=[row(d), row(attn.shape[1]), row(y.shape[1])] + [resident(a) for a in consts],
        out_specs=row(d),
        out_shape=jax.ShapeDtypeStruct((t, d), F32),
        compiler_params=pltpu.CompilerParams(dimension_semantics=("arbitrary",),
                                             vmem_limit_bytes=VMEM_LIMIT),
        name="ffn",
    )(x2d, attn, y, *consts)


def _rope_tabs(pos, tm):
    inv = 1.0 / (ROPE_BASE ** (jnp.arange(0, QK_ROPE, 2, dtype=F32) / QK_ROPE))
    ang = pos.astype(F32)[:, None] * inv[None, :]
    cos, sin = jnp.cos(ang), jnp.sin(ang)
    n = pos.shape[0]
    half = QK_ROPE // 2
    z = lambda w: jnp.zeros((n, w), F32)
    c = jnp.concatenate([jnp.ones((n, QK_NOPE), F32), cos, cos, z(LANES - QK_NOPE - QK_ROPE)], axis=1)
    s1 = jnp.concatenate([z(QK_NOPE + half), sin, z(LANES - QK_NOPE - QK_ROPE)], axis=1)
    s2 = jnp.concatenate([z(QK_NOPE), -sin, z(LANES - QK_NOPE - half)], axis=1)
    reps = max(1, tm // n)
    return tuple(jnp.tile(t, (reps, 1)) for t in (c, s1, s2))


def _pad_heads(w, width):
    r = w.shape[0]
    return jnp.pad(w.reshape(r, HEADS, width), ((0, 0), (0, 0), (0, LANES - width))).reshape(r, HEADS * LANES)


def _layer_weights(w_in, w_uq, w_ukv, q_lora, kv_lora):
    o1, o2, o3 = q_lora, q_lora + kv_lora, q_lora + kv_lora + QK_ROPE
    d = w_in.shape[0]
    kr_cols = jnp.concatenate([jnp.zeros((d, QK_NOPE), F32), w_in[:, o2:o3],
                               jnp.zeros((d, LANES - QK_NOPE - QK_ROPE), F32)], axis=1)
    win = jnp.concatenate([w_in[:, :o2], kr_cols, w_in[:, o3:]], axis=1).astype(BF16)
    wuq = _pad_heads(w_uq, QK_NOPE + QK_ROPE).astype(BF16)
    ukv = w_ukv.reshape(kv_lora, HEADS, QK_NOPE + V_HEAD)
    w_uk, w_uv = ukv[..., :QK_NOPE], ukv[..., QK_NOPE:]
    wuk = _pad_heads(w_uk.reshape(kv_lora, HEADS * QK_NOPE), QK_NOPE).astype(BF16)
    wuv_t = w_uv.reshape(kv_lora, HEADS * V_HEAD).T.astype(BF16)
    wabs = jnp.zeros((HEADS, LANES, kv_lora + LANES), F32)
    wabs = wabs.at[:, :QK_NOPE, :kv_lora].set(w_uk.transpose(1, 2, 0))
    wabs = wabs.at[:, QK_NOPE:QK_NOPE + QK_ROPE, kv_lora:kv_lora + QK_ROPE].set(
        jnp.broadcast_to(jnp.eye(QK_ROPE, dtype=F32), (HEADS, QK_ROPE, QK_ROPE)))
    wuv = jnp.zeros((HEADS, kv_lora, HEADS, V_HEAD), F32)
    wuv = wuv.at[jnp.arange(HEADS), :, jnp.arange(HEADS), :].set(w_uv.transpose(1, 0, 2))
    wuv = wuv.reshape(HEADS, kv_lora, HEADS * V_HEAD)
    return win, wuq, wuk, wuv_t, wabs.astype(BF16), wuv.astype(BF16)


def _token_tile(batch, seq):
    for tm in (512, 256, 128, 64, 32, 16):
        if (batch * seq) % tm == 0 and (seq % tm == 0 or tm % seq == 0):
            return tm
    raise ValueError(f"no token tile for batch={batch} seq={seq}")


def kernel(x_prompt, x_sample, cache_mla_ckv, cache_mla_krope, state_s5_re, state_s5_im,
           g_mix, w_in, g_q, w_uq, g_kv, w_ukv,
           s5_a_re, s5_a_im, s5_log_dt, s5_b_re, s5_b_im, s5_c_re, s5_c_im, s5_d, w_glu, b_glu,
           g_out_mla, g_out_s5, w_out, g_ffn, w_gate, w_up, w_down, g_final):
    bp, lp, d = x_prompt.shape
    bs, ls, _ = x_sample.shape
    past = cache_mla_ckv.shape[2]
    q_lora, kv_lora = g_q.shape[1], g_kv.shape[1]
    groups, state = s5_a_re.shape[1], s5_a_re.shape[2]
    assert g_mix.shape[0] == 1, "single-layer model"
    assert state == S5_STATE and s5_b_re.shape[-1] == S5_CH and groups % 2 == 0
    assert lp % S5_TC == 0 and ls % S5_TC == 0 and bp % 8 == 0 and bs % 8 == 0

    xp = x_prompt.reshape(bp * lp, d)
    xs = x_sample.reshape(bs * ls, d)
    tmp, tms = _token_tile(bp, lp), _token_tile(bs, ls)
    tabs_p = _rope_tabs(jnp.arange(lp), tmp)
    tabs_s = _rope_tabs(past + jnp.arange(ls), tms)
    tq = min(lp, ATTN_TQ)
    row2 = lambda a: a.reshape(1, -1)

    win, wuq, wuk, wuv_t, wabs, wuv = _layer_weights(w_in[0], w_uq[0], w_ukv[0], q_lora, kv_lora)
    mla_cols = q_lora + kv_lora + LANES
    ops_row, ops_col = _s5_operators(s5_a_re[0], s5_a_im[0], s5_log_dt[0], s5_b_re[0], s5_b_im[0],
                                     s5_c_re[0], s5_c_im[0], s5_d[0])
    wglu = w_glu[0].astype(BF16)
    ffn_w = (wglu, row2(b_glu[0]), row2(g_out_mla[0]), row2(g_out_s5[0]),
             w_out[0].astype(BF16), row2(g_ffn[0]), w_gate[0].astype(BF16),
             w_up[0].astype(BF16), w_down[0].astype(BF16), row2(g_final))
    norms = (row2(g_mix[0]), row2(g_q[0]), row2(g_kv[0]))

    q, k, vt, p_ckv, p_kr = _proj(xp, tabs_p, norms[0], win[:, :mla_cols], norms[1], wuq, norms[2],
                                  wuk, wuv_t, tm=tmp, emit_kv=True, emit_u=False)
    attn = _attn_prompt(q, k, vt, batch=bp, seq=lp, tq=tq)
    u2t = _s5in(xp, norms[0], win[:, mla_cols:].T, batch=bp, seq=lp)
    zeros = jnp.zeros((bp, groups * state), F32)
    nb = 16 if bp % 16 == 0 else 8
    y2t, p_re, p_im = _s5t(u2t.reshape(groups, S5_TC * S5_CH, -1), *ops_col, zeros, zeros,
                           nb=nb, nchunk=lp // S5_TC)
    s5n = _s5out(y2t.reshape(u2t.shape), wglu.T, b_glu[0].reshape(-1, 1), g_out_s5[0].reshape(-1, 1),
                 batch=bp, seq=lp)
    yp = _ffn(xp, attn, s5n, *ffn_w, tm=tmp, s5_activated=True)

    q, s_ckv, s_kr, u = _proj(xs, tabs_s, norms[0], win, norms[1], wuq, norms[2], wuk, wuv_t,
                              tm=tms, emit_kv=False, emit_u=True)
    attn = _attn_sample(q, cache_mla_ckv[0].reshape(bs * past, kv_lora),
                        cache_mla_krope[0].reshape(bs * past, QK_ROPE), s_ckv, s_kr, wabs, wuv,
                        batch=bs, dec=ls, past=past)
    y, s_re, s_im = _s5_layer(u, ops_row, state_s5_re[0].reshape(bs, groups * state),
                              state_s5_im[0].reshape(bs, groups * state), batch=bs, seq=ls)
    ys = _ffn(xs, attn, y, *ffn_w, tm=tms, s5_activated=False)

    return (yp.reshape(bp, lp, d), ys.reshape(bs, ls, d),
            p_ckv.reshape(1, bp, lp, kv_lora), p_kr.reshape(1, bp, lp, QK_ROPE),
            p_re.reshape(1, bp, groups, state), p_im.reshape(1, bp, groups, state),
            s_ckv.reshape(1, bs, ls, kv_lora), s_kr.reshape(1, bs, ls, QK_ROPE),
            s_re.reshape(1, bs, groups, state), s_im.reshape(1, bs, groups, state))
```

```python
import functools
import math

import jax
import jax.numpy as jnp
from jax import lax
from jax.experimental import pallas as pl
from jax.experimental.pallas import tpu as pltpu

F32 = jnp.float32
BF16 = jnp.bfloat16

EPS = 1e-6
NEG_INF = -1e30
CHUNK_SHIFT = 6
ATTN_TQ = 1024
ATTN_HB = 4
HEADS = 8
QK_NOPE = 64
QK_ROPE = 32
V_HEAD = 64
ROPE_BASE = 10000.0
Q_SCALE = (QK_NOPE + QK_ROPE) ** -0.5 * math.log2(math.e)
S5_CH = 16
S5_STATE = 64
S5_TC = 16
S5_KW = S5_TC * S5_CH
S5_SB = 4
LANES = 128
VMEM_LIMIT = 56 * 1024 * 1024


def _rms(x, g):
    return x * lax.rsqrt(jnp.mean(x * x, axis=-1, keepdims=True) + EPS) * g


def _dot(a, b, precision=None):
    return jnp.dot(a, b, preferred_element_type=F32, precision=precision)


def _dot_t(a, b):
    return lax.dot_general(a, b, (((1,), (1,)), ((), ())), preferred_element_type=F32)


def _const_spec(shape):
    return pl.BlockSpec(shape, lambda *_: (0,) * len(shape))


def _params(*semantics):
    return pltpu.CompilerParams(dimension_semantics=semantics, vmem_limit_bytes=VMEM_LIMIT)


def _rope(t, c, s1, s2):
    return t * c + pltpu.roll(t, QK_ROPE // 2, 1) * s1 + pltpu.roll(t, LANES - QK_ROPE // 2, 1) * s2


def _proj_kernel(x_ref, c_ref, s1_ref, s2_ref, gmix_ref, win_ref, gq_ref, wuq_ref, gkv_ref, *rest,
                 q_lora, kv_lora, emit_kv):
    rest = list(rest)
    wuk_ref, wuvt_ref = (rest.pop(0), rest.pop(0)) if emit_kv else (None, None)
    q_ref = rest.pop(0)
    k_ref, vt_ref = (rest.pop(0), rest.pop(0)) if emit_kv else (None, None)
    ckv_ref, kr_ref = rest
    o1 = q_lora
    o2 = o1 + kv_lora
    h = _rms(x_ref[...], gmix_ref[...]).astype(BF16)
    proj = _dot(h, win_ref[...])
    c, s1, s2 = c_ref[...], s1_ref[...], s2_ref[...]
    cq = _rms(proj[:, :o1], gq_ref[...]).astype(BF16)
    q = _dot(cq, wuq_ref[...])
    ckv = _rms(proj[:, o1:o2], gkv_ref[...])
    ckv_ref[...] = ckv
    kr = _rope(proj[:, o2:], c, s1, s2)
    kr_ref[...] = kr[:, QK_NOPE:QK_NOPE + QK_ROPE]
    if emit_kv:
        ckv_b = ckv.astype(BF16)
        kn = _dot(ckv_b, wuk_ref[...])
        vt_ref[...] = _dot_t(wuvt_ref[...], ckv_b).astype(BF16)
    for hh in range(HEADS):
        sl = slice(hh * LANES, (hh + 1) * LANES)
        q_ref[:, sl] = (_rope(q[:, sl], c, s1, s2) * Q_SCALE).astype(BF16)
        if emit_kv:
            k_ref[:, sl] = (kn[:, sl] + kr).astype(BF16)


def _proj(x2d, tabs, gmix, win, gq, wuq, gkv, wuk, wuv_t, *, tm, emit_kv):
    t, d = x2d.shape
    q_lora, kv_lora = gq.shape[1], gkv.shape[1]
    ntab = tabs[0].shape[0] // tm
    tab_spec = pl.BlockSpec((tm, LANES), lambda i: (i % ntab, 0))
    row = lambda w: pl.BlockSpec((tm, w), lambda i: (i, 0))
    specs = [row(HEADS * LANES)]
    shapes = [jax.ShapeDtypeStruct((t, HEADS * LANES), BF16)]
    if emit_kv:
        specs += [row(HEADS * LANES), pl.BlockSpec((HEADS * V_HEAD, tm), lambda i: (0, i))]
        shapes += [jax.ShapeDtypeStruct((t, HEADS * LANES), BF16),
                   jax.ShapeDtypeStruct((HEADS * V_HEAD, t), BF16)]
    specs += [row(kv_lora), row(QK_ROPE)]
    shapes += [jax.ShapeDtypeStruct((t, kv_lora), F32), jax.ShapeDtypeStruct((t, QK_ROPE), F32)]
    weights = (gmix, win, gq, wuq, gkv) + ((wuk, wuv_t) if emit_kv else ())
    return pl.pallas_call(
        functools.partial(_proj_kernel, q_lora=q_lora, kv_lora=kv_lora, emit_kv=emit_kv),
        grid=(t // tm,),
        in_specs=[row(d), tab_spec, tab_spec, tab_spec] + [_const_spec(w.shape) for w in weights],
        out_specs=specs,
        out_shape=shapes,
        compiler_params=_params("arbitrary"),
        name="proj",
    )(x2d, *tabs, *weights)


def _attn_p_kernel(q_ref, k_ref, vt_ref, o_ref, *, tq):
    qi = pl.program_id(2)

    def tile(kt, carry, vis):
        ks = pl.multiple_of(kt * tq, tq)
        out = []
        for hh in range(ATTN_HB):
            m, a = carry[hh]
            sl = slice(hh * LANES, (hh + 1) * LANES)
            st = _dot_t(k_ref[pl.ds(ks, tq), sl], q_ref[:, sl])
            if vis is not None:
                st = jnp.where(vis, st, NEG_INF)
            mn = jnp.maximum(m, jnp.max(st, axis=0, keepdims=True))
            p = jnp.exp2(st - mn).astype(BF16)
            alpha = jnp.exp2(m - mn)
            vt = jnp.concatenate([vt_ref[hh * V_HEAD:(hh + 1) * V_HEAD, pl.ds(ks, tq)], ones], axis=0)
            out.append((mn, alpha * a + _dot(vt, p)))
        return tuple(out)

    ones = jnp.ones((16, tq), BF16)
    init = (jnp.full((1, tq), NEG_INF, F32), jnp.zeros((V_HEAD + 16, tq), F32))
    carry = lax.fori_loop(0, qi, lambda kt, c: tile(kt, c, None), (init,) * ATTN_HB)
    kchunk = lax.broadcasted_iota(jnp.int32, (tq, tq), 0) >> CHUNK_SHIFT
    qchunk = lax.broadcasted_iota(jnp.int32, (tq, tq), 1) >> CHUNK_SHIFT
    carry = tile(qi, carry, kchunk <= qchunk)
    ot = jnp.concatenate([a[:V_HEAD] / a[V_HEAD:V_HEAD + 1] for _, a in carry], axis=0)
    o_ref[...] = ot.T.astype(BF16)


def _attn_prompt(q, k, vt, *, batch, seq, tq):
    assert seq % tq == 0 and tq % (1 << CHUNK_SHIFT) == 0 and HEADS % ATTN_HB == 0
    nq = seq // tq
    return pl.pallas_call(
        functools.partial(_attn_p_kernel, tq=tq),
        grid=(batch, HEADS // ATTN_HB, nq),
        in_specs=[pl.BlockSpec((tq, ATTN_HB * LANES), lambda b, h, i: (b * nq + i, h)),
                  pl.BlockSpec((seq, ATTN_HB * LANES), lambda b, h, i: (b, h)),
                  pl.BlockSpec((ATTN_HB * V_HEAD, seq), lambda b, h, i: (h, b))],
        out_specs=pl.BlockSpec((tq, ATTN_HB * V_HEAD), lambda b, h, i: (b * nq + i, h)),
        out_shape=jax.ShapeDtypeStruct((batch * seq, HEADS * V_HEAD), BF16),
        compiler_params=_params("arbitrary", "arbitrary", "arbitrary"),
        name="attn_p",
    )(q, k, vt)


def _attn_s_kernel(q_ref, pc_ref, pk_ref, nc_ref, nk_ref, wabs_ref, wuv_ref, o_ref, *,
                   dec, past, kv_lora):
    q = q_ref[...]
    qf = jnp.concatenate(
        [_dot(q[:, hh * LANES:(hh + 1) * LANES], wabs_ref[hh]) for hh in range(HEADS)],
        axis=0).astype(BF16)
    qa, qr = qf[:, :kv_lora], qf[:, kv_lora:kv_lora + QK_ROPE]
    pc = pc_ref[...].astype(BF16)
    nc = nc_ref[...].astype(BF16)
    s_p = _dot_t(qa, pc) + _dot_t(qr, pk_ref[...].astype(BF16))
    s_n = _dot_t(qa, nc) + _dot_t(qr, nk_ref[...].astype(BF16))
    rows = HEADS * dec
    qpos = past + jnp.concatenate([lax.broadcasted_iota(jnp.int32, (dec, 1), 0)] * HEADS, axis=0)
    qchunk = qpos >> CHUNK_SHIFT
    kchunk_p = lax.broadcasted_iota(jnp.int32, (rows, past), 1) >> CHUNK_SHIFT
    kchunk_n = (past + lax.broadcasted_iota(jnp.int32, (rows, dec), 1)) >> CHUNK_SHIFT
    s_p = jnp.where(kchunk_p <= qchunk, s_p, NEG_INF)
    s_n = jnp.where(kchunk_n <= qchunk, s_n, NEG_INF)
    m = jnp.maximum(jnp.max(s_p, axis=-1, keepdims=True), jnp.max(s_n, axis=-1, keepdims=True))
    p_p = jnp.exp2(s_p - m)
    p_n = jnp.exp2(s_n - m)
    l = jnp.sum(p_p, axis=-1, keepdims=True) + jnp.sum(p_n, axis=-1, keepdims=True)
    olat = ((_dot(p_p.astype(BF16), pc) + _dot(p_n.astype(BF16), nc)) / l).astype(BF16)
    acc = jnp.zeros((dec, HEADS * V_HEAD), F32)
    for hh in range(HEADS):
        acc = acc + _dot(olat[hh * dec:(hh + 1) * dec], wuv_ref[hh])
    o_ref[...] = acc.astype(BF16)


def _attn_sample(q, past_ckv, past_kr, new_ckv, new_kr, wabs, wuv, *, batch, dec, past):
    kv_lora = past_ckv.shape[-1]
    return pl.pallas_call(
        functools.partial(_attn_s_kernel, dec=dec, past=past, kv_lora=kv_lora),
        grid=(batch,),
        in_specs=[pl.BlockSpec((dec, HEADS * LANES), lambda b: (b, 0)),
                  pl.BlockSpec((past, kv_lora), lambda b: (b, 0)),
                  pl.BlockSpec((past, QK_ROPE), lambda b: (b, 0)),
                  pl.BlockSpec((dec, kv_lora), lambda b: (b, 0)),
                  pl.BlockSpec((dec, QK_ROPE), lambda b: (b, 0)),
                  _const_spec(wabs.shape), _const_spec(wuv.shape)],
        out_specs=pl.BlockSpec((dec, HEADS * V_HEAD), lambda b: (b, 0)),
        out_shape=jax.ShapeDtypeStruct((batch * dec, HEADS * V_HEAD), BF16),
        compiler_params=_params("arbitrary"),
        name="attn_s",
    )(q, past_ckv, past_kr, new_ckv, new_kr, wabs, wuv)


def _s5prep_kernel(lr_ref, li_ref, ldt_ref, brt_ref, bit_ref, crt_ref, cit_ref, cr_ref, ci_ref,
                   arow_ref, airow_ref, ldtrow_ref, mt_ref, pp_ref, qt_ref, am_ref):
    hi = lax.Precision.HIGHEST
    lane = lax.broadcasted_iota(jnp.int32, (1, S5_KW), 1)
    pos = (lane >> 4).astype(F32)
    zeros = jnp.zeros((S5_STATE, S5_KW), F32)

    def powers(lrdt, lidt, expo):
        mag, ang = jnp.exp(expo * lrdt), expo * lidt
        return mag * jnp.cos(ang), mag * jnp.sin(ang)

    for gl in range(2):
        lr, li = lr_ref[gl], li_ref[gl]
        dt = jnp.exp(ldt_ref[gl])
        lrdt, lidt = lr * dt, li * dt
        abr, abi = powers(lrdt, lidt, 1.0)
        den = lr * lr + li * li
        ir, ii = lr / den, -li / den
        zr, zi = (abr - 1.0) * ir - abi * ii, (abr - 1.0) * ii + abi * ir
        bbr = zr * brt_ref[gl] - zi * bit_ref[gl]
        bbi = zr * bit_ref[gl] + zi * brt_ref[gl]
        pr, pi = powers(lrdt, lidt, (S5_TC - 1) - pos)
        pt_re, pt_im = pr * bbr - pi * bbi, pr * bbi + pi * bbr
        krev = _dot(cr_ref[gl], pt_re, hi) - _dot(ci_ref[gl], pt_im, hi)
        for t in range(S5_TC):
            shift = (S5_TC - 1 - t) * S5_CH
            rolled = krev if shift == 0 else pltpu.roll(krev, S5_KW - shift, 1)
            mt_ref[gl, t * S5_CH:(t + 1) * S5_CH, :] = jnp.where(
                lane < S5_KW - shift, rolled, 0.0).astype(BF16)
        qr, qi = powers(lrdt, lidt, pos + 1.0)
        q_re = crt_ref[gl] * qr - cit_ref[gl] * qi
        q_im = -(crt_ref[gl] * qi + cit_ref[gl] * qr)
        own = lambda a, b: [a, zeros, b, zeros] if gl == 0 else [zeros, a, zeros, b]
        pp_ref[gl] = jnp.concatenate(own(pt_re, pt_im), axis=0).T.astype(BF16)
        qt_ref[gl] = jnp.concatenate(own(q_re, q_im), axis=0).T.astype(BF16)
    dtrow = jnp.exp(ldtrow_ref[...])
    ar, ai = powers(arow_ref[...] * dtrow, airow_ref[...] * dtrow, float(S5_TC))
    am_ref[...] = jnp.concatenate([ar, ai, jnp.zeros((6, 2 * S5_STATE), F32)], axis=0)


def _s5_operators(a_re, a_im, log_dt, b_re, b_im, c_re, c_im, d_skip):
    g, p = a_re.shape
    npair = g // 2
    col = lambda a: a.reshape(g, p, 1)
    tile_s = lambda a: jnp.tile(a, (1, 1, S5_TC))
    pair = lambda a: a.reshape(npair, 1, 2 * p)
    ldt_col = log_dt.reshape(g, 1, 1)
    ldt_row = jnp.repeat(log_dt, p).reshape(npair, 1, 2 * p)
    args = (col(a_re), col(a_im), ldt_col, tile_s(b_re), tile_s(b_im),
            tile_s(c_re.transpose(0, 2, 1)), tile_s(c_im.transpose(0, 2, 1)), c_re, c_im,
            pair(a_re), pair(a_im), ldt_row)
    spec = lambda a: pl.BlockSpec((2,) + a.shape[1:], lambda i: (i,) + (0,) * (a.ndim - 1))
    pspec = lambda a: pl.BlockSpec((None,) + a.shape[1:], lambda i: (i,) + (0,) * (a.ndim - 1))
    opspec = pl.BlockSpec((2, S5_KW, S5_KW), lambda i: (i, 0, 0))
    mt, pp, qt, am = pl.pallas_call(
        _s5prep_kernel,
        grid=(npair,),
        in_specs=[spec(a) for a in args[:9]] + [pspec(a) for a in args[9:]],
        out_specs=[opspec, opspec, opspec, pl.BlockSpec((None, 8, 2 * p), lambda i: (i, 0, 0))],
        out_shape=[jax.ShapeDtypeStruct((g, S5_KW, S5_KW), BF16)] * 3
        + [jax.ShapeDtypeStruct((npair, 8, 2 * p), F32)],
        compiler_params=_params("arbitrary"),
        name="s5prep",
    )(*args)
    dcol = jnp.tile(d_skip.reshape(g, 1, S5_CH), (1, S5_TC, 1)).reshape(g, S5_KW, 1)
    return mt, pp, qt, am, dcol


def _s5in_kernel(x_ref, gmix_ref, perm_ref, wu_ref, o_ref, hp_scr, *, nchunk, groups):
    blk = S5_TC * S5_TC
    for jb in range(nchunk // S5_TC):
        h = _rms(x_ref[jb * blk:(jb + 1) * blk, :], gmix_ref[...]).astype(BF16)
        hb = _dot(perm_ref[...], h).astype(BF16)
        for s in range(S5_TC):
            hp_scr[s * nchunk + jb * S5_TC:s * nchunk + (jb + 1) * S5_TC, :] = hb[s * S5_TC:(s + 1) * S5_TC, :]
    for sb in range(S5_TC // S5_SB):
        rows = slice(sb * S5_SB * nchunk, (sb + 1) * S5_SB * nchunk)
        ut = _dot_t(wu_ref[...], hp_scr[rows, :])
        for k in range(S5_SB):
            for g in range(groups):
                o_ref[g, sb * S5_SB + k] = (
                    ut[g * S5_CH:(g + 1) * S5_CH, k * nchunk:(k + 1) * nchunk].astype(BF16))


def _s5in(x2d, gmix, wu_t, *, nseq, seq):
    d = x2d.shape[1]
    groups = wu_t.shape[0] // S5_CH
    nchunk = seq // S5_TC
    assert nchunk % S5_TC == 0
    idx = jnp.arange(S5_TC * S5_TC)
    perm = (idx[None, :] == (idx[:, None] % S5_TC) * S5_TC + idx[:, None] // S5_TC).astype(BF16)
    return pl.pallas_call(
        functools.partial(_s5in_kernel, nchunk=nchunk, groups=groups),
        grid=(nseq,),
        in_specs=[pl.BlockSpec((seq, d), lambda b: (b, 0)),
                  _const_spec(gmix.shape), _const_spec(perm.shape), _const_spec(wu_t.shape)],
        out_specs=pl.BlockSpec((groups, S5_TC, S5_CH, nchunk), lambda b: (0, 0, 0, b)),
        out_shape=jax.ShapeDtypeStruct((groups, S5_TC, S5_CH, nseq * nchunk), BF16),
        scratch_shapes=[pltpu.VMEM((seq, d), BF16)],
        compiler_params=_params("arbitrary"),
        name="s5in",
    )(x2d, gmix, perm, wu_t)


def _s5t_kernel(x_ref, mt_ref, p_ref, qt_ref, a_ref, d_ref, x0re_ref, x0im_ref,
                y_ref, sre_ref, sim_ref, ere_scr, eim_scr, xre_scr, xim_scr, *, nb, nchunk):
    tdims = (((0,), (0,)), ((), ()))
    e = (lax.dot_general(x_ref[0], p_ref[0], tdims, preferred_element_type=F32)
         + lax.dot_general(x_ref[1], p_ref[1], tdims, preferred_element_type=F32))
    ere_scr[...] = e[:, :LANES]
    eim_scr[...] = e[:, LANES:]
    ar, ai = a_ref[0:1, :], a_ref[1:2, :]

    def step(j, carry):
        re, im = carry
        rows = pl.ds(j, nb, stride=nchunk)
        xre_scr[rows, :] = re
        xim_scr[rows, :] = im
        return ar * re - ai * im + ere_scr[rows, :], ar * im + ai * re + eim_scr[rows, :]

    re, im = lax.fori_loop(0, nchunk, step, (x0re_ref[...], x0im_ref[...]),
                           unroll=8 if nchunk % 8 == 0 else 1)
    sre_ref[...] = re
    sim_ref[...] = im
    xs = jnp.concatenate([xre_scr[...], xim_scr[...]], axis=1).astype(BF16)
    for g in range(2):
        y_ref[g] = (_dot(mt_ref[g], x_ref[g]) + _dot_t(qt_ref[g], xs)
                    + x_ref[g].astype(F32) * d_ref[g]).astype(BF16)


def _s5t(u2t, mt, pp, qt, am, dcol, x0re, x0im, *, nb, nchunk):
    g, kw, total = u2t.shape
    cols = nb * nchunk
    opspec = pl.BlockSpec((2, kw, kw), lambda p, r: (p, 0, 0))
    stspec = pl.BlockSpec((nb, LANES), lambda p, r: (r, p))
    return pl.pallas_call(
        functools.partial(_s5t_kernel, nb=nb, nchunk=nchunk),
        grid=(g // 2, total // cols),
        in_specs=[pl.BlockSpec((2, kw, cols), lambda p, r: (p, 0, r)),
                  opspec, opspec, opspec,
                  pl.BlockSpec((None, 8, LANES), lambda p, r: (p, 0, 0)),
                  pl.BlockSpec((2, kw, 1), lambda p, r: (p, 0, 0)),
                  stspec, stspec],
        out_specs=[pl.BlockSpec((2, kw, cols), lambda p, r: (p, 0, r)), stspec, stspec],
        out_shape=[jax.ShapeDtypeStruct(u2t.shape, BF16),
                   jax.ShapeDtypeStruct(x0re.shape, F32),
                   jax.ShapeDtypeStruct(x0im.shape, F32)],
        scratch_shapes=[pltpu.VMEM((cols, LANES), F32)] * 4,
        compiler_params=_params("arbitrary", "arbitrary"),
        name="s5t",
    )(u2t, mt, pp, qt, am, dcol, x0re, x0im)


def _gelu(x):
    return 0.5 * x * (1.0 + jnp.tanh(math.sqrt(2.0 / math.pi) * (x + 0.044715 * (x * x * x))))


def _sigmoid(x):
    return 1.0 / (1.0 + jnp.exp(-x))


def _s5out_kernel(y_ref, wglut_ref, b_ref, g_ref, o_ref, z_scr, t_scr, *, nchunk, groups):
    width = groups * S5_CH
    for tb in range(S5_TC // S5_SB):
        for k in range(S5_SB):
            for g in range(groups):
                z_scr[g * S5_CH:(g + 1) * S5_CH, k * nchunk:(k + 1) * nchunk] = (
                    y_ref[g, tb * S5_SB + k].astype(F32))
        z = _gelu(z_scr[...])
        ssm = z * _sigmoid(_dot(wglut_ref[...], z.astype(BF16)) + b_ref[...])
        sn = ssm * lax.rsqrt(jnp.mean(ssm * ssm, axis=0, keepdims=True) + EPS) * g_ref[...]
        for k in range(S5_SB):
            snt = sn[:, k * nchunk:(k + 1) * nchunk].T
            for c in range(width // LANES):
                t_scr[c, pl.ds(tb * S5_SB + k, nchunk, stride=S5_TC), :] = snt[:, c * LANES:(c + 1) * LANES]
    for c in range(width // LANES):
        o_ref[:, c * LANES:(c + 1) * LANES] = t_scr[c].astype(BF16)


def _s5out(y2t, wglu_t, bcol, gcol, *, nseq, seq):
    groups = y2t.shape[0]
    width = groups * S5_CH
    nchunk = seq // S5_TC
    return pl.pallas_call(
        functools.partial(_s5out_kernel, nchunk=nchunk, groups=groups),
        grid=(nseq,),
        in_specs=[pl.BlockSpec((groups, S5_TC, S5_CH, nchunk), lambda b: (0, 0, 0, b)),
                  _const_spec(wglu_t.shape), _const_spec(bcol.shape), _const_spec(gcol.shape)],
        out_specs=pl.BlockSpec((seq, width), lambda b: (b, 0)),
        out_shape=jax.ShapeDtypeStruct((nseq * seq, width), BF16),
        scratch_shapes=[pltpu.VMEM((width, S5_SB * nchunk), F32),
                        pltpu.VMEM((width // LANES, seq, LANES), F32)],
        compiler_params=_params("arbitrary"),
        name="s5out",
    )(y2t, wglu_t, bcol, gcol)


def _s5_branch(x2d, gmix, wu_t, ops, wglu_t, bcol, gcol, x0_re, x0_im, *, batch, seq):
    groups = wu_t.shape[0] // S5_CH
    nchunk = seq // S5_TC
    nseq, run = (batch, seq) if nchunk % S5_TC == 0 else (1, batch * seq)
    u2t = _s5in(x2d, gmix, wu_t, nseq=nseq, seq=run)
    nb = 16 if (batch % 16 == 0 and nchunk % LANES == 0) else batch
    y2t, s_re, s_im = _s5t(u2t.reshape(groups, S5_KW, -1), *ops, x0_re, x0_im, nb=nb, nchunk=nchunk)
    s5n = _s5out(y2t.reshape(u2t.shape), wglu_t, bcol, gcol, nseq=nseq, seq=run)
    return s5n, s_re, s_im


def _ffn_kernel(x_ref, a_ref, s5_ref, gmla_ref, wout_ref, gffn_ref, wgate_ref, wup_ref, wdown_ref,
                gfin_ref, o_ref):
    mixed = jnp.concatenate([_rms(a_ref[...].astype(F32), gmla_ref[...]).astype(BF16), s5_ref[...]],
                            axis=-1)
    x1 = x_ref[...] + _dot(mixed, wout_ref[...])
    h2 = _rms(x1, gffn_ref[...]).astype(BF16)
    gate = _dot(h2, wgate_ref[...])
    act = (gate * _sigmoid(gate) * _dot(h2, wup_ref[...])).astype(BF16)
    x2 = x1 + _dot(act, wdown_ref[...])
    o_ref[...] = _rms(x2, gfin_ref[...])


def _ffn(x2d, attn, s5n, gmla, wout, gffn, wgate, wup, wdown, gfin, *, tm):
    t, d = x2d.shape
    row = lambda w: pl.BlockSpec((tm, w), lambda i: (i, 0))
    resident = lambda a: pl.BlockSpec(a.shape, lambda i: (0,) * a.ndim, pipeline_mode=pl.Buffered(1))
    consts = (gmla, wout, gffn, wgate, wup, wdown, gfin)
    return pl.pallas_call(
        _ffn_kernel,
        grid=(t // tm,),
        in_specs=[row(d), row(attn.shape[1]), row(s5n.shape[1])] + [resident(a) for a in consts],
        out_specs=row(d),
        out_shape=jax.ShapeDtypeStruct((t, d), F32),
        compiler_params=_params("arbitrary"),
        name="ffn",
    )(x2d, attn, s5n, *consts)


def _rope_tabs(pos, tm):
    inv = 1.0 / (ROPE_BASE ** (jnp.arange(0, QK_ROPE, 2, dtype=F32) / QK_ROPE))
    ang = pos.astype(F32)[:, None] * inv[None, :]
    cos, sin = jnp.cos(ang), jnp.sin(ang)
    n = pos.shape[0]
    half = QK_ROPE // 2
    z = lambda w: jnp.zeros((n, w), F32)
    c = jnp.concatenate([jnp.ones((n, QK_NOPE), F32), cos, cos, z(LANES - QK_NOPE - QK_ROPE)], axis=1)
    s1 = jnp.concatenate([z(QK_NOPE + half), sin, z(LANES - QK_NOPE - QK_ROPE)], axis=1)
    s2 = jnp.concatenate([z(QK_NOPE), -sin, z(LANES - QK_NOPE - half)], axis=1)
    reps = max(1, tm // n)
    return tuple(jnp.tile(t, (reps, 1)) for t in (c, s1, s2))


def _pad_heads(w, width):
    r = w.shape[0]
    return jnp.pad(w.reshape(r, HEADS, width), ((0, 0), (0, 0), (0, LANES - width))).reshape(r, HEADS * LANES)


def _layer_weights(w_in, w_uq, w_ukv, q_lora, kv_lora):
    o2, o3 = q_lora + kv_lora, q_lora + kv_lora + QK_ROPE
    d = w_in.shape[0]
    kr_cols = jnp.concatenate([jnp.zeros((d, QK_NOPE), F32), w_in[:, o2:o3],
                               jnp.zeros((d, LANES - QK_NOPE - QK_ROPE), F32)], axis=1)
    win = jnp.concatenate([w_in[:, :o2], kr_cols], axis=1).astype(BF16)
    wu_t = w_in[:, o3:].T.astype(BF16)
    wuq = _pad_heads(w_uq, QK_NOPE + QK_ROPE).astype(BF16)
    ukv = w_ukv.reshape(kv_lora, HEADS, QK_NOPE + V_HEAD)
    w_uk, w_uv = ukv[..., :QK_NOPE], ukv[..., QK_NOPE:]
    wuk = _pad_heads(w_uk.reshape(kv_lora, HEADS * QK_NOPE), QK_NOPE).astype(BF16)
    wuv_t = w_uv.reshape(kv_lora, HEADS * V_HEAD).T.astype(BF16)
    wabs = jnp.zeros((HEADS, LANES, kv_lora + LANES), F32)
    wabs = wabs.at[:, :QK_NOPE, :kv_lora].set(w_uk.transpose(1, 2, 0))
    wabs = wabs.at[:, QK_NOPE:QK_NOPE + QK_ROPE, kv_lora:kv_lora + QK_ROPE].set(
        jnp.broadcast_to(jnp.eye(QK_ROPE, dtype=F32), (HEADS, QK_ROPE, QK_ROPE)))
    wuv = jnp.zeros((HEADS, kv_lora, HEADS, V_HEAD), F32)
    wuv = wuv.at[jnp.arange(HEADS), :, jnp.arange(HEADS), :].set(w_uv.transpose(1, 0, 2))
    wuv = wuv.reshape(HEADS, kv_lora, HEADS * V_HEAD)
    return win, wu_t, wuq, wuk, wuv_t, wabs.astype(BF16), wuv.astype(BF16)


def _token_tile(batch, seq):
    for tm in (512, 256, 128, 64, 32, 16):
        if (batch * seq) % tm == 0 and (seq % tm == 0 or tm % seq == 0):
            return tm
    raise ValueError(f"no token tile for batch={batch} seq={seq}")


def kernel(x_prompt, x_sample, cache_mla_ckv, cache_mla_krope, state_s5_re, state_s5_im,
           g_mix, w_in, g_q, w_uq, g_kv, w_ukv,
           s5_a_re, s5_a_im, s5_log_dt, s5_b_re, s5_b_im, s5_c_re, s5_c_im, s5_d, w_glu, b_glu,
           g_out_mla, g_out_s5, w_out, g_ffn, w_gate, w_up, w_down, g_final):
    bp, lp, d = x_prompt.shape
    bs, ls, _ = x_sample.shape
    past = cache_mla_ckv.shape[2]
    q_lora, kv_lora = g_q.shape[1], g_kv.shape[1]
    groups, state = s5_a_re.shape[1], s5_a_re.shape[2]
    assert g_mix.shape[0] == 1, "single-layer model"
    assert state == S5_STATE and s5_b_re.shape[-1] == S5_CH and groups % 2 == 0
    assert lp % S5_TC == 0 and ls % S5_TC == 0 and bp % 8 == 0 and bs % 8 == 0

    xp = x_prompt.reshape(bp * lp, d)
    xs = x_sample.reshape(bs * ls, d)
    tmp, tms = _token_tile(bp, lp), _token_tile(bs, ls)
    tabs_p = _rope_tabs(jnp.arange(lp), tmp)
    tabs_s = _rope_tabs(past + jnp.arange(ls), tms)
    row2 = lambda a: a.reshape(1, -1)

    win, wu_t, wuq, wuk, wuv_t, wabs, wuv = _layer_weights(w_in[0], w_uq[0], w_ukv[0], q_lora, kv_lora)
    ops = _s5_operators(s5_a_re[0], s5_a_im[0], s5_log_dt[0], s5_b_re[0], s5_b_im[0],
                        s5_c_re[0], s5_c_im[0], s5_d[0])
    s5_w = (row2(g_mix[0]), wu_t, ops, w_glu[0].T.astype(BF16), b_glu[0].reshape(-1, 1),
            g_out_s5[0].reshape(-1, 1))
    ffn_w = (row2(g_out_mla[0]), w_out[0].astype(BF16), row2(g_ffn[0]), w_gate[0].astype(BF16),
             w_up[0].astype(BF16), w_down[0].astype(BF16), row2(g_final))
    proj_w = (row2(g_mix[0]), win, row2(g_q[0]), wuq, row2(g_kv[0]), wuk, wuv_t)

    q, k, vt, p_ckv, p_kr = _proj(xp, tabs_p, *proj_w, tm=tmp, emit_kv=True)
    attn = _attn_prompt(q, k, vt, batch=bp, seq=lp, tq=min(lp, ATTN_TQ))
    zeros = jnp.zeros((bp, groups * state), F32)
    s5n, p_re, p_im = _s5_branch(xp, *s5_w, zeros, zeros, batch=bp, seq=lp)
    yp = _ffn(xp, attn, s5n, *ffn_w, tm=tmp)

    q, s_ckv, s_kr = _proj(xs, tabs_s, *proj_w, tm=tms, emit_kv=False)
    attn = _attn_sample(q, cache_mla_ckv[0].reshape(bs * past, kv_lora),
                        cache_mla_krope[0].reshape(bs * past, QK_ROPE), s_ckv, s_kr, wabs, wuv,
                        batch=bs, dec=ls, past=past)
    s5n, s_re, s_im = _s5_branch(xs, *s5_w, state_s5_re[0].reshape(bs, groups * state),
                                 state_s5_im[0].reshape(bs, groups * state), batch=bs, seq=ls)
    ys = _ffn(xs, attn, s5n, *ffn_w, tm=tms)

    return (yp.reshape(bp, lp, d), ys.reshape(bs, ls, d),
            p_ckv.reshape(1, bp, lp, kv_lora), p_kr.reshape(1, bp, lp, QK_ROPE),
            p_re.reshape(1, bp, groups, state), p_im.reshape(1, bp, groups, state),
            s_ckv.reshape(1, bs, ls, kv_lora), s_kr.reshape(1, bs, ls, QK_ROPE),
            s_re.reshape(1, bs, groups, state), s_im.reshape(1, bs, groups, state))
```

```python
import functools
import math

import jax
import jax.numpy as jnp
from jax import lax
from jax.experimental import pallas as pl
from jax.experimental.pallas import tpu as pltpu

F32 = jnp.float32
BF16 = jnp.bfloat16

EPS = 1e-6
NEG_INF = -1e30
CHUNK_SHIFT = 6
ATTN_TQ = 1024
ATTN_HB = 4
PROJ_TM = 1024
FFN_TM = 512
HEADS = 8
QK_NOPE = 64
QK_ROPE = 32
V_HEAD = 64
ROPE_BASE = 10000.0
Q_SCALE = (QK_NOPE + QK_ROPE) ** -0.5 * math.log2(math.e)
S5_CH = 16
S5_STATE = 64
S5_TC = 16
S5_KW = S5_TC * S5_CH
S5_SB = 4
LANES = 128
VMEM_LIMIT = 56 * 1024 * 1024


def _rms(x, g):
    return x * lax.rsqrt(jnp.mean(x * x, axis=-1, keepdims=True) + EPS) * g


def _dot(a, b, precision=None):
    return jnp.dot(a, b, preferred_element_type=F32, precision=precision)


def _dot_t(a, b):
    return lax.dot_general(a, b, (((1,), (1,)), ((), ())), preferred_element_type=F32)


def _const_spec(shape):
    return pl.BlockSpec(shape, lambda *_: (0,) * len(shape))


def _params(*semantics):
    return pltpu.CompilerParams(dimension_semantics=semantics, vmem_limit_bytes=VMEM_LIMIT)


def _rope(t, c, s):
    return t * c + pltpu.roll(t, LANES - QK_ROPE, 1) * s


def _proj_kernel(x_ref, cq_ref, sq_ref, ck_ref, sk_ref, gmix_ref, win_ref, gq_ref, wuq_ref, gkv_ref, *rest,
                 q_lora, kv_lora, emit_kv):
    rest = list(rest)
    wuk_ref, wuvt_ref = (rest.pop(0), rest.pop(0)) if emit_kv else (None, None)
    q_ref = rest.pop(0)
    k_ref, vt_ref = (rest.pop(0), rest.pop(0)) if emit_kv else (None, None)
    ckv_ref, kr_ref = rest
    o1 = q_lora
    o2 = o1 + kv_lora
    h = _rms(x_ref[...], gmix_ref[...]).astype(BF16)
    proj = _dot(h, win_ref[...])
    cq, sq = cq_ref[...], sq_ref[...]
    q = _dot(_rms(proj[:, :o1], gq_ref[...]).astype(BF16), wuq_ref[...])
    ckv = _rms(proj[:, o1:o2], gkv_ref[...])
    ckv_ref[...] = ckv
    kr = _rope(proj[:, o2:], ck_ref[...], sk_ref[...])
    kr_ref[...] = kr[:, QK_NOPE:QK_NOPE + QK_ROPE]
    if emit_kv:
        ckv_b = ckv.astype(BF16)
        kn = _dot(ckv_b, wuk_ref[...])
        vt_ref[...] = _dot_t(wuvt_ref[...], ckv_b).astype(BF16)
    for hh in range(HEADS):
        sl = slice(hh * LANES, (hh + 1) * LANES)
        q_ref[:, sl] = _rope(q[:, sl], cq, sq).astype(BF16)
        if emit_kv:
            k_ref[:, sl] = (kn[:, sl] + kr).astype(BF16)


def _proj(x2d, tabs, gmix, win, gq, wuq, gkv, wuk, wuv_t, *, tm, emit_kv):
    t, d = x2d.shape
    q_lora, kv_lora = gq.shape[1], gkv.shape[1]
    ntab = tabs[0].shape[0] // tm
    tab_spec = pl.BlockSpec((tm, LANES), lambda i: (i % ntab, 0))
    row = lambda w: pl.BlockSpec((tm, w), lambda i: (i, 0))
    specs = [row(HEADS * LANES)]
    shapes = [jax.ShapeDtypeStruct((t, HEADS * LANES), BF16)]
    if emit_kv:
        specs += [row(HEADS * LANES), pl.BlockSpec((HEADS * V_HEAD, tm), lambda i: (0, i))]
        shapes += [jax.ShapeDtypeStruct((t, HEADS * LANES), BF16),
                   jax.ShapeDtypeStruct((HEADS * V_HEAD, t), BF16)]
    specs += [row(kv_lora), row(QK_ROPE)]
    shapes += [jax.ShapeDtypeStruct((t, kv_lora), F32), jax.ShapeDtypeStruct((t, QK_ROPE), F32)]
    weights = (gmix, win, gq, wuq, gkv) + ((wuk, wuv_t) if emit_kv else ())
    return pl.pallas_call(
        functools.partial(_proj_kernel, q_lora=q_lora, kv_lora=kv_lora, emit_kv=emit_kv),
        grid=(t // tm,),
        in_specs=[row(d)] + [tab_spec] * len(tabs) + [_const_spec(w.shape) for w in weights],
        out_specs=specs,
        out_shape=shapes,
        compiler_params=_params("arbitrary"),
        name="proj",
    )(x2d, *tabs, *weights)


def _attn_p_kernel(q_ref, k_ref, vt_ref, o_ref, *, tq):
    qi = pl.program_id(2)
    half = tq // 2

    def tile(carry, ks, nk, q0, vis):
        out = []
        for hh in range(ATTN_HB):
            m, a = carry[hh]
            sl = slice(hh * LANES, (hh + 1) * LANES)
            st = _dot_t(k_ref[pl.ds(ks, nk), sl], q_ref[q0:, sl])
            if vis is not None:
                st = jnp.where(vis, st, NEG_INF)
            mn = jnp.maximum(m, jnp.max(st, axis=0, keepdims=True))
            p = jnp.exp2(st - mn).astype(BF16)
            alpha = jnp.exp2(m - mn)
            vt = jnp.concatenate([vt_ref[hh * V_HEAD:(hh + 1) * V_HEAD, pl.ds(ks, nk)], ones[:, :nk]],
                                 axis=0)
            out.append((mn, alpha * a + _dot(vt, p)))
        return tuple(out)

    def visible(nk, k0, q0):
        kchunk = (k0 + lax.broadcasted_iota(jnp.int32, (nk, tq - q0), 0)) >> CHUNK_SHIFT
        qchunk = (q0 + lax.broadcasted_iota(jnp.int32, (nk, tq - q0), 1)) >> CHUNK_SHIFT
        return kchunk <= qchunk

    ones = jnp.ones((16, tq), BF16)
    init = (jnp.full((1, tq), NEG_INF, F32), jnp.zeros((V_HEAD + 16, tq), F32))
    carry = lax.fori_loop(0, qi, lambda kt, c: tile(c, pl.multiple_of(kt * tq, tq), tq, 0, None),
                          (init,) * ATTN_HB)
    diag = pl.multiple_of(qi * tq, tq)
    carry = tile(carry, diag, half, 0, visible(half, 0, 0))
    right = tile(tuple((m[:, half:], a[:, half:]) for m, a in carry),
                 pl.multiple_of(diag + half, half), half, half, visible(half, half, half))
    carry = tuple((jnp.concatenate([m[:, :half], mr], axis=1), jnp.concatenate([a[:, :half], ar], axis=1))
                  for (m, a), (mr, ar) in zip(carry, right))
    ot = jnp.concatenate([a[:V_HEAD] / a[V_HEAD:V_HEAD + 1] for _, a in carry], axis=0)
    o_ref[...] = ot.T.astype(BF16)


def _attn_prompt(q, k, vt, *, batch, seq, tq):
    assert seq % tq == 0 and tq % (1 << CHUNK_SHIFT) == 0 and HEADS % ATTN_HB == 0
    nq = seq // tq
    return pl.pallas_call(
        functools.partial(_attn_p_kernel, tq=tq),
        grid=(batch, HEADS // ATTN_HB, nq),
        in_specs=[pl.BlockSpec((tq, ATTN_HB * LANES), lambda b, h, i: (b * nq + i, h)),
                  pl.BlockSpec((seq, ATTN_HB * LANES), lambda b, h, i: (b, h)),
                  pl.BlockSpec((ATTN_HB * V_HEAD, seq), lambda b, h, i: (h, b))],
        out_specs=pl.BlockSpec((tq, ATTN_HB * V_HEAD), lambda b, h, i: (b * nq + i, h)),
        out_shape=jax.ShapeDtypeStruct((batch * seq, HEADS * V_HEAD), BF16),
        compiler_params=_params("arbitrary", "arbitrary", "arbitrary"),
        name="attn_p",
    )(q, k, vt)


def _attn_s_kernel(q_ref, pc_ref, pk_ref, nc_ref, nk_ref, wabs_ref, wuv_ref, o_ref, *,
                   dec, past, kv_lora):
    q = q_ref[...]
    qf = jnp.concatenate(
        [_dot(q[:, hh * LANES:(hh + 1) * LANES], wabs_ref[hh]) for hh in range(HEADS)],
        axis=0).astype(BF16)
    qa, qr = qf[:, :kv_lora], qf[:, kv_lora:kv_lora + QK_ROPE]
    pc = pc_ref[...].astype(BF16)
    nc = nc_ref[...].astype(BF16)
    s_p = _dot_t(qa, pc) + _dot_t(qr, pk_ref[...].astype(BF16))
    s_n = _dot_t(qa, nc) + _dot_t(qr, nk_ref[...].astype(BF16))
    rows = HEADS * dec
    qpos = past + jnp.concatenate([lax.broadcasted_iota(jnp.int32, (dec, 1), 0)] * HEADS, axis=0)
    qchunk = qpos >> CHUNK_SHIFT
    kchunk_p = lax.broadcasted_iota(jnp.int32, (rows, past), 1) >> CHUNK_SHIFT
    kchunk_n = (past + lax.broadcasted_iota(jnp.int32, (rows, dec), 1)) >> CHUNK_SHIFT
    s_p = jnp.where(kchunk_p <= qchunk, s_p, NEG_INF)
    s_n = jnp.where(kchunk_n <= qchunk, s_n, NEG_INF)
    m = jnp.maximum(jnp.max(s_p, axis=-1, keepdims=True), jnp.max(s_n, axis=-1, keepdims=True))
    p_p = jnp.exp2(s_p - m)
    p_n = jnp.exp2(s_n - m)
    l = jnp.sum(p_p, axis=-1, keepdims=True) + jnp.sum(p_n, axis=-1, keepdims=True)
    olat = ((_dot(p_p.astype(BF16), pc) + _dot(p_n.astype(BF16), nc)) / l).astype(BF16)
    acc = jnp.zeros((dec, HEADS * V_HEAD), F32)
    for hh in range(HEADS):
        acc = acc + _dot(olat[hh * dec:(hh + 1) * dec], wuv_ref[hh])
    o_ref[...] = acc.astype(BF16)


def _attn_sample(q, past_ckv, past_kr, new_ckv, new_kr, wabs, wuv, *, batch, dec, past):
    kv_lora = past_ckv.shape[-1]
    return pl.pallas_call(
        functools.partial(_attn_s_kernel, dec=dec, past=past, kv_lora=kv_lora),
        grid=(batch,),
        in_specs=[pl.BlockSpec((dec, HEADS * LANES), lambda b: (b, 0)),
                  pl.BlockSpec((past, kv_lora), lambda b: (b, 0)),
                  pl.BlockSpec((past, QK_ROPE), lambda b: (b, 0)),
                  pl.BlockSpec((dec, kv_lora), lambda b: (b, 0)),
                  pl.BlockSpec((dec, QK_ROPE), lambda b: (b, 0)),
                  _const_spec(wabs.shape), _const_spec(wuv.shape)],
        out_specs=pl.BlockSpec((dec, HEADS * V_HEAD), lambda b: (b, 0)),
        out_shape=jax.ShapeDtypeStruct((batch * dec, HEADS * V_HEAD), BF16),
        compiler_params=_params("arbitrary"),
        name="attn_s",
    )(q, past_ckv, past_kr, new_ckv, new_kr, wabs, wuv)


def _s5prep_kernel(lr_ref, li_ref, ldt_ref, brt_ref, bit_ref, crt_ref, cit_ref, cr_ref, ci_ref,
                   arow_ref, airow_ref, ldtrow_ref, mt_ref, pp_ref, qt_ref, am_ref):
    hi = lax.Precision.HIGHEST
    lane = lax.broadcasted_iota(jnp.int32, (1, S5_KW), 1)
    pos = (lane >> 4).astype(F32)
    zeros = jnp.zeros((S5_STATE, S5_KW), F32)

    def powers(lrdt, lidt, expo):
        mag, ang = jnp.exp(expo * lrdt), expo * lidt
        return mag * jnp.cos(ang), mag * jnp.sin(ang)

    for gl in range(2):
        lr, li = lr_ref[gl], li_ref[gl]
        dt = jnp.exp(ldt_ref[gl])
        lrdt, lidt = lr * dt, li * dt
        abr, abi = powers(lrdt, lidt, 1.0)
        den = lr * lr + li * li
        ir, ii = lr / den, -li / den
        zr, zi = (abr - 1.0) * ir - abi * ii, (abr - 1.0) * ii + abi * ir
        bbr = zr * brt_ref[gl] - zi * bit_ref[gl]
        bbi = zr * bit_ref[gl] + zi * brt_ref[gl]
        pr, pi = powers(lrdt, lidt, (S5_TC - 1) - pos)
        pt_re, pt_im = pr * bbr - pi * bbi, pr * bbi + pi * bbr
        krev = _dot(cr_ref[gl], pt_re, hi) - _dot(ci_ref[gl], pt_im, hi)
        for t in range(S5_TC):
            shift = (S5_TC - 1 - t) * S5_CH
            rolled = krev if shift == 0 else pltpu.roll(krev, S5_KW - shift, 1)
            mt_ref[gl, t * S5_CH:(t + 1) * S5_CH, :] = jnp.where(
                lane < S5_KW - shift, rolled, 0.0).astype(BF16)
        qr, qi = powers(lrdt, lidt, pos + 1.0)
        q_re = crt_ref[gl] * qr - cit_ref[gl] * qi
        q_im = -(crt_ref[gl] * qi + cit_ref[gl] * qr)
        own = lambda a, b: [a, zeros, b, zeros] if gl == 0 else [zeros, a, zeros, b]
        pp_ref[gl] = jnp.concatenate(own(pt_re, pt_im), axis=0).T.astype(BF16)
        qt_ref[gl] = jnp.concatenate(own(q_re, q_im), axis=0).T.astype(BF16)
    dtrow = jnp.exp(ldtrow_ref[...])
    ar, ai = powers(arow_ref[...] * dtrow, airow_ref[...] * dtrow, float(S5_TC))
    am_ref[...] = jnp.concatenate([ar, ai, jnp.zeros((6, 2 * S5_STATE), F32)], axis=0)


def _s5_operators(a_re, a_im, log_dt, b_re, b_im, c_re, c_im, d_skip):
    g, p = a_re.shape
    npair = g // 2
    col = lambda a: a.reshape(g, p, 1)
    tile_s = lambda a: jnp.tile(a, (1, 1, S5_TC))
    pair = lambda a: a.reshape(npair, 1, 2 * p)
    ldt_col = log_dt.reshape(g, 1, 1)
    ldt_row = jnp.repeat(log_dt, p).reshape(npair, 1, 2 * p)
    args = (col(a_re), col(a_im), ldt_col, tile_s(b_re), tile_s(b_im),
            tile_s(c_re.transpose(0, 2, 1)), tile_s(c_im.transpose(0, 2, 1)), c_re, c_im,
            pair(a_re), pair(a_im), ldt_row)
    spec = lambda a: pl.BlockSpec((2,) + a.shape[1:], lambda i: (i,) + (0,) * (a.ndim - 1))
    pspec = lambda a: pl.BlockSpec((None,) + a.shape[1:], lambda i: (i,) + (0,) * (a.ndim - 1))
    opspec = pl.BlockSpec((2, S5_KW, S5_KW), lambda i: (i, 0, 0))
    mt, pp, qt, am = pl.pallas_call(
        _s5prep_kernel,
        grid=(npair,),
        in_specs=[spec(a) for a in args[:9]] + [pspec(a) for a in args[9:]],
        out_specs=[opspec, opspec, opspec, pl.BlockSpec((None, 8, 2 * p), lambda i: (i, 0, 0))],
        out_shape=[jax.ShapeDtypeStruct((g, S5_KW, S5_KW), BF16)] * 3
        + [jax.ShapeDtypeStruct((npair, 8, 2 * p), F32)],
        compiler_params=_params("arbitrary"),
        name="s5prep",
    )(*args)
    dcol = jnp.tile(d_skip.reshape(g, 1, S5_CH), (1, S5_TC, 1)).reshape(g, S5_KW, 1)
    return mt, pp, qt, am, dcol


def _s5in_kernel(x_ref, gmix_ref, perm_ref, wu_ref, o_ref, hp_scr, *, nchunk, groups):
    blk = S5_TC * S5_TC
    for jb in range(nchunk // S5_TC):
        h = _rms(x_ref[jb * blk:(jb + 1) * blk, :], gmix_ref[...]).astype(BF16)
        hb = _dot(perm_ref[...], h).astype(BF16)
        for s in range(S5_TC):
            hp_scr[s * nchunk + jb * S5_TC:s * nchunk + (jb + 1) * S5_TC, :] = hb[s * S5_TC:(s + 1) * S5_TC, :]
    for sb in range(S5_TC // S5_SB):
        rows = slice(sb * S5_SB * nchunk, (sb + 1) * S5_SB * nchunk)
        ut = _dot_t(wu_ref[...], hp_scr[rows, :])
        for k in range(S5_SB):
            for g in range(groups):
                o_ref[g, sb * S5_SB + k] = (
                    ut[g * S5_CH:(g + 1) * S5_CH, k * nchunk:(k + 1) * nchunk].astype(BF16))


def _s5in(x2d, gmix, wu_t, *, nseq, seq):
    d = x2d.shape[1]
    groups = wu_t.shape[0] // S5_CH
    nchunk = seq // S5_TC
    assert nchunk % S5_TC == 0
    idx = jnp.arange(S5_TC * S5_TC)
    perm = (idx[None, :] == (idx[:, None] % S5_TC) * S5_TC + idx[:, None] // S5_TC).astype(BF16)
    return pl.pallas_call(
        functools.partial(_s5in_kernel, nchunk=nchunk, groups=groups),
        grid=(nseq,),
        in_specs=[pl.BlockSpec((seq, d), lambda b: (b, 0)),
                  _const_spec(gmix.shape), _const_spec(perm.shape), _const_spec(wu_t.shape)],
        out_specs=pl.BlockSpec((groups, S5_TC, S5_CH, nchunk), lambda b: (0, 0, 0, b)),
        out_shape=jax.ShapeDtypeStruct((groups, S5_TC, S5_CH, nseq * nchunk), BF16),
        scratch_shapes=[pltpu.VMEM((seq, d), BF16)],
        compiler_params=_params("arbitrary"),
        name="s5in",
    )(x2d, gmix, perm, wu_t)


def _s5t_kernel(x_ref, mt_ref, p_ref, qt_ref, a_ref, d_ref, x0re_ref, x0im_ref,
                y_ref, sre_ref, sim_ref, ere_scr, eim_scr, xre_scr, xim_scr, *, nb, nchunk):
    tdims = (((0,), (0,)), ((), ()))
    e = (lax.dot_general(x_ref[0], p_ref[0], tdims, preferred_element_type=F32)
         + lax.dot_general(x_ref[1], p_ref[1], tdims, preferred_element_type=F32))
    ere_scr[...] = e[:, :LANES]
    eim_scr[...] = e[:, LANES:]
    ar, ai = a_ref[0:1, :], a_ref[1:2, :]

    def step(j, carry):
        re, im = carry
        rows = pl.ds(j, nb, stride=nchunk)
        xre_scr[rows, :] = re
        xim_scr[rows, :] = im
        return ar * re - ai * im + ere_scr[rows, :], ar * im + ai * re + eim_scr[rows, :]

    re, im = lax.fori_loop(0, nchunk, step, (x0re_ref[...], x0im_ref[...]),
                           unroll=8 if nchunk % 8 == 0 else 1)
    sre_ref[...] = re
    sim_ref[...] = im
    xs = jnp.concatenate([xre_scr[...], xim_scr[...]], axis=1).astype(BF16)
    for g in range(2):
        y_ref[g] = (_dot(mt_ref[g], x_ref[g]) + _dot_t(qt_ref[g], xs)
                    + x_ref[g].astype(F32) * d_ref[g]).astype(BF16)


def _s5t(u2t, mt, pp, qt, am, dcol, x0re, x0im, *, nb, nchunk):
    g, kw, total = u2t.shape
    cols = nb * nchunk
    opspec = pl.BlockSpec((2, kw, kw), lambda p, r: (p, 0, 0))
    stspec = pl.BlockSpec((nb, LANES), lambda p, r: (r, p))
    return pl.pallas_call(
        functools.partial(_s5t_kernel, nb=nb, nchunk=nchunk),
        grid=(g // 2, total // cols),
        in_specs=[pl.BlockSpec((2, kw, cols), lambda p, r: (p, 0, r)),
                  opspec, opspec, opspec,
                  pl.BlockSpec((None, 8, LANES), lambda p, r: (p, 0, 0)),
                  pl.BlockSpec((2, kw, 1), lambda p, r: (p, 0, 0)),
                  stspec, stspec],
        out_specs=[pl.BlockSpec((2, kw, cols), lambda p, r: (p, 0, r)), stspec, stspec],
        out_shape=[jax.ShapeDtypeStruct(u2t.shape, BF16),
                   jax.ShapeDtypeStruct(x0re.shape, F32),
                   jax.ShapeDtypeStruct(x0im.shape, F32)],
        scratch_shapes=[pltpu.VMEM((cols, LANES), F32)] * 4,
        compiler_params=_params("arbitrary", "arbitrary"),
        name="s5t",
    )(u2t, mt, pp, qt, am, dcol, x0re, x0im)


def _gelu(x):
    return 0.5 * x * (1.0 + jnp.tanh(math.sqrt(2.0 / math.pi) * (x + 0.044715 * (x * x * x))))


def _sigmoid(x):
    return 1.0 / (1.0 + jnp.exp(-x))


def _s5out_kernel(y_ref, wglut_ref, b_ref, g_ref, o_ref, z_scr, t_scr, *, nchunk, groups):
    width = groups * S5_CH
    for tb in range(S5_TC // S5_SB):
        for k in range(S5_SB):
            for g in range(groups):
                z_scr[g * S5_CH:(g + 1) * S5_CH, k * nchunk:(k + 1) * nchunk] = (
                    y_ref[g, tb * S5_SB + k].astype(F32))
        z = _gelu(z_scr[...])
        ssm = z * _sigmoid(_dot(wglut_ref[...], z.astype(BF16)) + b_ref[...])
        sn = ssm * lax.rsqrt(jnp.mean(ssm * ssm, axis=0, keepdims=True) + EPS) * g_ref[...]
        for k in range(S5_SB):
            snt = sn[:, k * nchunk:(k + 1) * nchunk].T
            for c in range(width // LANES):
                t_scr[c, pl.ds(tb * S5_SB + k, nchunk, stride=S5_TC), :] = snt[:, c * LANES:(c + 1) * LANES]
    for c in range(width // LANES):
        o_ref[:, c * LANES:(c + 1) * LANES] = t_scr[c].astype(BF16)


def _s5out(y2t, wglu_t, bcol, gcol, *, nseq, seq):
    groups = y2t.shape[0]
    width = groups * S5_CH
    nchunk = seq // S5_TC
    return pl.pallas_call(
        functools.partial(_s5out_kernel, nchunk=nchunk, groups=groups),
        grid=(nseq,),
        in_specs=[pl.BlockSpec((groups, S5_TC, S5_CH, nchunk), lambda b: (0, 0, 0, b)),
                  _const_spec(wglu_t.shape), _const_spec(bcol.shape), _const_spec(gcol.shape)],
        out_specs=pl.BlockSpec((seq, width), lambda b: (b, 0)),
        out_shape=jax.ShapeDtypeStruct((nseq * seq, width), BF16),
        scratch_shapes=[pltpu.VMEM((width, S5_SB * nchunk), F32),
                        pltpu.VMEM((width // LANES, seq, LANES), F32)],
        compiler_params=_params("arbitrary"),
        name="s5out",
    )(y2t, wglu_t, bcol, gcol)


def _s5_branch(x2d, gmix, wu_t, ops, wglu_t, bcol, gcol, x0_re, x0_im, *, batch, seq):
    groups = wu_t.shape[0] // S5_CH
    nchunk = seq // S5_TC
    nseq, run = (batch, seq) if nchunk % S5_TC == 0 else (1, batch * seq)
    u2t = _s5in(x2d, gmix, wu_t, nseq=nseq, seq=run)
    nb = 16 if (batch % 16 == 0 and nchunk % LANES == 0) else batch
    y2t, s_re, s_im = _s5t(u2t.reshape(groups, S5_KW, -1), *ops, x0_re, x0_im, nb=nb, nchunk=nchunk)
    s5n = _s5out(y2t.reshape(u2t.shape), wglu_t, bcol, gcol, nseq=nseq, seq=run)
    return s5n, s_re, s_im


def _ffn_kernel(x_ref, a_ref, s5_ref, gmla_ref, wout_ref, gffn_ref, wgate_ref, wup_ref, wdown_ref,
                gfin_ref, o_ref):
    mixed = jnp.concatenate([_rms(a_ref[...].astype(F32), gmla_ref[...]).astype(BF16), s5_ref[...]],
                            axis=-1)
    x1 = x_ref[...] + _dot(mixed, wout_ref[...])
    h2 = _rms(x1, gffn_ref[...]).astype(BF16)
    gate = _dot(h2, wgate_ref[...])
    act = (gate * _sigmoid(gate) * _dot(h2, wup_ref[...])).astype(BF16)
    x2 = x1 + _dot(act, wdown_ref[...])
    o_ref[...] = _rms(x2, gfin_ref[...])


def _ffn(x2d, attn, s5n, gmla, wout, gffn, wgate, wup, wdown, gfin, *, tm):
    t, d = x2d.shape
    row = lambda w: pl.BlockSpec((tm, w), lambda i: (i, 0))
    resident = lambda a: pl.BlockSpec(a.shape, lambda i: (0,) * a.ndim, pipeline_mode=pl.Buffered(1))
    consts = (gmla, wout, gffn, wgate, wup, wdown, gfin)
    return pl.pallas_call(
        _ffn_kernel,
        grid=(t // tm,),
        in_specs=[row(d), row(attn.shape[1]), row(s5n.shape[1])] + [resident(a) for a in consts],
        out_specs=row(d),
        out_shape=jax.ShapeDtypeStruct((t, d), F32),
        compiler_params=_params("arbitrary"),
        name="ffn",
    )(x2d, attn, s5n, *consts)


def _rope_tabs(pos, tm):
    inv = 1.0 / (ROPE_BASE ** (jnp.arange(0, QK_ROPE, 2, dtype=F32) / QK_ROPE))
    ang = pos.astype(F32)[:, None] * inv[None, :]
    cos, sin = jnp.cos(ang), jnp.sin(ang)
    n = pos.shape[0]
    z = lambda w: jnp.zeros((n, w), F32)
    c = jnp.concatenate([jnp.ones((n, QK_NOPE), F32), cos, cos, z(LANES - QK_NOPE - QK_ROPE)], axis=1)
    s = jnp.concatenate([z(QK_NOPE), sin, sin, z(LANES - QK_NOPE - QK_ROPE)], axis=1)
    reps = max(1, tm // n)
    return tuple(jnp.tile(t, (reps, 1)) for t in (c * Q_SCALE, s * Q_SCALE, c, s))


def _pad_heads(w, width):
    r = w.shape[0]
    return jnp.pad(w.reshape(r, HEADS, width), ((0, 0), (0, 0), (0, LANES - width))).reshape(r, HEADS * LANES)


def _layer_weights(w_in, w_uq, w_ukv, q_lora, kv_lora):
    o2, o3 = q_lora + kv_lora, q_lora + kv_lora + QK_ROPE
    d = w_in.shape[0]
    half = QK_ROPE // 2
    swapped = lambda w: jnp.concatenate([-w[..., half:], w[..., :half]], axis=-1)
    w_kr = w_in[:, o2:o3]
    win = jnp.concatenate([w_in[:, :o2], jnp.zeros((d, QK_NOPE), F32), w_kr, swapped(w_kr)],
                          axis=1).astype(BF16)
    wu_t = w_in[:, o3:].T.astype(BF16)
    uq = w_uq.reshape(q_lora, HEADS, QK_NOPE + QK_ROPE)
    wuq = jnp.concatenate([uq, swapped(uq[..., QK_NOPE:])], axis=-1).reshape(q_lora, HEADS * LANES)
    wuq = wuq.astype(BF16)
    ukv = w_ukv.reshape(kv_lora, HEADS, QK_NOPE + V_HEAD)
    w_uk, w_uv = ukv[..., :QK_NOPE], ukv[..., QK_NOPE:]
    wuk = _pad_heads(w_uk.reshape(kv_lora, HEADS * QK_NOPE), QK_NOPE).astype(BF16)
    wuv_t = w_uv.reshape(kv_lora, HEADS * V_HEAD).T.astype(BF16)
    wabs = jnp.zeros((HEADS, LANES, kv_lora + LANES), F32)
    wabs = wabs.at[:, :QK_NOPE, :kv_lora].set(w_uk.transpose(1, 2, 0))
    wabs = wabs.at[:, QK_NOPE:QK_NOPE + QK_ROPE, kv_lora:kv_lora + QK_ROPE].set(
        jnp.broadcast_to(jnp.eye(QK_ROPE, dtype=F32), (HEADS, QK_ROPE, QK_ROPE)))
    wuv = jnp.zeros((HEADS, kv_lora, HEADS, V_HEAD), F32)
    wuv = wuv.at[jnp.arange(HEADS), :, jnp.arange(HEADS), :].set(w_uv.transpose(1, 0, 2))
    wuv = wuv.reshape(HEADS, kv_lora, HEADS * V_HEAD)
    return win, wu_t, wuq, wuk, wuv_t, wabs.astype(BF16), wuv.astype(BF16)


def _token_tile(batch, seq, cap):
    for tm in (1024, 512, 256, 128, 64, 32, 16):
        if tm <= cap and (batch * seq) % tm == 0 and (seq % tm == 0 or tm % seq == 0):
            return tm
    raise ValueError(f"no token tile for batch={batch} seq={seq}")


def kernel(x_prompt, x_sample, cache_mla_ckv, cache_mla_krope, state_s5_re, state_s5_im,
           g_mix, w_in, g_q, w_uq, g_kv, w_ukv,
           s5_a_re, s5_a_im, s5_log_dt, s5_b_re, s5_b_im, s5_c_re, s5_c_im, s5_d, w_glu, b_glu,
           g_out_mla, g_out_s5, w_out, g_ffn, w_gate, w_up, w_down, g_final):
    bp, lp, d = x_prompt.shape
    bs, ls, _ = x_sample.shape
    past = cache_mla_ckv.shape[2]
    q_lora, kv_lora = g_q.shape[1], g_kv.shape[1]
    groups, state = s5_a_re.shape[1], s5_a_re.shape[2]
    assert g_mix.shape[0] == 1, "single-layer model"
    assert state == S5_STATE and s5_b_re.shape[-1] == S5_CH and groups % 2 == 0
    assert lp % S5_TC == 0 and ls % S5_TC == 0 and bp % 8 == 0 and bs % 8 == 0

    xp = x_prompt.reshape(bp * lp, d)
    xs = x_sample.reshape(bs * ls, d)
    tmp, tms = _token_tile(bp, lp, FFN_TM), _token_tile(bs, ls, FFN_TM)
    tpp, tps = _token_tile(bp, lp, PROJ_TM), _token_tile(bs, ls, PROJ_TM)
    tabs_p = _rope_tabs(jnp.arange(lp), tpp)
    tabs_s = _rope_tabs(past + jnp.arange(ls), tps)
    row2 = lambda a: a.reshape(1, -1)

    win, wu_t, wuq, wuk, wuv_t, wabs, wuv = _layer_weights(w_in[0], w_uq[0], w_ukv[0], q_lora, kv_lora)
    ops = _s5_operators(s5_a_re[0], s5_a_im[0], s5_log_dt[0], s5_b_re[0], s5_b_im[0],
                        s5_c_re[0], s5_c_im[0], s5_d[0])
    s5_w = (row2(g_mix[0]), wu_t, ops, w_glu[0].T.astype(BF16), b_glu[0].reshape(-1, 1),
            g_out_s5[0].reshape(-1, 1))
    ffn_w = (row2(g_out_mla[0]), w_out[0].astype(BF16), row2(g_ffn[0]), w_gate[0].astype(BF16),
             w_up[0].astype(BF16), w_down[0].astype(BF16), row2(g_final))
    proj_w = (row2(g_mix[0]), win, row2(g_q[0]), wuq, row2(g_kv[0]), wuk, wuv_t)

    q, k, vt, p_ckv, p_kr = _proj(xp, tabs_p, *proj_w, tm=tpp, emit_kv=True)
    attn = _attn_prompt(q, k, vt, batch=bp, seq=lp, tq=min(lp, ATTN_TQ))
    zeros = jnp.zeros((bp, groups * state), F32)
    s5n, p_re, p_im = _s5_branch(xp, *s5_w, zeros, zeros, batch=bp, seq=lp)
    yp = _ffn(xp, attn, s5n, *ffn_w, tm=tmp)

    q, s_ckv, s_kr = _proj(xs, tabs_s, *proj_w, tm=tps, emit_kv=False)
    attn = _attn_sample(q, cache_mla_ckv[0].reshape(bs * past, kv_lora),
                        cache_mla_krope[0].reshape(bs * past, QK_ROPE), s_ckv, s_kr, wabs, wuv,
                        batch=bs, dec=ls, past=past)
    s5n, s_re, s_im = _s5_branch(xs, *s5_w, state_s5_re[0].reshape(bs, groups * state),
                                 state_s5_im[0].reshape(bs, groups * state), batch=bs, seq=ls)
    ys = _ffn(xs, attn, s5n, *ffn_w, tm=tms)

    return (yp.reshape(bp, lp, d), ys.reshape(bs, ls, d),
            p_ckv.reshape(1, bp, lp, kv_lora), p_kr.reshape(1, bp, lp, QK_ROPE),
            p_re.reshape(1, bp, groups, state), p_im.reshape(1, bp, groups, state),
            s_ckv.reshape(1, bs, ls, kv_lora), s_kr.reshape(1, bs, ls, QK_ROPE),
            s_re.reshape(1, bs, groups, state), s_im.reshape(1, bs, groups, state))
```

```python
import functools
import math

import jax
import jax.numpy as jnp
from jax import lax
from jax.experimental import pallas as pl
from jax.experimental.pallas import tpu as pltpu

F32 = jnp.float32
BF16 = jnp.bfloat16

EPS = 1e-6
NEG_INF = -1e30
CHUNK_SHIFT = 6
ATTN_TQ = 1024
ATTN_HB = 4
PROJ_TM = 1024
PROJ_PARTS = 2
FFN_TM = 512
FFN_PARTS = 2
HEADS = 8
QK_NOPE = 64
QK_ROPE = 32
V_HEAD = 64
ROPE_BASE = 10000.0
Q_SCALE = (QK_NOPE + QK_ROPE) ** -0.5 * math.log2(math.e)
S5_CH = 16
S5_STATE = 64
S5_TC = 16
S5_KW = S5_TC * S5_CH
S5_SB = 4
LANES = 128
VMEM_LIMIT = 56 * 1024 * 1024


def _rms(x, g):
    return x * lax.rsqrt(jnp.mean(x * x, axis=-1, keepdims=True) + EPS) * g


def _dot(a, b, precision=None):
    return jnp.dot(a, b, preferred_element_type=F32, precision=precision)


def _dot_t(a, b):
    return lax.dot_general(a, b, (((1,), (1,)), ((), ())), preferred_element_type=F32)


def _const_spec(shape):
    return pl.BlockSpec(shape, lambda *_: (0,) * len(shape))


def _params(*semantics, **kw):
    return pltpu.CompilerParams(dimension_semantics=semantics, vmem_limit_bytes=VMEM_LIMIT, **kw)


def _rope(t, c, s):
    return t * c + pltpu.roll(t, LANES - QK_ROPE, 1) * s


def _proj_kernel(x_ref, cq_ref, sq_ref, ck_ref, sk_ref, gmix_ref, win_ref, gq_ref, wuq_ref, gkv_ref, *rest,
                 q_lora, kv_lora, emit_kv):
    rest = list(rest)
    wuk_ref, wuvt_ref = (rest.pop(0), rest.pop(0)) if emit_kv else (None, None)
    q_ref = rest.pop(0)
    k_ref, vt_ref = (rest.pop(0), rest.pop(0)) if emit_kv else (None, None)
    ckv_ref, kr_ref = rest
    o1 = q_lora
    o2 = o1 + kv_lora
    tm = x_ref.shape[0]
    parts = [slice(i * (tm // PROJ_PARTS), (i + 1) * (tm // PROJ_PARTS)) for i in range(PROJ_PARTS)]
    hs = [_rms(x_ref[r, :], gmix_ref[...]).astype(BF16) for r in parts]
    projs = [_dot(h, win_ref[...]) for h in hs]
    qs = [_dot(_rms(p[:, :o1], gq_ref[...]).astype(BF16), wuq_ref[...]) for p in projs]
    ckvs = [_rms(p[:, o1:o2], gkv_ref[...]) for p in projs]
    krs = [_rope(p[:, o2:], ck_ref[r, :], sk_ref[r, :]) for p, r in zip(projs, parts)]
    for r, ckv, kr in zip(parts, ckvs, krs):
        ckv_ref[r, :] = ckv
        kr_ref[r, :] = kr[:, QK_NOPE:QK_NOPE + QK_ROPE]
    if emit_kv:
        ckv_bs = [ckv.astype(BF16) for ckv in ckvs]
        kns = [_dot(c, wuk_ref[...]) for c in ckv_bs]
        for r, c in zip(parts, ckv_bs):
            vt_ref[:, r] = _dot_t(wuvt_ref[...], c).astype(BF16)
    for i, r in enumerate(parts):
        cq, sq = cq_ref[r, :], sq_ref[r, :]
        for hh in range(HEADS):
            sl = slice(hh * LANES, (hh + 1) * LANES)
            q_ref[r, sl] = _rope(qs[i][:, sl], cq, sq).astype(BF16)
            if emit_kv:
                k_ref[r, sl] = (kns[i][:, sl] + krs[i]).astype(BF16)


def _proj(x2d, tabs, gmix, win, gq, wuq, gkv, wuk, wuv_t, *, tm, emit_kv):
    t, d = x2d.shape
    q_lora, kv_lora = gq.shape[1], gkv.shape[1]
    ntab = tabs[0].shape[0] // tm
    tab_spec = pl.BlockSpec((tm, LANES), lambda i: (i % ntab, 0))
    row = lambda w: pl.BlockSpec((tm, w), lambda i: (i, 0))
    specs = [row(HEADS * LANES)]
    shapes = [jax.ShapeDtypeStruct((t, HEADS * LANES), BF16)]
    if emit_kv:
        specs += [row(HEADS * LANES), pl.BlockSpec((HEADS * V_HEAD, tm), lambda i: (0, i))]
        shapes += [jax.ShapeDtypeStruct((t, HEADS * LANES), BF16),
                   jax.ShapeDtypeStruct((HEADS * V_HEAD, t), BF16)]
    specs += [row(kv_lora), row(QK_ROPE)]
    shapes += [jax.ShapeDtypeStruct((t, kv_lora), F32), jax.ShapeDtypeStruct((t, QK_ROPE), F32)]
    weights = (gmix, win, gq, wuq, gkv) + ((wuk, wuv_t) if emit_kv else ())
    return pl.pallas_call(
        functools.partial(_proj_kernel, q_lora=q_lora, kv_lora=kv_lora, emit_kv=emit_kv),
        grid=(t // tm,),
        in_specs=[row(d)] + [tab_spec] * len(tabs) + [_const_spec(w.shape) for w in weights],
        out_specs=specs,
        out_shape=shapes,
        compiler_params=_params("arbitrary"),
        name="proj",
    )(x2d, *tabs, *weights)


def _attn_p_kernel(q_ref, k_ref, vt_ref, o_ref, *, tq):
    qi = pl.program_id(2)
    half = tq // 2

    def tile(carry, ks, nk, q0, vis):
        def scores(hh):
            sl = slice(hh * LANES, (hh + 1) * LANES)
            st = _dot_t(k_ref[pl.ds(ks, nk), sl], q_ref[q0:, sl])
            return st if vis is None else jnp.where(vis, st, NEG_INF)

        def softmax(hh, st):
            m, a = carry[hh]
            mn = jnp.maximum(m, jnp.max(st, axis=0, keepdims=True))
            return mn, jnp.exp2(st - mn).astype(BF16), jnp.exp2(m - mn) * a

        def values(hh, mn, p, a):
            vt = jnp.concatenate([vt_ref[hh * V_HEAD:(hh + 1) * V_HEAD, pl.ds(ks, nk)], ones[:, :nk]],
                                 axis=0)
            return mn, a + _dot(vt, p)

        sts = [scores(hh) for hh in range(ATTN_HB)]
        ps = [softmax(hh, sts[hh]) for hh in range(ATTN_HB)]
        out = [values(hh, *ps[hh]) for hh in range(ATTN_HB)]
        return tuple(out)

    def visible(nk, k0, q0):
        kchunk = (k0 + lax.broadcasted_iota(jnp.int32, (nk, tq - q0), 0)) >> CHUNK_SHIFT
        qchunk = (q0 + lax.broadcasted_iota(jnp.int32, (nk, tq - q0), 1)) >> CHUNK_SHIFT
        return kchunk <= qchunk

    ones = jnp.ones((16, tq), BF16)
    init = (jnp.full((1, tq), NEG_INF, F32), jnp.zeros((V_HEAD + 16, tq), F32))
    carry = lax.fori_loop(0, qi, lambda kt, c: tile(c, pl.multiple_of(kt * tq, tq), tq, 0, None),
                          (init,) * ATTN_HB)
    diag = pl.multiple_of(qi * tq, tq)
    carry = tile(carry, diag, half, 0, visible(half, 0, 0))
    right = tile(tuple((m[:, half:], a[:, half:]) for m, a in carry),
                 pl.multiple_of(diag + half, half), half, half, visible(half, half, half))
    carry = tuple((jnp.concatenate([m[:, :half], mr], axis=1), jnp.concatenate([a[:, :half], ar], axis=1))
                  for (m, a), (mr, ar) in zip(carry, right))
    ot = jnp.concatenate([a[:V_HEAD] / a[V_HEAD:V_HEAD + 1] for _, a in carry], axis=0)
    o_ref[...] = ot.T.astype(BF16)


def _attn_prompt(q, k, vt, *, batch, seq, tq):
    assert seq % tq == 0 and tq % (1 << CHUNK_SHIFT) == 0 and HEADS % ATTN_HB == 0
    nq = seq // tq
    return pl.pallas_call(
        functools.partial(_attn_p_kernel, tq=tq),
        grid=(batch, HEADS // ATTN_HB, nq),
        in_specs=[pl.BlockSpec((tq, ATTN_HB * LANES), lambda b, h, i: (b * nq + i, h)),
                  pl.BlockSpec((seq, ATTN_HB * LANES), lambda b, h, i: (b, h)),
                  pl.BlockSpec((ATTN_HB * V_HEAD, seq), lambda b, h, i: (h, b))],
        out_specs=pl.BlockSpec((tq, ATTN_HB * V_HEAD), lambda b, h, i: (b * nq + i, h)),
        out_shape=jax.ShapeDtypeStruct((batch * seq, HEADS * V_HEAD), BF16),
        compiler_params=_params("arbitrary", "arbitrary", "arbitrary"),
        name="attn_p",
    )(q, k, vt)


def _attn_s_kernel(q_ref, pc_ref, pk_ref, nc_ref, nk_ref, wabs_ref, wuv_ref, o_ref, *,
                   dec, past, kv_lora):
    q = q_ref[...]
    qf = jnp.concatenate(
        [_dot(q[:, hh * LANES:(hh + 1) * LANES], wabs_ref[hh]) for hh in range(HEADS)],
        axis=0).astype(BF16)
    qa, qr = qf[:, :kv_lora], qf[:, kv_lora:kv_lora + QK_ROPE]
    pc = pc_ref[...].astype(BF16)
    nc = nc_ref[...].astype(BF16)
    s_p = _dot_t(qa, pc) + _dot_t(qr, pk_ref[...].astype(BF16))
    s_n = _dot_t(qa, nc) + _dot_t(qr, nk_ref[...].astype(BF16))
    rows = HEADS * dec
    qpos = past + jnp.concatenate([lax.broadcasted_iota(jnp.int32, (dec, 1), 0)] * HEADS, axis=0)
    qchunk = qpos >> CHUNK_SHIFT
    kchunk_p = lax.broadcasted_iota(jnp.int32, (rows, past), 1) >> CHUNK_SHIFT
    kchunk_n = (past + lax.broadcasted_iota(jnp.int32, (rows, dec), 1)) >> CHUNK_SHIFT
    s_p = jnp.where(kchunk_p <= qchunk, s_p, NEG_INF)
    s_n = jnp.where(kchunk_n <= qchunk, s_n, NEG_INF)
    m = jnp.maximum(jnp.max(s_p, axis=-1, keepdims=True), jnp.max(s_n, axis=-1, keepdims=True))
    p_p = jnp.exp2(s_p - m)
    p_n = jnp.exp2(s_n - m)
    l = jnp.sum(p_p, axis=-1, keepdims=True) + jnp.sum(p_n, axis=-1, keepdims=True)
    olat = ((_dot(p_p.astype(BF16), pc) + _dot(p_n.astype(BF16), nc)) / l).astype(BF16)
    acc = jnp.zeros((dec, HEADS * V_HEAD), F32)
    for hh in range(HEADS):
        acc = acc + _dot(olat[hh * dec:(hh + 1) * dec], wuv_ref[hh])
    o_ref[...] = acc.astype(BF16)


def _attn_sample(q, past_ckv, past_kr, new_ckv, new_kr, wabs, wuv, *, batch, dec, past):
    kv_lora = past_ckv.shape[-1]
    return pl.pallas_call(
        functools.partial(_attn_s_kernel, dec=dec, past=past, kv_lora=kv_lora),
        grid=(batch,),
        in_specs=[pl.BlockSpec((dec, HEADS * LANES), lambda b: (b, 0)),
                  pl.BlockSpec((past, kv_lora), lambda b: (b, 0)),
                  pl.BlockSpec((past, QK_ROPE), lambda b: (b, 0)),
                  pl.BlockSpec((dec, kv_lora), lambda b: (b, 0)),
                  pl.BlockSpec((dec, QK_ROPE), lambda b: (b, 0)),
                  _const_spec(wabs.shape), _const_spec(wuv.shape)],
        out_specs=pl.BlockSpec((dec, HEADS * V_HEAD), lambda b: (b, 0)),
        out_shape=jax.ShapeDtypeStruct((batch * dec, HEADS * V_HEAD), BF16),
        compiler_params=_params("arbitrary"),
        name="attn_s",
    )(q, past_ckv, past_kr, new_ckv, new_kr, wabs, wuv)


def _s5prep_kernel(lr_ref, li_ref, ldt_ref, brt_ref, bit_ref, crt_ref, cit_ref, cr_ref, ci_ref,
                   arow_ref, airow_ref, ldtrow_ref, mt_ref, pp_ref, qt_ref, am_ref):
    hi = lax.Precision.HIGHEST
    lane = lax.broadcasted_iota(jnp.int32, (1, S5_KW), 1)
    pos = (lane >> 4).astype(F32)
    zeros = jnp.zeros((S5_STATE, S5_KW), F32)

    def powers(lrdt, lidt, expo):
        mag, ang = jnp.exp(expo * lrdt), expo * lidt
        return mag * jnp.cos(ang), mag * jnp.sin(ang)

    for gl in range(2):
        lr, li = lr_ref[gl], li_ref[gl]
        dt = jnp.exp(ldt_ref[gl])
        lrdt, lidt = lr * dt, li * dt
        abr, abi = powers(lrdt, lidt, 1.0)
        den = lr * lr + li * li
        ir, ii = lr / den, -li / den
        zr, zi = (abr - 1.0) * ir - abi * ii, (abr - 1.0) * ii + abi * ir
        bbr = zr * brt_ref[gl] - zi * bit_ref[gl]
        bbi = zr * bit_ref[gl] + zi * brt_ref[gl]
        pr, pi = powers(lrdt, lidt, (S5_TC - 1) - pos)
        pt_re, pt_im = pr * bbr - pi * bbi, pr * bbi + pi * bbr
        krev = _dot(cr_ref[gl], pt_re, hi) - _dot(ci_ref[gl], pt_im, hi)
        for t in range(S5_TC):
            shift = (S5_TC - 1 - t) * S5_CH
            rolled = krev if shift == 0 else pltpu.roll(krev, S5_KW - shift, 1)
            mt_ref[gl, t * S5_CH:(t + 1) * S5_CH, :] = jnp.where(
                lane < S5_KW - shift, rolled, 0.0).astype(BF16)
        qr, qi = powers(lrdt, lidt, pos + 1.0)
        q_re = crt_ref[gl] * qr - cit_ref[gl] * qi
        q_im = -(crt_ref[gl] * qi + cit_ref[gl] * qr)
        own = lambda a, b: [a, zeros, b, zeros] if gl == 0 else [zeros, a, zeros, b]
        pp_ref[gl] = jnp.concatenate(own(pt_re, pt_im), axis=0).T.astype(BF16)
        qt_ref[gl] = jnp.concatenate(own(q_re, q_im), axis=0).T.astype(BF16)
    dtrow = jnp.exp(ldtrow_ref[...])
    ar, ai = powers(arow_ref[...] * dtrow, airow_ref[...] * dtrow, float(S5_TC))
    am_ref[...] = jnp.concatenate([ar, ai, jnp.zeros((6, 2 * S5_STATE), F32)], axis=0)


def _s5_operators(a_re, a_im, log_dt, b_re, b_im, c_re, c_im, d_skip):
    g, p = a_re.shape
    npair = g // 2
    col = lambda a: a.reshape(g, p, 1)
    tile_s = lambda a: jnp.tile(a, (1, 1, S5_TC))
    pair = lambda a: a.reshape(npair, 1, 2 * p)
    ldt_col = log_dt.reshape(g, 1, 1)
    ldt_row = jnp.repeat(log_dt, p).reshape(npair, 1, 2 * p)
    args = (col(a_re), col(a_im), ldt_col, tile_s(b_re), tile_s(b_im),
            tile_s(c_re.transpose(0, 2, 1)), tile_s(c_im.transpose(0, 2, 1)), c_re, c_im,
            pair(a_re), pair(a_im), ldt_row)
    spec = lambda a: pl.BlockSpec((2,) + a.shape[1:], lambda i: (i,) + (0,) * (a.ndim - 1))
    pspec = lambda a: pl.BlockSpec((None,) + a.shape[1:], lambda i: (i,) + (0,) * (a.ndim - 1))
    opspec = pl.BlockSpec((2, S5_KW, S5_KW), lambda i: (i, 0, 0))
    mt, pp, qt, am = pl.pallas_call(
        _s5prep_kernel,
        grid=(npair,),
        in_specs=[spec(a) for a in args[:9]] + [pspec(a) for a in args[9:]],
        out_specs=[opspec, opspec, opspec, pl.BlockSpec((None, 8, 2 * p), lambda i: (i, 0, 0))],
        out_shape=[jax.ShapeDtypeStruct((g, S5_KW, S5_KW), BF16)] * 3
        + [jax.ShapeDtypeStruct((npair, 8, 2 * p), F32)],
        compiler_params=_params("arbitrary"),
        name="s5prep",
    )(*args)
    dcol = jnp.tile(d_skip.reshape(g, 1, S5_CH), (1, S5_TC, 1)).reshape(g, S5_KW, 1)
    return mt, pp, qt, am, dcol


def _s5in_kernel(x_ref, gmix_ref, perm_ref, wu_ref, o_ref, hp_scr, *, nchunk, groups):
    blk = S5_TC * S5_TC
    nblk = nchunk // S5_TC
    hs = [_rms(x_ref[jb * blk:(jb + 1) * blk, :], gmix_ref[...]).astype(BF16) for jb in range(nblk)]
    hbs = [_dot(perm_ref[...], h).astype(BF16) for h in hs]
    for jb, hb in enumerate(hbs):
        for s in range(S5_TC):
            hp_scr[s * nchunk + jb * S5_TC:s * nchunk + (jb + 1) * S5_TC, :] = hb[s * S5_TC:(s + 1) * S5_TC, :]
    for sb in range(S5_TC // S5_SB):
        rows = slice(sb * S5_SB * nchunk, (sb + 1) * S5_SB * nchunk)
        ut = _dot_t(wu_ref[...], hp_scr[rows, :])
        for k in range(S5_SB):
            for g in range(groups):
                o_ref[g, sb * S5_SB + k] = (
                    ut[g * S5_CH:(g + 1) * S5_CH, k * nchunk:(k + 1) * nchunk].astype(BF16))


def _s5in(x2d, gmix, wu_t, *, nseq, seq):
    d = x2d.shape[1]
    groups = wu_t.shape[0] // S5_CH
    nchunk = seq // S5_TC
    assert nchunk % S5_TC == 0
    idx = jnp.arange(S5_TC * S5_TC)
    perm = (idx[None, :] == (idx[:, None] % S5_TC) * S5_TC + idx[:, None] // S5_TC).astype(BF16)
    return pl.pallas_call(
        functools.partial(_s5in_kernel, nchunk=nchunk, groups=groups),
        grid=(nseq,),
        in_specs=[pl.BlockSpec((seq, d), lambda b: (b, 0)),
                  _const_spec(gmix.shape), _const_spec(perm.shape), _const_spec(wu_t.shape)],
        out_specs=pl.BlockSpec((groups, S5_TC, S5_CH, nchunk), lambda b: (0, 0, 0, b)),
        out_shape=jax.ShapeDtypeStruct((groups, S5_TC, S5_CH, nseq * nchunk), BF16),
        scratch_shapes=[pltpu.VMEM((seq, d), BF16)],
        compiler_params=_params("arbitrary"),
        name="s5in",
    )(x2d, gmix, perm, wu_t)


def _s5t_kernel(x_ref, mt_ref, p_ref, qt_ref, a_ref, d_ref, x0re_ref, x0im_ref,
                y_ref, sre_ref, sim_ref, ere_scr, eim_scr, xre_scr, xim_scr, *, nb, nchunk):
    tdims = (((0,), (0,)), ((), ()))
    e = (lax.dot_general(x_ref[0], p_ref[0], tdims, preferred_element_type=F32)
         + lax.dot_general(x_ref[1], p_ref[1], tdims, preferred_element_type=F32))
    ere_scr[...] = e[:, :LANES]
    eim_scr[...] = e[:, LANES:]
    ar, ai = a_ref[0:1, :], a_ref[1:2, :]

    def step(j, carry):
        re, im = carry
        rows = pl.ds(j, nb, stride=nchunk)
        xre_scr[rows, :] = re
        xim_scr[rows, :] = im
        return ar * re - ai * im + ere_scr[rows, :], ar * im + ai * re + eim_scr[rows, :]

    re, im = lax.fori_loop(0, nchunk, step, (x0re_ref[...], x0im_ref[...]),
                           unroll=8 if nchunk % 8 == 0 else 1)
    sre_ref[...] = re
    sim_ref[...] = im
    xs = jnp.concatenate([xre_scr[...], xim_scr[...]], axis=1).astype(BF16)
    for g in range(2):
        y_ref[g] = (_dot(mt_ref[g], x_ref[g]) + _dot_t(qt_ref[g], xs)
                    + x_ref[g].astype(F32) * d_ref[g]).astype(BF16)


def _s5t(u2t, mt, pp, qt, am, dcol, x0re, x0im, *, nb, nchunk):
    g, kw, total = u2t.shape
    cols = nb * nchunk
    opspec = pl.BlockSpec((2, kw, kw), lambda p, r: (p, 0, 0))
    stspec = pl.BlockSpec((nb, LANES), lambda p, r: (r, p))
    return pl.pallas_call(
        functools.partial(_s5t_kernel, nb=nb, nchunk=nchunk),
        grid=(g // 2, total // cols),
        in_specs=[pl.BlockSpec((2, kw, cols), lambda p, r: (p, 0, r)),
                  opspec, opspec, opspec,
                  pl.BlockSpec((None, 8, LANES), lambda p, r: (p, 0, 0)),
                  pl.BlockSpec((2, kw, 1), lambda p, r: (p, 0, 0)),
                  stspec, stspec],
        out_specs=[pl.BlockSpec((2, kw, cols), lambda p, r: (p, 0, r)), stspec, stspec],
        out_shape=[jax.ShapeDtypeStruct(u2t.shape, BF16),
                   jax.ShapeDtypeStruct(x0re.shape, F32),
                   jax.ShapeDtypeStruct(x0im.shape, F32)],
        scratch_shapes=[pltpu.VMEM((cols, LANES), F32)] * 4,
        compiler_params=_params("arbitrary", "arbitrary"),
        name="s5t",
    )(u2t, mt, pp, qt, am, dcol, x0re, x0im)


def _gelu(x):
    c = math.sqrt(2.0 / math.pi)
    hx = 0.5 * x
    return hx + hx * jnp.tanh(x * (c + (c * 0.044715) * (x * x)))


def _sigmoid(x):
    return 1.0 / (1.0 + jnp.exp(-x))


def _s5out_kernel(y_ref, wglut_ref, b_ref, g_ref, o_ref, z_scr, t_scr, *, nchunk, groups):
    width = groups * S5_CH
    for tb in range(S5_TC // S5_SB):
        for k in range(S5_SB):
            for g in range(groups):
                z_scr[g * S5_CH:(g + 1) * S5_CH, k * nchunk:(k + 1) * nchunk] = (
                    y_ref[g, tb * S5_SB + k].astype(F32))
        z = _gelu(z_scr[...])
        ssm = z * _sigmoid(_dot(wglut_ref[...], z.astype(BF16)) + b_ref[...])
        sn = ssm * lax.rsqrt(jnp.mean(ssm * ssm, axis=0, keepdims=True) + EPS) * g_ref[...]
        for k in range(S5_SB):
            snt = sn[:, k * nchunk:(k + 1) * nchunk].T
            for c in range(width // LANES):
                t_scr[c, pl.ds(tb * S5_SB + k, nchunk, stride=S5_TC), :] = snt[:, c * LANES:(c + 1) * LANES]
    for c in range(width // LANES):
        o_ref[:, c * LANES:(c + 1) * LANES] = t_scr[c].astype(BF16)


def _s5out(y2t, wglu_t, bcol, gcol, *, nseq, seq):
    groups = y2t.shape[0]
    width = groups * S5_CH
    nchunk = seq // S5_TC
    return pl.pallas_call(
        functools.partial(_s5out_kernel, nchunk=nchunk, groups=groups),
        grid=(nseq,),
        in_specs=[pl.BlockSpec((groups, S5_TC, S5_CH, nchunk), lambda b: (0, 0, 0, b)),
                  _const_spec(wglu_t.shape), _const_spec(bcol.shape), _const_spec(gcol.shape)],
        out_specs=pl.BlockSpec((seq, width), lambda b: (b, 0)),
        out_shape=jax.ShapeDtypeStruct((nseq * seq, width), BF16),
        scratch_shapes=[pltpu.VMEM((width, S5_SB * nchunk), F32),
                        pltpu.VMEM((width // LANES, seq, LANES), F32)],
        compiler_params=_params("arbitrary"),
        name="s5out",
    )(y2t, wglu_t, bcol, gcol)


def _s5_branch(x2d, gmix, wu_t, ops, wglu_t, bcol, gcol, x0_re, x0_im, *, batch, seq):
    groups = wu_t.shape[0] // S5_CH
    nchunk = seq // S5_TC
    nseq, run = (batch, seq) if nchunk % S5_TC == 0 else (1, batch * seq)
    u2t = _s5in(x2d, gmix, wu_t, nseq=nseq, seq=run)
    nb = 16 if (batch % 16 == 0 and nchunk % LANES == 0) else batch
    y2t, s_re, s_im = _s5t(u2t.reshape(groups, S5_KW, -1), *ops, x0_re, x0_im, nb=nb, nchunk=nchunk)
    s5n = _s5out(y2t.reshape(u2t.shape), wglu_t, bcol, gcol, nseq=nseq, seq=run)
    return s5n, s_re, s_im


def _ffn_kernel(x_ref, a_ref, s5_ref, gmla_ref, wout_ref, gffn_ref, wgate_ref, wup_ref, wdown_ref,
                gfin_ref, o_ref):
    tm = x_ref.shape[0]
    parts = [slice(i * (tm // FFN_PARTS), (i + 1) * (tm // FFN_PARTS)) for i in range(FFN_PARTS)]
    mixed = [jnp.concatenate([_rms(a_ref[r, :].astype(F32), gmla_ref[...]).astype(BF16), s5_ref[r, :]],
                             axis=-1) for r in parts]
    x1 = [x_ref[r, :] + _dot(m, wout_ref[...]) for r, m in zip(parts, mixed)]
    h2 = [_rms(x, gffn_ref[...]).astype(BF16) for x in x1]
    gate = [_dot(h, wgate_ref[...]) for h in h2]
    up = [_dot(h, wup_ref[...]) for h in h2]
    act = [(g * _sigmoid(g) * u).astype(BF16) for g, u in zip(gate, up)]
    x2 = [x + _dot(a, wdown_ref[...]) for x, a in zip(x1, act)]
    for r, x in zip(parts, x2):
        o_ref[r, :] = _rms(x, gfin_ref[...])


def _ffn(x2d, attn, s5n, gmla, wout, gffn, wgate, wup, wdown, gfin, *, tm):
    t, d = x2d.shape
    row = lambda w: pl.BlockSpec((tm, w), lambda i: (i, 0))
    resident = lambda a: pl.BlockSpec(a.shape, lambda i: (0,) * a.ndim, pipeline_mode=pl.Buffered(1))
    consts = (gmla, wout, gffn, wgate, wup, wdown, gfin)
    return pl.pallas_call(
        _ffn_kernel,
        grid=(t // tm,),
        in_specs=[row(d), row(attn.shape[1]), row(s5n.shape[1])] + [resident(a) for a in consts],
        out_specs=row(d),
        out_shape=jax.ShapeDtypeStruct((t, d), F32),
        compiler_params=_params("arbitrary"),
        name="ffn",
    )(x2d, attn, s5n, *consts)


def _rope_tabs(pos, tm):
    inv = 1.0 / (ROPE_BASE ** (jnp.arange(0, QK_ROPE, 2, dtype=F32) / QK_ROPE))
    ang = pos.astype(F32)[:, None] * inv[None, :]
    cos, sin = jnp.cos(ang), jnp.sin(ang)
    n = pos.shape[0]
    z = lambda w: jnp.zeros((n, w), F32)
    c = jnp.concatenate([jnp.ones((n, QK_NOPE), F32), cos, cos, z(LANES - QK_NOPE - QK_ROPE)], axis=1)
    s = jnp.concatenate([z(QK_NOPE), sin, sin, z(LANES - QK_NOPE - QK_ROPE)], axis=1)
    reps = max(1, tm // n)
    return tuple(jnp.tile(t, (reps, 1)) for t in (c * Q_SCALE, s * Q_SCALE, c, s))


def _pad_heads(w, width):
    r = w.shape[0]
    return jnp.pad(w.reshape(r, HEADS, width), ((0, 0), (0, 0), (0, LANES - width))).reshape(r, HEADS * LANES)


def _layer_weights(w_in, w_uq, w_ukv, q_lora, kv_lora):
    o2, o3 = q_lora + kv_lora, q_lora + kv_lora + QK_ROPE
    d = w_in.shape[0]
    half = QK_ROPE // 2
    swapped = lambda w: jnp.concatenate([-w[..., half:], w[..., :half]], axis=-1)
    w_kr = w_in[:, o2:o3]
    win = jnp.concatenate([w_in[:, :o2], jnp.zeros((d, QK_NOPE), F32), w_kr, swapped(w_kr)],
                          axis=1).astype(BF16)
    wu_t = w_in[:, o3:].T.astype(BF16)
    uq = w_uq.reshape(q_lora, HEADS, QK_NOPE + QK_ROPE)
    wuq = jnp.concatenate([uq, swapped(uq[..., QK_NOPE:])], axis=-1).reshape(q_lora, HEADS * LANES)
    wuq = wuq.astype(BF16)
    ukv = w_ukv.reshape(kv_lora, HEADS, QK_NOPE + V_HEAD)
    w_uk, w_uv = ukv[..., :QK_NOPE], ukv[..., QK_NOPE:]
    wuk = _pad_heads(w_uk.reshape(kv_lora, HEADS * QK_NOPE), QK_NOPE).astype(BF16)
    wuv_t = w_uv.reshape(kv_lora, HEADS * V_HEAD).T.astype(BF16)
    wabs = jnp.zeros((HEADS, LANES, kv_lora + LANES), F32)
    wabs = wabs.at[:, :QK_NOPE, :kv_lora].set(w_uk.transpose(1, 2, 0))
    wabs = wabs.at[:, QK_NOPE:QK_NOPE + QK_ROPE, kv_lora:kv_lora + QK_ROPE].set(
        jnp.broadcast_to(jnp.eye(QK_ROPE, dtype=F32), (HEADS, QK_ROPE, QK_ROPE)))
    wuv = jnp.zeros((HEADS, kv_lora, HEADS, V_HEAD), F32)
    wuv = wuv.at[jnp.arange(HEADS), :, jnp.arange(HEADS), :].set(w_uv.transpose(1, 0, 2))
    wuv = wuv.reshape(HEADS, kv_lora, HEADS * V_HEAD)
    return win, wu_t, wuq, wuk, wuv_t, wabs.astype(BF16), wuv.astype(BF16)


def _token_tile(batch, seq, cap):
    for tm in (1024, 512, 256, 128, 64, 32, 16):
        if tm <= cap and (batch * seq) % tm == 0 and (seq % tm == 0 or tm % seq == 0):
            return tm
    raise ValueError(f"no token tile for batch={batch} seq={seq}")


def kernel(x_prompt, x_sample, cache_mla_ckv, cache_mla_krope, state_s5_re, state_s5_im,
           g_mix, w_in, g_q, w_uq, g_kv, w_ukv,
           s5_a_re, s5_a_im, s5_log_dt, s5_b_re, s5_b_im, s5_c_re, s5_c_im, s5_d, w_glu, b_glu,
           g_out_mla, g_out_s5, w_out, g_ffn, w_gate, w_up, w_down, g_final):
    bp, lp, d = x_prompt.shape
    bs, ls, _ = x_sample.shape
    past = cache_mla_ckv.shape[2]
    q_lora, kv_lora = g_q.shape[1], g_kv.shape[1]
    groups, state = s5_a_re.shape[1], s5_a_re.shape[2]
    assert g_mix.shape[0] == 1, "single-layer model"
    assert state == S5_STATE and s5_b_re.shape[-1] == S5_CH and groups % 2 == 0
    assert lp % S5_TC == 0 and ls % S5_TC == 0 and bp % 8 == 0 and bs % 8 == 0

    xp = x_prompt.reshape(bp * lp, d)
    xs = x_sample.reshape(bs * ls, d)
    tmp, tms = _token_tile(bp, lp, FFN_TM), _token_tile(bs, ls, FFN_TM)
    tpp, tps = _token_tile(bp, lp, PROJ_TM), _token_tile(bs, ls, PROJ_TM)
    tabs_p = _rope_tabs(jnp.arange(lp), tpp)
    tabs_s = _rope_tabs(past + jnp.arange(ls), tps)
    row2 = lambda a: a.reshape(1, -1)

    win, wu_t, wuq, wuk, wuv_t, wabs, wuv = _layer_weights(w_in[0], w_uq[0], w_ukv[0], q_lora, kv_lora)
    ops = _s5_operators(s5_a_re[0], s5_a_im[0], s5_log_dt[0], s5_b_re[0], s5_b_im[0],
                        s5_c_re[0], s5_c_im[0], s5_d[0])
    s5_w = (row2(g_mix[0]), wu_t, ops, w_glu[0].T.astype(BF16), b_glu[0].reshape(-1, 1),
            g_out_s5[0].reshape(-1, 1))
    ffn_w = (row2(g_out_mla[0]), w_out[0].astype(BF16), row2(g_ffn[0]), w_gate[0].astype(BF16),
             w_up[0].astype(BF16), w_down[0].astype(BF16), row2(g_final))
    proj_w = (row2(g_mix[0]), win, row2(g_q[0]), wuq, row2(g_kv[0]), wuk, wuv_t)

    q, k, vt, p_ckv, p_kr = _proj(xp, tabs_p, *proj_w, tm=tpp, emit_kv=True)
    attn = _attn_prompt(q, k, vt, batch=bp, seq=lp, tq=min(lp, ATTN_TQ))
    zeros = jnp.zeros((bp, groups * state), F32)
    s5n, p_re, p_im = _s5_branch(xp, *s5_w, zeros, zeros, batch=bp, seq=lp)
    yp = _ffn(xp, attn, s5n, *ffn_w, tm=tmp)

    q, s_ckv, s_kr = _proj(xs, tabs_s, *proj_w, tm=tps, emit_kv=False)
    attn = _attn_sample(q, cache_mla_ckv[0].reshape(bs * past, kv_lora),
                        cache_mla_krope[0].reshape(bs * past, QK_ROPE), s_ckv, s_kr, wabs, wuv,
                        batch=bs, dec=ls, past=past)
    s5n, s_re, s_im = _s5_branch(xs, *s5_w, state_s5_re[0].reshape(bs, groups * state),
                                 state_s5_im[0].reshape(bs, groups * state), batch=bs, seq=ls)
    ys = _ffn(xs, attn, s5n, *ffn_w, tm=tms)

    return (yp.reshape(bp, lp, d), ys.reshape(bs, ls, d),
            p_ckv.reshape(1, bp, lp, kv_lora), p_kr.reshape(1, bp, lp, QK_ROPE),
            p_re.reshape(1, bp, groups, state), p_im.reshape(1, bp, groups, state),
            s_ckv.reshape(1, bs, ls, kv_lora), s_kr.reshape(1, bs, ls, QK_ROPE),
            s_re.reshape(1, bs, groups, state), s_im.reshape(1, bs, groups, state))
```

```python
import functools
import math

import jax
import jax.numpy as jnp
from jax import lax
from jax.experimental import pallas as pl
from jax.experimental.pallas import tpu as pltpu

F32 = jnp.float32
BF16 = jnp.bfloat16

EPS = 1e-6
NEG_INF = -1e30
CHUNK_SHIFT = 6
ATTN_TQ = 1024
ATTN_HB = 4
PROJ_TM = 1024
PROJ_PARTS = 2
FFN_TM = 512
FFN_PARTS = 2
HEADS = 8
QK_NOPE = 64
QK_ROPE = 32
V_HEAD = 64
ROPE_BASE = 10000.0
Q_SCALE = (QK_NOPE + QK_ROPE) ** -0.5 * math.log2(math.e)
S5_CH = 16
S5_STATE = 64
S5_TC = 16
S5_KW = S5_TC * S5_CH
S5_SB = 4
LANES = 128
VMEM_LIMIT = 56 * 1024 * 1024


def _rms(x, g):
    return x * lax.rsqrt(jnp.mean(x * x, axis=-1, keepdims=True) + EPS) * g


def _dot(a, b, precision=None):
    return jnp.dot(a, b, preferred_element_type=F32, precision=precision)


def _dot_t(a, b):
    return lax.dot_general(a, b, (((1,), (1,)), ((), ())), preferred_element_type=F32)


def _const_spec(shape):
    return pl.BlockSpec(shape, lambda *_: (0,) * len(shape))


def _params(*semantics, **kw):
    return pltpu.CompilerParams(dimension_semantics=semantics, vmem_limit_bytes=VMEM_LIMIT, **kw)


def _rope(t, c, s):
    return t * c + pltpu.roll(t, LANES - QK_ROPE, 1) * s


def _proj_kernel(x_ref, c_ref, s_ref, gmix_ref, win_ref, gq_ref, wuq_ref, gkv_ref, *rest,
                 q_lora, kv_lora, emit_kv, ntab):
    rest = list(rest)
    wuk_ref, wuvt_ref = (rest.pop(0), rest.pop(0)) if emit_kv else (None, None)
    q_ref = rest.pop(0)
    k_ref, vt_ref = (rest.pop(0), rest.pop(0)) if emit_kv else (None, None)
    ckv_ref, kr_ref = rest
    o1 = q_lora
    o2 = o1 + kv_lora
    tm = x_ref.shape[0]
    rs = tm // PROJ_PARTS
    parts = [slice(i * rs, (i + 1) * rs) for i in range(PROJ_PARTS)]
    off = (pl.program_id(0) % ntab) * tm
    tabs = [(c_ref[pl.ds(pl.multiple_of(off + i * rs, 8), rs), :],
             s_ref[pl.ds(pl.multiple_of(off + i * rs, 8), rs), :]) for i in range(PROJ_PARTS)]
    hs = [_rms(x_ref[r, :], gmix_ref[...]).astype(BF16) for r in parts]
    projs = [_dot(h, win_ref[...]) for h in hs]
    qs = [_dot(_rms(p[:, :o1], gq_ref[...]).astype(BF16), wuq_ref[...]) for p in projs]
    ckvs = [_rms(p[:, o1:o2], gkv_ref[...]) for p in projs]
    krs = [_rope(p[:, o2:], c, s) for p, (c, s) in zip(projs, tabs)]
    for r, ckv, kr in zip(parts, ckvs, krs):
        ckv_ref[r, :] = ckv
        kr_ref[r, :] = kr[:, QK_NOPE:QK_NOPE + QK_ROPE]
    if emit_kv:
        ckv_bs = [ckv.astype(BF16) for ckv in ckvs]
        kns = [_dot(c, wuk_ref[...]) for c in ckv_bs]
        for r, c in zip(parts, ckv_bs):
            vt_ref[:, r] = _dot_t(wuvt_ref[...], c).astype(BF16)
    for i, r in enumerate(parts):
        cq, sq = tabs[i][0] * Q_SCALE, tabs[i][1] * Q_SCALE
        for hh in range(HEADS):
            sl = slice(hh * LANES, (hh + 1) * LANES)
            q_ref[r, sl] = _rope(qs[i][:, sl], cq, sq).astype(BF16)
            if emit_kv:
                k_ref[r, sl] = (kns[i][:, sl] + krs[i]).astype(BF16)


def _proj(x2d, tabs, gmix, win, gq, wuq, gkv, wuk, wuv_t, *, tm, emit_kv):
    t, d = x2d.shape
    q_lora, kv_lora = gq.shape[1], gkv.shape[1]
    ntab = tabs[0].shape[0] // tm
    tab_spec = _const_spec(tabs[0].shape)
    row = lambda w: pl.BlockSpec((tm, w), lambda i: (i, 0))
    specs = [row(HEADS * LANES)]
    shapes = [jax.ShapeDtypeStruct((t, HEADS * LANES), BF16)]
    if emit_kv:
        specs += [row(HEADS * LANES), pl.BlockSpec((HEADS * V_HEAD, tm), lambda i: (0, i))]
        shapes += [jax.ShapeDtypeStruct((t, HEADS * LANES), BF16),
                   jax.ShapeDtypeStruct((HEADS * V_HEAD, t), BF16)]
    specs += [row(kv_lora), row(QK_ROPE)]
    shapes += [jax.ShapeDtypeStruct((t, kv_lora), F32), jax.ShapeDtypeStruct((t, QK_ROPE), F32)]
    weights = (gmix, win, gq, wuq, gkv) + ((wuk, wuv_t) if emit_kv else ())
    return pl.pallas_call(
        functools.partial(_proj_kernel, q_lora=q_lora, kv_lora=kv_lora, emit_kv=emit_kv, ntab=ntab),
        grid=(t // tm,),
        in_specs=[row(d)] + [tab_spec] * len(tabs) + [_const_spec(w.shape) for w in weights],
        out_specs=specs,
        out_shape=shapes,
        compiler_params=_params("arbitrary"),
        name="proj",
    )(x2d, *tabs, *weights)


def _attn_p_kernel(q_ref, k_ref, vt_ref, o_ref, bias_scr, *, tq):
    qi = pl.program_id(2)
    half = tq // 2

    def tile(carry, ks, nk, q0, masked):
        def scores(hh):
            sl = slice(hh * LANES, (hh + 1) * LANES)
            st = _dot_t(k_ref[pl.ds(ks, nk), sl], q_ref[q0:, sl])
            return st + bias_scr[:, :tq - q0] if masked else st

        def softmax(hh, st):
            m, a = carry[hh]
            mn = jnp.maximum(m, jnp.max(st, axis=0, keepdims=True))
            return mn, jnp.exp2(st - mn).astype(BF16), jnp.exp2(m - mn) * a

        def values(hh, mn, p, a):
            vt = jnp.concatenate([vt_ref[hh * V_HEAD:(hh + 1) * V_HEAD, pl.ds(ks, nk)], ones[:, :nk]],
                                 axis=0)
            return mn, a + _dot(vt, p)

        sts = [scores(hh) for hh in range(ATTN_HB)]
        ps = [softmax(hh, sts[hh]) for hh in range(ATTN_HB)]
        out = [values(hh, *ps[hh]) for hh in range(ATTN_HB)]
        return tuple(out)

    @pl.when((pl.program_id(0) == 0) & (pl.program_id(1) == 0) & (qi == 0))
    def _():
        kchunk = lax.broadcasted_iota(jnp.int32, (half, tq), 0) >> CHUNK_SHIFT
        qchunk = lax.broadcasted_iota(jnp.int32, (half, tq), 1) >> CHUNK_SHIFT
        bias_scr[...] = jnp.where(kchunk <= qchunk, 0.0, NEG_INF)

    ones = jnp.ones((16, tq), BF16)
    init = (jnp.full((1, tq), NEG_INF, F32), jnp.zeros((V_HEAD + 16, tq), F32))
    carry = lax.fori_loop(0, qi, lambda kt, c: tile(c, pl.multiple_of(kt * tq, tq), tq, 0, False),
                          (init,) * ATTN_HB)
    diag = pl.multiple_of(qi * tq, tq)
    carry = tile(carry, diag, half, 0, True)
    right = tile(tuple((m[:, half:], a[:, half:]) for m, a in carry),
                 pl.multiple_of(diag + half, half), half, half, True)
    carry = tuple((jnp.concatenate([m[:, :half], mr], axis=1), jnp.concatenate([a[:, :half], ar], axis=1))
                  for (m, a), (mr, ar) in zip(carry, right))
    ot = jnp.concatenate([a[:V_HEAD] / a[V_HEAD:V_HEAD + 1] for _, a in carry], axis=0)
    o_ref[...] = ot.T.astype(BF16)


def _attn_prompt(q, k, vt, *, batch, seq, tq):
    assert seq % tq == 0 and tq % (1 << CHUNK_SHIFT) == 0 and HEADS % ATTN_HB == 0
    nq = seq // tq
    return pl.pallas_call(
        functools.partial(_attn_p_kernel, tq=tq),
        grid=(batch, HEADS // ATTN_HB, nq),
        in_specs=[pl.BlockSpec((tq, ATTN_HB * LANES), lambda b, h, i: (b * nq + i, h)),
                  pl.BlockSpec((seq, ATTN_HB * LANES), lambda b, h, i: (b, h)),
                  pl.BlockSpec((ATTN_HB * V_HEAD, seq), lambda b, h, i: (h, b))],
        out_specs=pl.BlockSpec((tq, ATTN_HB * V_HEAD), lambda b, h, i: (b * nq + i, h)),
        out_shape=jax.ShapeDtypeStruct((batch * seq, HEADS * V_HEAD), BF16),
        scratch_shapes=[pltpu.VMEM((tq // 2, tq), F32)],
        compiler_params=_params("arbitrary", "arbitrary", "arbitrary"),
        name="attn_p",
    )(q, k, vt)


def _attn_s_kernel(q_ref, pc_ref, pk_ref, nc_ref, nk_ref, wabs_ref, wuv_ref, o_ref, *,
                   dec, past, kv_lora):
    q = q_ref[...]
    qf = jnp.concatenate(
        [_dot(q[:, hh * LANES:(hh + 1) * LANES], wabs_ref[hh]) for hh in range(HEADS)],
        axis=0).astype(BF16)
    qa, qr = qf[:, :kv_lora], qf[:, kv_lora:kv_lora + QK_ROPE]
    pc = pc_ref[...].astype(BF16)
    nc = nc_ref[...].astype(BF16)
    s_p = _dot_t(qa, pc) + _dot_t(qr, pk_ref[...].astype(BF16))
    s_n = _dot_t(qa, nc) + _dot_t(qr, nk_ref[...].astype(BF16))
    rows = HEADS * dec
    qpos = past + jnp.concatenate([lax.broadcasted_iota(jnp.int32, (dec, 1), 0)] * HEADS, axis=0)
    qchunk = qpos >> CHUNK_SHIFT
    kchunk_p = lax.broadcasted_iota(jnp.int32, (rows, past), 1) >> CHUNK_SHIFT
    kchunk_n = (past + lax.broadcasted_iota(jnp.int32, (rows, dec), 1)) >> CHUNK_SHIFT
    s_p = jnp.where(kchunk_p <= qchunk, s_p, NEG_INF)
    s_n = jnp.where(kchunk_n <= qchunk, s_n, NEG_INF)
    m = jnp.maximum(jnp.max(s_p, axis=-1, keepdims=True), jnp.max(s_n, axis=-1, keepdims=True))
    p_p = jnp.exp2(s_p - m)
    p_n = jnp.exp2(s_n - m)
    l = jnp.sum(p_p, axis=-1, keepdims=True) + jnp.sum(p_n, axis=-1, keepdims=True)
    olat = ((_dot(p_p.astype(BF16), pc) + _dot(p_n.astype(BF16), nc)) / l).astype(BF16)
    acc = jnp.zeros((dec, HEADS * V_HEAD), F32)
    for hh in range(HEADS):
        acc = acc + _dot(olat[hh * dec:(hh + 1) * dec], wuv_ref[hh])
    o_ref[...] = acc.astype(BF16)


def _attn_sample(q, past_ckv, past_kr, new_ckv, new_kr, wabs, wuv, *, batch, dec, past):
    kv_lora = past_ckv.shape[-1]
    return pl.pallas_call(
        functools.partial(_attn_s_kernel, dec=dec, past=past, kv_lora=kv_lora),
        grid=(batch,),
        in_specs=[pl.BlockSpec((dec, HEADS * LANES), lambda b: (b, 0)),
                  pl.BlockSpec((past, kv_lora), lambda b: (b, 0)),
                  pl.BlockSpec((past, QK_ROPE), lambda b: (b, 0)),
                  pl.BlockSpec((dec, kv_lora), lambda b: (b, 0)),
                  pl.BlockSpec((dec, QK_ROPE), lambda b: (b, 0)),
                  _const_spec(wabs.shape), _const_spec(wuv.shape)],
        out_specs=pl.BlockSpec((dec, HEADS * V_HEAD), lambda b: (b, 0)),
        out_shape=jax.ShapeDtypeStruct((batch * dec, HEADS * V_HEAD), BF16),
        compiler_params=_params("arbitrary"),
        name="attn_s",
    )(q, past_ckv, past_kr, new_ckv, new_kr, wabs, wuv)


def _s5prep_kernel(lr_ref, li_ref, ldt_ref, brt_ref, bit_ref, crt_ref, cit_ref, cr_ref, ci_ref,
                   arow_ref, airow_ref, ldtrow_ref, mt_ref, pp_ref, qt_ref, am_ref):
    hi = lax.Precision.HIGHEST
    lane = lax.broadcasted_iota(jnp.int32, (1, S5_KW), 1)
    pos = (lane >> 4).astype(F32)
    zeros = jnp.zeros((S5_STATE, S5_KW), F32)

    def powers(lrdt, lidt, expo):
        mag, ang = jnp.exp(expo * lrdt), expo * lidt
        return mag * jnp.cos(ang), mag * jnp.sin(ang)

    for gl in range(2):
        lr, li = lr_ref[gl], li_ref[gl]
        dt = jnp.exp(ldt_ref[gl])
        lrdt, lidt = lr * dt, li * dt
        abr, abi = powers(lrdt, lidt, 1.0)
        den = lr * lr + li * li
        ir, ii = lr / den, -li / den
        zr, zi = (abr - 1.0) * ir - abi * ii, (abr - 1.0) * ii + abi * ir
        bbr = zr * brt_ref[gl] - zi * bit_ref[gl]
        bbi = zr * bit_ref[gl] + zi * brt_ref[gl]
        pr, pi = powers(lrdt, lidt, (S5_TC - 1) - pos)
        pt_re, pt_im = pr * bbr - pi * bbi, pr * bbi + pi * bbr
        krev = _dot(cr_ref[gl], pt_re, hi) - _dot(ci_ref[gl], pt_im, hi)
        for t in range(S5_TC):
            shift = (S5_TC - 1 - t) * S5_CH
            rolled = krev if shift == 0 else pltpu.roll(krev, S5_KW - shift, 1)
            mt_ref[gl, t * S5_CH:(t + 1) * S5_CH, :] = jnp.where(
                lane < S5_KW - shift, rolled, 0.0).astype(BF16)
        qr, qi = powers(lrdt, lidt, pos + 1.0)
        q_re = crt_ref[gl] * qr - cit_ref[gl] * qi
        q_im = -(crt_ref[gl] * qi + cit_ref[gl] * qr)
        own = lambda a, b: [a, zeros, b, zeros] if gl == 0 else [zeros, a, zeros, b]
        pp_ref[gl] = jnp.concatenate(own(pt_re, pt_im), axis=0).T.astype(BF16)
        qt_ref[gl] = jnp.concatenate(own(q_re, q_im), axis=0).T.astype(BF16)
    dtrow = jnp.exp(ldtrow_ref[...])
    ar, ai = powers(arow_ref[...] * dtrow, airow_ref[...] * dtrow, float(S5_TC))
    am_ref[...] = jnp.concatenate([ar, ai, jnp.zeros((6, 2 * S5_STATE), F32)], axis=0)


def _s5_operators(a_re, a_im, log_dt, b_re, b_im, c_re, c_im, d_skip):
    g, p = a_re.shape
    npair = g // 2
    col = lambda a: a.reshape(g, p, 1)
    tile_s = lambda a: jnp.tile(a, (1, 1, S5_TC))
    pair = lambda a: a.reshape(npair, 1, 2 * p)
    ldt_col = log_dt.reshape(g, 1, 1)
    ldt_row = jnp.repeat(log_dt, p).reshape(npair, 1, 2 * p)
    args = (col(a_re), col(a_im), ldt_col, tile_s(b_re), tile_s(b_im),
            tile_s(c_re.transpose(0, 2, 1)), tile_s(c_im.transpose(0, 2, 1)), c_re, c_im,
            pair(a_re), pair(a_im), ldt_row)
    spec = lambda a: pl.BlockSpec((2,) + a.shape[1:], lambda i: (i,) + (0,) * (a.ndim - 1))
    pspec = lambda a: pl.BlockSpec((None,) + a.shape[1:], lambda i: (i,) + (0,) * (a.ndim - 1))
    opspec = pl.BlockSpec((2, S5_KW, S5_KW), lambda i: (i, 0, 0))
    mt, pp, qt, am = pl.pallas_call(
        _s5prep_kernel,
        grid=(npair,),
        in_specs=[spec(a) for a in args[:9]] + [pspec(a) for a in args[9:]],
        out_specs=[opspec, opspec, opspec, pl.BlockSpec((None, 8, 2 * p), lambda i: (i, 0, 0))],
        out_shape=[jax.ShapeDtypeStruct((g, S5_KW, S5_KW), BF16)] * 3
        + [jax.ShapeDtypeStruct((npair, 8, 2 * p), F32)],
        compiler_params=_params("arbitrary"),
        name="s5prep",
    )(*args)
    dcol = jnp.tile(d_skip.reshape(g, 1, S5_CH), (1, S5_TC, 1)).reshape(g, S5_KW, 1)
    return mt, pp, qt, am, dcol


def _s5in_kernel(x_ref, gmix_ref, perm_ref, wu_ref, o_ref, hp_scr, *, nchunk, groups):
    blk = S5_TC * S5_TC
    nblk = nchunk // S5_TC
    hs = [_rms(x_ref[jb * blk:(jb + 1) * blk, :], gmix_ref[...]).astype(BF16) for jb in range(nblk)]
    hbs = [_dot(perm_ref[...], h).astype(BF16) for h in hs]
    for jb, hb in enumerate(hbs):
        for s in range(S5_TC):
            hp_scr[s * nchunk + jb * S5_TC:s * nchunk + (jb + 1) * S5_TC, :] = hb[s * S5_TC:(s + 1) * S5_TC, :]
    for sb in range(S5_TC // S5_SB):
        rows = slice(sb * S5_SB * nchunk, (sb + 1) * S5_SB * nchunk)
        ut = _dot_t(wu_ref[...], hp_scr[rows, :])
        for k in range(S5_SB):
            for g in range(groups):
                o_ref[g, sb * S5_SB + k] = (
                    ut[g * S5_CH:(g + 1) * S5_CH, k * nchunk:(k + 1) * nchunk].astype(BF16))


def _chunk_block_perm():
    idx = jnp.arange(S5_TC * S5_TC)
    return (idx[None, :] == (idx[:, None] % S5_TC) * S5_TC + idx[:, None] // S5_TC).astype(BF16)


def _s5in(x2d, gmix, wu_t, *, nseq, seq):
    d = x2d.shape[1]
    groups = wu_t.shape[0] // S5_CH
    nchunk = seq // S5_TC
    assert nchunk % S5_TC == 0
    perm = _chunk_block_perm()
    return pl.pallas_call(
        functools.partial(_s5in_kernel, nchunk=nchunk, groups=groups),
        grid=(nseq,),
        in_specs=[pl.BlockSpec((seq, d), lambda b: (b, 0)),
                  _const_spec(gmix.shape), _const_spec(perm.shape), _const_spec(wu_t.shape)],
        out_specs=pl.BlockSpec((groups, S5_TC, S5_CH, nchunk), lambda b: (0, 0, 0, b)),
        out_shape=jax.ShapeDtypeStruct((groups, S5_TC, S5_CH, nseq * nchunk), BF16),
        scratch_shapes=[pltpu.VMEM((seq, d), BF16)],
        compiler_params=_params("arbitrary"),
        name="s5in",
    )(x2d, gmix, perm, wu_t)


def _s5t_kernel(x_ref, mt_ref, p_ref, qt_ref, a_ref, d_ref, x0re_ref, x0im_ref,
                y_ref, sre_ref, sim_ref, ere_scr, eim_scr, xre_scr, xim_scr, *, nb, nchunk):
    tdims = (((0,), (0,)), ((), ()))
    e = (lax.dot_general(x_ref[0], p_ref[0], tdims, preferred_element_type=F32)
         + lax.dot_general(x_ref[1], p_ref[1], tdims, preferred_element_type=F32))
    ere_scr[...] = e[:, :LANES]
    eim_scr[...] = e[:, LANES:]
    local = [_dot(mt_ref[g], x_ref[g]) + x_ref[g].astype(F32) * d_ref[g] for g in range(2)]
    ar, ai = a_ref[0:1, :], a_ref[1:2, :]
    re, im = x0re_ref[...], x0im_ref[...]
    for j in range(nchunk):
        rows = pl.ds(j, nb, stride=nchunk)
        xre_scr[rows, :] = re
        xim_scr[rows, :] = im
        re, im = ar * re - ai * im + ere_scr[rows, :], ar * im + ai * re + eim_scr[rows, :]
    sre_ref[...] = re
    sim_ref[...] = im
    xs = jnp.concatenate([xre_scr[...], xim_scr[...]], axis=1).astype(BF16)
    for g in range(2):
        y_ref[g] = (local[g] + _dot_t(qt_ref[g], xs)).astype(BF16)


def _s5t(u2t, mt, pp, qt, am, dcol, x0re, x0im, *, nb, nchunk):
    g, kw, total = u2t.shape
    cols = nb * nchunk
    opspec = pl.BlockSpec((2, kw, kw), lambda p, r: (p, 0, 0))
    stspec = pl.BlockSpec((nb, LANES), lambda p, r: (r, p))
    return pl.pallas_call(
        functools.partial(_s5t_kernel, nb=nb, nchunk=nchunk),
        grid=(g // 2, total // cols),
        in_specs=[pl.BlockSpec((2, kw, cols), lambda p, r: (p, 0, r)),
                  opspec, opspec, opspec,
                  pl.BlockSpec((None, 8, LANES), lambda p, r: (p, 0, 0)),
                  pl.BlockSpec((2, kw, 1), lambda p, r: (p, 0, 0)),
                  stspec, stspec],
        out_specs=[pl.BlockSpec((2, kw, cols), lambda p, r: (p, 0, r)), stspec, stspec],
        out_shape=[jax.ShapeDtypeStruct(u2t.shape, BF16),
                   jax.ShapeDtypeStruct(x0re.shape, F32),
                   jax.ShapeDtypeStruct(x0im.shape, F32)],
        scratch_shapes=[pltpu.VMEM((cols, LANES), F32)] * 4,
        compiler_params=_params("arbitrary", "arbitrary"),
        name="s5t",
    )(u2t, mt, pp, qt, am, dcol, x0re, x0im)


def _gelu(x):
    c = math.sqrt(2.0 / math.pi)
    hx = 0.5 * x
    return hx + hx * jnp.tanh(x * (c + (c * 0.044715) * (x * x)))


def _sigmoid(x):
    return 1.0 / (1.0 + jnp.exp(-x))


def _s5out_kernel(y_ref, wglut_ref, b_ref, g_ref, perm_ref, o_ref, z_scr, t_scr, *, nchunk, groups):
    for tb in range(S5_TC // S5_SB):
        for k in range(S5_SB):
            for g in range(groups):
                z_scr[g * S5_CH:(g + 1) * S5_CH, k * nchunk:(k + 1) * nchunk] = (
                    y_ref[g, tb * S5_SB + k].astype(F32))
        z = _gelu(z_scr[...])
        ssm = z * _sigmoid(_dot(wglut_ref[...], z.astype(BF16)) + b_ref[...])
        sn = ssm * lax.rsqrt(jnp.mean(ssm * ssm, axis=0, keepdims=True) + EPS) * g_ref[...]
        for k in range(S5_SB):
            t = tb * S5_SB + k
            t_scr[t * nchunk:(t + 1) * nchunk, :] = sn[:, k * nchunk:(k + 1) * nchunk].T.astype(BF16)
    blk = S5_TC * S5_TC
    for jb in range(nchunk // S5_TC):
        rows = jnp.concatenate([t_scr[t * nchunk + jb * S5_TC:t * nchunk + (jb + 1) * S5_TC, :]
                                for t in range(S5_TC)], axis=0)
        o_ref[jb * blk:(jb + 1) * blk, :] = _dot(perm_ref[...], rows).astype(BF16)


def _s5out(y2t, wglu_t, bcol, gcol, *, nseq, seq):
    groups = y2t.shape[0]
    width = groups * S5_CH
    nchunk = seq // S5_TC
    perm = _chunk_block_perm()
    return pl.pallas_call(
        functools.partial(_s5out_kernel, nchunk=nchunk, groups=groups),
        grid=(nseq,),
        in_specs=[pl.BlockSpec((groups, S5_TC, S5_CH, nchunk), lambda b: (0, 0, 0, b)),
                  _const_spec(wglu_t.shape), _const_spec(bcol.shape), _const_spec(gcol.shape),
                  _const_spec(perm.shape)],
        out_specs=pl.BlockSpec((seq, width), lambda b: (b, 0)),
        out_shape=jax.ShapeDtypeStruct((nseq * seq, width), BF16),
        scratch_shapes=[pltpu.VMEM((width, S5_SB * nchunk), F32), pltpu.VMEM((seq, width), BF16)],
        compiler_params=_params("arbitrary"),
        name="s5out",
    )(y2t, wglu_t, bcol, gcol, perm)


def _s5_branch(x2d, gmix, wu_t, ops, wglu_t, bcol, gcol, x0_re, x0_im, *, batch, seq):
    groups = wu_t.shape[0] // S5_CH
    nchunk = seq // S5_TC
    nseq, run = (batch, seq) if nchunk % S5_TC == 0 else (1, batch * seq)
    u2t = _s5in(x2d, gmix, wu_t, nseq=nseq, seq=run)
    nb = 16 if (batch % 16 == 0 and nchunk % LANES == 0) else batch
    y2t, s_re, s_im = _s5t(u2t.reshape(groups, S5_KW, -1), *ops, x0_re, x0_im, nb=nb, nchunk=nchunk)
    s5n = _s5out(y2t.reshape(u2t.shape), wglu_t, bcol, gcol, nseq=nseq, seq=run)
    return s5n, s_re, s_im


def _ffn_kernel(x_ref, a_ref, s5_ref, gmla_ref, wout_ref, gffn_ref, wgate_ref, wup_ref, wdown_ref,
                gfin_ref, o_ref):
    tm = x_ref.shape[0]
    parts = [slice(i * (tm // FFN_PARTS), (i + 1) * (tm // FFN_PARTS)) for i in range(FFN_PARTS)]
    mixed = [jnp.concatenate([_rms(a_ref[r, :].astype(F32), gmla_ref[...]).astype(BF16), s5_ref[r, :]],
                             axis=-1) for r in parts]
    x1 = [x_ref[r, :] + _dot(m, wout_ref[...]) for r, m in zip(parts, mixed)]
    h2 = [_rms(x, gffn_ref[...]).astype(BF16) for x in x1]
    gate = [_dot(h, wgate_ref[...]) for h in h2]
    up = [_dot(h, wup_ref[...]) for h in h2]
    act = [(g * _sigmoid(g) * u).astype(BF16) for g, u in zip(gate, up)]
    x2 = [x + _dot(a, wdown_ref[...]) for x, a in zip(x1, act)]
    for r, x in zip(parts, x2):
        o_ref[r, :] = _rms(x, gfin_ref[...])


def _ffn(x2d, attn, s5n, gmla, wout, gffn, wgate, wup, wdown, gfin, *, tm):
    t, d = x2d.shape
    row = lambda w: pl.BlockSpec((tm, w), lambda i: (i, 0))
    resident = lambda a: pl.BlockSpec(a.shape, lambda i: (0,) * a.ndim, pipeline_mode=pl.Buffered(1))
    consts = (gmla, wout, gffn, wgate, wup, wdown, gfin)
    return pl.pallas_call(
        _ffn_kernel,
        grid=(t // tm,),
        in_specs=[row(d), row(attn.shape[1]), row(s5n.shape[1])] + [resident(a) for a in consts],
        out_specs=row(d),
        out_shape=jax.ShapeDtypeStruct((t, d), F32),
        compiler_params=_params("arbitrary"),
        name="ffn",
    )(x2d, attn, s5n, *consts)


def _rope_tabs(pos, tm):
    inv = 1.0 / (ROPE_BASE ** (jnp.arange(0, QK_ROPE, 2, dtype=F32) / QK_ROPE))
    ang = pos.astype(F32)[:, None] * inv[None, :]
    cos, sin = jnp.cos(ang), jnp.sin(ang)
    n = pos.shape[0]
    z = lambda w: jnp.zeros((n, w), F32)
    c = jnp.concatenate([jnp.ones((n, QK_NOPE), F32), cos, cos, z(LANES - QK_NOPE - QK_ROPE)], axis=1)
    s = jnp.concatenate([z(QK_NOPE), sin, sin, z(LANES - QK_NOPE - QK_ROPE)], axis=1)
    reps = max(1, tm // n)
    return tuple(jnp.tile(t, (reps, 1)) for t in (c, s))


def _pad_heads(w, width):
    r = w.shape[0]
    return jnp.pad(w.reshape(r, HEADS, width), ((0, 0), (0, 0), (0, LANES - width))).reshape(r, HEADS * LANES)


def _layer_weights(w_in, w_uq, w_ukv, q_lora, kv_lora):
    o2, o3 = q_lora + kv_lora, q_lora + kv_lora + QK_ROPE
    d = w_in.shape[0]
    half = QK_ROPE // 2
    swapped = lambda w: jnp.concatenate([-w[..., half:], w[..., :half]], axis=-1)
    w_kr = w_in[:, o2:o3]
    win = jnp.concatenate([w_in[:, :o2], jnp.zeros((d, QK_NOPE), F32), w_kr, swapped(w_kr)],
                          axis=1).astype(BF16)
    wu_t = w_in[:, o3:].T.astype(BF16)
    uq = w_uq.reshape(q_lora, HEADS, QK_NOPE + QK_ROPE)
    wuq = jnp.concatenate([uq, swapped(uq[..., QK_NOPE:])], axis=-1).reshape(q_lora, HEADS * LANES)
    wuq = wuq.astype(BF16)
    ukv = w_ukv.reshape(kv_lora, HEADS, QK_NOPE + V_HEAD)
    w_uk, w_uv = ukv[..., :QK_NOPE], ukv[..., QK_NOPE:]
    wuk = _pad_heads(w_uk.reshape(kv_lora, HEADS * QK_NOPE), QK_NOPE).astype(BF16)
    wuv_t = w_uv.reshape(kv_lora, HEADS * V_HEAD).T.astype(BF16)
    wabs = jnp.zeros((HEADS, LANES, kv_lora + LANES), F32)
    wabs = wabs.at[:, :QK_NOPE, :kv_lora].set(w_uk.transpose(1, 2, 0))
    wabs = wabs.at[:, QK_NOPE:QK_NOPE + QK_ROPE, kv_lora:kv_lora + QK_ROPE].set(
        jnp.broadcast_to(jnp.eye(QK_ROPE, dtype=F32), (HEADS, QK_ROPE, QK_ROPE)))
    wuv = jnp.zeros((HEADS, kv_lora, HEADS, V_HEAD), F32)
    wuv = wuv.at[jnp.arange(HEADS), :, jnp.arange(HEADS), :].set(w_uv.transpose(1, 0, 2))
    wuv = wuv.reshape(HEADS, kv_lora, HEADS * V_HEAD)
    return win, wu_t, wuq, wuk, wuv_t, wabs.astype(BF16), wuv.astype(BF16)


def _token_tile(batch, seq, cap):
    for tm in (1024, 512, 256, 128, 64, 32, 16):
        if tm <= cap and (batch * seq) % tm == 0 and (seq % tm == 0 or tm % seq == 0):
            return tm
    raise ValueError(f"no token tile for batch={batch} seq={seq}")


def kernel(x_prompt, x_sample, cache_mla_ckv, cache_mla_krope, state_s5_re, state_s5_im,
           g_mix, w_in, g_q, w_uq, g_kv, w_ukv,
           s5_a_re, s5_a_im, s5_log_dt, s5_b_re, s5_b_im, s5_c_re, s5_c_im, s5_d, w_glu, b_glu,
           g_out_mla, g_out_s5, w_out, g_ffn, w_gate, w_up, w_down, g_final):
    bp, lp, d = x_prompt.shape
    bs, ls, _ = x_sample.shape
    past = cache_mla_ckv.shape[2]
    q_lora, kv_lora = g_q.shape[1], g_kv.shape[1]
    groups, state = s5_a_re.shape[1], s5_a_re.shape[2]
    assert g_mix.shape[0] == 1, "single-layer model"
    assert state == S5_STATE and s5_b_re.shape[-1] == S5_CH and groups % 2 == 0
    assert lp % S5_TC == 0 and ls % S5_TC == 0 and bp % 8 == 0 and bs % 8 == 0

    xp = x_prompt.reshape(bp * lp, d)
    xs = x_sample.reshape(bs * ls, d)
    tmp, tms = _token_tile(bp, lp, FFN_TM), _token_tile(bs, ls, FFN_TM)
    tpp, tps = _token_tile(bp, lp, PROJ_TM), _token_tile(bs, ls, PROJ_TM)
    tabs_p = _rope_tabs(jnp.arange(lp), tpp)
    tabs_s = _rope_tabs(past + jnp.arange(ls), tps)
    row2 = lambda a: a.reshape(1, -1)

    win, wu_t, wuq, wuk, wuv_t, wabs, wuv = _layer_weights(w_in[0], w_uq[0], w_ukv[0], q_lora, kv_lora)
    ops = _s5_operators(s5_a_re[0], s5_a_im[0], s5_log_dt[0], s5_b_re[0], s5_b_im[0],
                        s5_c_re[0], s5_c_im[0], s5_d[0])
    s5_w = (row2(g_mix[0]), wu_t, ops, w_glu[0].T.astype(BF16), b_glu[0].reshape(-1, 1),
            g_out_s5[0].reshape(-1, 1))
    ffn_w = (row2(g_out_mla[0]), w_out[0].astype(BF16), row2(g_ffn[0]), w_gate[0].astype(BF16),
             w_up[0].astype(BF16), w_down[0].astype(BF16), row2(g_final))
    proj_w = (row2(g_mix[0]), win, row2(g_q[0]), wuq, row2(g_kv[0]), wuk, wuv_t)

    q, k, vt, p_ckv, p_kr = _proj(xp, tabs_p, *proj_w, tm=tpp, emit_kv=True)
    attn = _attn_prompt(q, k, vt, batch=bp, seq=lp, tq=min(lp, ATTN_TQ))
    zeros = jnp.zeros((bp, groups * state), F32)
    s5n, p_re, p_im = _s5_branch(xp, *s5_w, zeros, zeros, batch=bp, seq=lp)
    yp = _ffn(xp, attn, s5n, *ffn_w, tm=tmp)

    q, s_ckv, s_kr = _proj(xs, tabs_s, *proj_w, tm=tps, emit_kv=False)
    attn = _attn_sample(q, cache_mla_ckv[0].reshape(bs * past, kv_lora),
                        cache_mla_krope[0].reshape(bs * past, QK_ROPE), s_ckv, s_kr, wabs, wuv,
                        batch=bs, dec=ls, past=past)
    s5n, s_re, s_im = _s5_branch(xs, *s5_w, state_s5_re[0].reshape(bs, groups * state),
                                 state_s5_im[0].reshape(bs, groups * state), batch=bs, seq=ls)
    ys = _ffn(xs, attn, s5n, *ffn_w, tm=tms)

    return (yp.reshape(bp, lp, d), ys.reshape(bs, ls, d),
            p_ckv.reshape(1, bp, lp, kv_lora), p_kr.reshape(1, bp, lp, QK_ROPE),
            p_re.reshape(1, bp, groups, state), p_im.reshape(1, bp, groups, state),
            s_ckv.reshape(1, bs, ls, kv_lora), s_kr.reshape(1, bs, ls, QK_ROPE),
            s_re.reshape(1, bs, groups, state), s_im.reshape(1, bs, groups, state))
```

```python
import functools
import math

import jax
import jax.numpy as jnp
from jax import lax
from jax.experimental import pallas as pl
from jax.experimental.pallas import tpu as pltpu

F32 = jnp.float32
BF16 = jnp.bfloat16

EPS = 1e-6
NEG_INF = -1e30
CHUNK_SHIFT = 6
ATTN_TQ = 1024
ATTN_HB = 4
ATTN_S_SEQS = 4
PROJ_TM = 1024
PROJ_PARTS = 2
FFN_TM = 512
FFN_PARTS = 2
HEADS = 8
QK_NOPE = 64
QK_ROPE = 32
V_HEAD = 64
ROPE_BASE = 10000.0
Q_SCALE = (QK_NOPE + QK_ROPE) ** -0.5 * math.log2(math.e)
S5_CH = 16
S5_STATE = 64
S5_TC = 16
S5_KW = S5_TC * S5_CH
S5_SB = 4
LANES = 128
VMEM_LIMIT = 56 * 1024 * 1024


def _rms(x, g):
    return x * lax.rsqrt(jnp.mean(x * x, axis=-1, keepdims=True) + EPS) * g


def _dot(a, b, precision=None):
    return jnp.dot(a, b, preferred_element_type=F32, precision=precision)


def _dot_t(a, b):
    return lax.dot_general(a, b, (((1,), (1,)), ((), ())), preferred_element_type=F32)


def _const_spec(shape):
    return pl.BlockSpec(shape, lambda *_: (0,) * len(shape))


def _params(*semantics, **kw):
    return pltpu.CompilerParams(dimension_semantics=semantics, vmem_limit_bytes=VMEM_LIMIT, **kw)


def _rope(t, c, s):
    return t * c + pltpu.roll(t, LANES - QK_ROPE, 1) * s


def _proj_kernel(x_ref, c_ref, s_ref, gmix_ref, win_ref, gq_ref, wuq_ref, gkv_ref, *rest,
                 q_lora, kv_lora, emit_kv, ntab):
    rest = list(rest)
    wuk_ref, wuvt_ref = (rest.pop(0), rest.pop(0)) if emit_kv else (None, None)
    q_ref = rest.pop(0)
    k_ref, vt_ref = (rest.pop(0), rest.pop(0)) if emit_kv else (None, None)
    ckv_ref, kr_ref = rest
    o1 = q_lora
    o2 = o1 + kv_lora
    tm = x_ref.shape[0]
    rs = tm // PROJ_PARTS
    parts = [slice(i * rs, (i + 1) * rs) for i in range(PROJ_PARTS)]
    off = (pl.program_id(0) % ntab) * tm
    tabs = [(c_ref[pl.ds(pl.multiple_of(off + i * rs, 8), rs), :],
             s_ref[pl.ds(pl.multiple_of(off + i * rs, 8), rs), :]) for i in range(PROJ_PARTS)]
    hs = [_rms(x_ref[r, :], gmix_ref[...]).astype(BF16) for r in parts]
    projs = [_dot(h, win_ref[...]) for h in hs]
    qs = [_dot(_rms(p[:, :o1], gq_ref[...]).astype(BF16), wuq_ref[...]) for p in projs]
    ckvs = [_rms(p[:, o1:o2], gkv_ref[...]) for p in projs]
    krs = [_rope(p[:, o2:], c, s) for p, (c, s) in zip(projs, tabs)]
    for r, ckv, kr in zip(parts, ckvs, krs):
        ckv_ref[r, :] = ckv
        kr_ref[r, :] = kr[:, QK_NOPE:QK_NOPE + QK_ROPE]
    if emit_kv:
        ckv_bs = [ckv.astype(BF16) for ckv in ckvs]
        kns = [_dot(c, wuk_ref[...]) for c in ckv_bs]
        for r, c in zip(parts, ckv_bs):
            vt_ref[:, r] = _dot_t(wuvt_ref[...], c).astype(BF16)
    for i, r in enumerate(parts):
        cq, sq = tabs[i][0] * Q_SCALE, tabs[i][1] * Q_SCALE
        for hh in range(HEADS):
            sl = slice(hh * LANES, (hh + 1) * LANES)
            q_ref[r, sl] = _rope(qs[i][:, sl], cq, sq).astype(BF16)
            if emit_kv:
                k_ref[r, sl] = (kns[i][:, sl] + krs[i]).astype(BF16)


def _proj(x2d, tabs, gmix, win, gq, wuq, gkv, wuk, wuv_t, *, tm, emit_kv):
    t, d = x2d.shape
    q_lora, kv_lora = gq.shape[1], gkv.shape[1]
    ntab = tabs[0].shape[0] // tm
    tab_spec = _const_spec(tabs[0].shape)
    row = lambda w: pl.BlockSpec((tm, w), lambda i: (i, 0))
    specs = [row(HEADS * LANES)]
    shapes = [jax.ShapeDtypeStruct((t, HEADS * LANES), BF16)]
    if emit_kv:
        specs += [row(HEADS * LANES), pl.BlockSpec((HEADS * V_HEAD, tm), lambda i: (0, i))]
        shapes += [jax.ShapeDtypeStruct((t, HEADS * LANES), BF16),
                   jax.ShapeDtypeStruct((HEADS * V_HEAD, t), BF16)]
    specs += [row(kv_lora), row(QK_ROPE)]
    shapes += [jax.ShapeDtypeStruct((t, kv_lora), F32), jax.ShapeDtypeStruct((t, QK_ROPE), F32)]
    weights = (gmix, win, gq, wuq, gkv) + ((wuk, wuv_t) if emit_kv else ())
    return pl.pallas_call(
        functools.partial(_proj_kernel, q_lora=q_lora, kv_lora=kv_lora, emit_kv=emit_kv, ntab=ntab),
        grid=(t // tm,),
        in_specs=[row(d)] + [tab_spec] * len(tabs) + [_const_spec(w.shape) for w in weights],
        out_specs=specs,
        out_shape=shapes,
        compiler_params=_params("arbitrary"),
        name="proj",
    )(x2d, *tabs, *weights)


def _attn_p_kernel(q_ref, k_ref, vt_ref, o_ref, bias_scr, *, tq):
    qi = pl.program_id(2)
    half = tq // 2

    def fold(jobs):
        def scores(hh, carry, ks, nk, q0, q1, masked):
            sl = slice(hh * LANES, (hh + 1) * LANES)
            st = _dot_t(k_ref[pl.ds(ks, nk), sl], q_ref[q0:q1, sl])
            if masked and nk > half:
                st = jnp.concatenate([st[:nk - half], st[nk - half:] + bias_scr[...]], axis=0)
            elif masked:
                st = st + bias_scr[...]
            return st

        def softmax(st, carry):
            m, a = carry
            mn = jnp.maximum(m, jnp.max(st, axis=0, keepdims=True))
            return mn, jnp.exp2(st - mn).astype(BF16), jnp.exp2(m - mn) * a

        def values(hh, ks, nk, mn, p, a):
            vt = jnp.concatenate([vt_ref[hh * V_HEAD:(hh + 1) * V_HEAD, pl.ds(ks, nk)], ones[:, :nk]],
                                 axis=0)
            return mn, a + _dot(vt, p)

        sts = [scores(*job) for job in jobs]
        ps = [softmax(st, job[1]) for st, job in zip(sts, jobs)]
        return [values(job[0], job[2], job[3], *p) for p, job in zip(ps, jobs)]

    @pl.when((pl.program_id(0) == 0) & (pl.program_id(1) == 0) & (qi == 0))
    def _():
        kchunk = lax.broadcasted_iota(jnp.int32, (half, half), 0) >> CHUNK_SHIFT
        qchunk = lax.broadcasted_iota(jnp.int32, (half, half), 1) >> CHUNK_SHIFT
        bias_scr[...] = jnp.where(kchunk <= qchunk, 0.0, NEG_INF)

    ones = jnp.ones((16, tq), BF16)
    init = (jnp.full((1, tq), NEG_INF, F32), jnp.zeros((V_HEAD + 16, tq), F32))

    def full_tile(kt, carry):
        ks = pl.multiple_of(kt * tq, tq)
        return tuple(fold([(hh, carry[hh], ks, tq, 0, tq, False) for hh in range(ATTN_HB)]))

    carry = lax.fori_loop(0, qi, full_tile, (init,) * ATTN_HB)
    diag = pl.multiple_of(qi * tq, tq)
    jobs = [(hh, (carry[hh][0][:, :half], carry[hh][1][:, :half]), diag, half, 0, half, True)
            for hh in range(ATTN_HB)]
    jobs += [(hh, (carry[hh][0][:, half:], carry[hh][1][:, half:]), diag, tq, half, tq, True)
             for hh in range(ATTN_HB)]
    res = fold(jobs)
    carry = tuple((jnp.concatenate([res[hh][0], res[ATTN_HB + hh][0]], axis=1),
                   jnp.concatenate([res[hh][1], res[ATTN_HB + hh][1]], axis=1))
                  for hh in range(ATTN_HB))
    ot = jnp.concatenate([a[:V_HEAD] / a[V_HEAD:V_HEAD + 1] for _, a in carry], axis=0)
    o_ref[...] = ot.T.astype(BF16)


def _attn_prompt(q, k, vt, *, batch, seq, tq):
    assert seq % tq == 0 and tq % (1 << CHUNK_SHIFT) == 0 and HEADS % ATTN_HB == 0
    nq = seq // tq
    return pl.pallas_call(
        functools.partial(_attn_p_kernel, tq=tq),
        grid=(batch, HEADS // ATTN_HB, nq),
        in_specs=[pl.BlockSpec((tq, ATTN_HB * LANES), lambda b, h, i: (b * nq + i, h)),
                  pl.BlockSpec((seq, ATTN_HB * LANES), lambda b, h, i: (b, h)),
                  pl.BlockSpec((ATTN_HB * V_HEAD, seq), lambda b, h, i: (h, b))],
        out_specs=pl.BlockSpec((tq, ATTN_HB * V_HEAD), lambda b, h, i: (b * nq + i, h)),
        out_shape=jax.ShapeDtypeStruct((batch * seq, HEADS * V_HEAD), BF16),
        scratch_shapes=[pltpu.VMEM((tq // 2, tq // 2), F32)],
        compiler_params=_params("arbitrary", "arbitrary", "arbitrary"),
        name="attn_p",
    )(q, k, vt)


def _attn_s_kernel(q_ref, pc_ref, pk_ref, nc_ref, nk_ref, wabs_ref, wuv_ref, o_ref, *,
                   dec, past, kv_lora, nseq):
    q = q_ref[...]
    qh = [_dot(q[:, hh * LANES:(hh + 1) * LANES], wabs_ref[hh]).astype(BF16) for hh in range(HEADS)]
    rows = HEADS * dec
    qpos = past + jnp.concatenate([lax.broadcasted_iota(jnp.int32, (dec, 1), 0)] * HEADS, axis=0)
    qchunk = qpos >> CHUNK_SHIFT
    vis_p = (lax.broadcasted_iota(jnp.int32, (rows, past), 1) >> CHUNK_SHIFT) <= qchunk
    vis_n = ((past + lax.broadcasted_iota(jnp.int32, (rows, dec), 1)) >> CHUNK_SHIFT) <= qchunk
    seqs = range(nseq)
    qf = [jnp.concatenate([h[g * dec:(g + 1) * dec] for h in qh], axis=0) for g in seqs]
    pc = [pc_ref[g * past:(g + 1) * past, :].astype(BF16) for g in seqs]
    nc = [nc_ref[g * dec:(g + 1) * dec, :].astype(BF16) for g in seqs]
    pk = [pk_ref[g * past:(g + 1) * past, :].astype(BF16) for g in seqs]
    nk = [nk_ref[g * dec:(g + 1) * dec, :].astype(BF16) for g in seqs]
    qa = [f[:, :kv_lora] for f in qf]
    qr = [f[:, kv_lora:kv_lora + QK_ROPE] for f in qf]
    s_p = [jnp.where(vis_p, _dot_t(qa[g], pc[g]) + _dot_t(qr[g], pk[g]), NEG_INF) for g in seqs]
    s_n = [jnp.where(vis_n, _dot_t(qa[g], nc[g]) + _dot_t(qr[g], nk[g]), NEG_INF) for g in seqs]
    m = [jnp.maximum(jnp.max(s_p[g], axis=-1, keepdims=True), jnp.max(s_n[g], axis=-1, keepdims=True))
         for g in seqs]
    p_p = [jnp.exp2(s_p[g] - m[g]) for g in seqs]
    p_n = [jnp.exp2(s_n[g] - m[g]) for g in seqs]
    l = [jnp.sum(p_p[g], axis=-1, keepdims=True) + jnp.sum(p_n[g], axis=-1, keepdims=True) for g in seqs]
    olat = [((_dot(p_p[g].astype(BF16), pc[g]) + _dot(p_n[g].astype(BF16), nc[g])) / l[g]).astype(BF16)
            for g in seqs]
    acc = jnp.zeros((nseq * dec, HEADS * V_HEAD), F32)
    for hh in range(HEADS):
        oh = jnp.concatenate([o[hh * dec:(hh + 1) * dec] for o in olat], axis=0)
        acc = acc + _dot(oh, wuv_ref[hh])
    o_ref[...] = acc.astype(BF16)


def _attn_sample(q, past_ckv, past_kr, new_ckv, new_kr, wabs, wuv, *, batch, dec, past):
    kv_lora = past_ckv.shape[-1]
    nseq = ATTN_S_SEQS if batch % ATTN_S_SEQS == 0 else 1
    rows = lambda n, w: pl.BlockSpec((nseq * n, w), lambda b: (b, 0))
    return pl.pallas_call(
        functools.partial(_attn_s_kernel, dec=dec, past=past, kv_lora=kv_lora, nseq=nseq),
        grid=(batch // nseq,),
        in_specs=[rows(dec, HEADS * LANES), rows(past, kv_lora), rows(past, QK_ROPE),
                  rows(dec, kv_lora), rows(dec, QK_ROPE),
                  _const_spec(wabs.shape), _const_spec(wuv.shape)],
        out_specs=rows(dec, HEADS * V_HEAD),
        out_shape=jax.ShapeDtypeStruct((batch * dec, HEADS * V_HEAD), BF16),
        compiler_params=_params("arbitrary"),
        name="attn_s",
    )(q, past_ckv, past_kr, new_ckv, new_kr, wabs, wuv)


def _s5prep_kernel(lr_ref, li_ref, ldt_ref, brt_ref, bit_ref, crt_ref, cit_ref, cr_ref, ci_ref,
                   arow_ref, airow_ref, ldtrow_ref, mt_ref, pp_ref, qt_ref, am_ref):
    hi = lax.Precision.HIGHEST
    lane = lax.broadcasted_iota(jnp.int32, (1, S5_KW), 1)
    pos = (lane >> 4).astype(F32)
    zeros = jnp.zeros((S5_STATE, S5_KW), F32)

    def powers(lrdt, lidt, expo):
        mag, ang = jnp.exp(expo * lrdt), expo * lidt
        return mag * jnp.cos(ang), mag * jnp.sin(ang)

    for gl in range(2):
        lr, li = lr_ref[gl], li_ref[gl]
        dt = jnp.exp(ldt_ref[gl])
        lrdt, lidt = lr * dt, li * dt
        abr, abi = powers(lrdt, lidt, 1.0)
        den = lr * lr + li * li
        ir, ii = lr / den, -li / den
        zr, zi = (abr - 1.0) * ir - abi * ii, (abr - 1.0) * ii + abi * ir
        bbr = zr * brt_ref[gl] - zi * bit_ref[gl]
        bbi = zr * bit_ref[gl] + zi * brt_ref[gl]
        pr, pi = powers(lrdt, lidt, (S5_TC - 1) - pos)
        pt_re, pt_im = pr * bbr - pi * bbi, pr * bbi + pi * bbr
        krev = _dot(cr_ref[gl], pt_re, hi) - _dot(ci_ref[gl], pt_im, hi)
        for t in range(S5_TC):
            shift = (S5_TC - 1 - t) * S5_CH
            rolled = krev if shift == 0 else pltpu.roll(krev, S5_KW - shift, 1)
            mt_ref[gl, t * S5_CH:(t + 1) * S5_CH, :] = jnp.where(
                lane < S5_KW - shift, rolled, 0.0).astype(BF16)
        qr, qi = powers(lrdt, lidt, pos + 1.0)
        q_re = crt_ref[gl] * qr - cit_ref[gl] * qi
        q_im = -(crt_ref[gl] * qi + cit_ref[gl] * qr)
        own = lambda a, b: [a, zeros, b, zeros] if gl == 0 else [zeros, a, zeros, b]
        pp_ref[gl] = jnp.concatenate(own(pt_re, pt_im), axis=0).T.astype(BF16)
        qt_ref[gl] = jnp.concatenate(own(q_re, q_im), axis=0).T.astype(BF16)
    dtrow = jnp.exp(ldtrow_ref[...])
    ar, ai = powers(arow_ref[...] * dtrow, airow_ref[...] * dtrow, float(S5_TC))
    am_ref[...] = jnp.concatenate([ar, ai, jnp.zeros((6, 2 * S5_STATE), F32)], axis=0)


def _s5_operators(a_re, a_im, log_dt, b_re, b_im, c_re, c_im, d_skip):
    g, p = a_re.shape
    npair = g // 2
    col = lambda a: a.reshape(g, p, 1)
    tile_s = lambda a: jnp.tile(a, (1, 1, S5_TC))
    pair = lambda a: a.reshape(npair, 1, 2 * p)
    ldt_col = log_dt.reshape(g, 1, 1)
    ldt_row = jnp.repeat(log_dt, p).reshape(npair, 1, 2 * p)
    args = (col(a_re), col(a_im), ldt_col, tile_s(b_re), tile_s(b_im),
            tile_s(c_re.transpose(0, 2, 1)), tile_s(c_im.transpose(0, 2, 1)), c_re, c_im,
            pair(a_re), pair(a_im), ldt_row)
    spec = lambda a: pl.BlockSpec((2,) + a.shape[1:], lambda i: (i,) + (0,) * (a.ndim - 1))
    pspec = lambda a: pl.BlockSpec((None,) + a.shape[1:], lambda i: (i,) + (0,) * (a.ndim - 1))
    opspec = pl.BlockSpec((2, S5_KW, S5_KW), lambda i: (i, 0, 0))
    mt, pp, qt, am = pl.pallas_call(
        _s5prep_kernel,
        grid=(npair,),
        in_specs=[spec(a) for a in args[:9]] + [pspec(a) for a in args[9:]],
        out_specs=[opspec, opspec, opspec, pl.BlockSpec((None, 8, 2 * p), lambda i: (i, 0, 0))],
        out_shape=[jax.ShapeDtypeStruct((g, S5_KW, S5_KW), BF16)] * 3
        + [jax.ShapeDtypeStruct((npair, 8, 2 * p), F32)],
        compiler_params=_params("arbitrary"),
        name="s5prep",
    )(*args)
    dcol = jnp.tile(d_skip.reshape(g, 1, S5_CH), (1, S5_TC, 1)).reshape(g, S5_KW, 1)
    return mt, pp, qt, am, dcol


def _s5in_kernel(x_ref, gmix_ref, perm_ref, wu_ref, o_ref, hp_scr, *, nchunk, groups):
    blk = S5_TC * S5_TC
    nblk = nchunk // S5_TC
    hs = [_rms(x_ref[jb * blk:(jb + 1) * blk, :], gmix_ref[...]).astype(BF16) for jb in range(nblk)]
    hbs = [_dot(perm_ref[...], h).astype(BF16) for h in hs]
    for jb, hb in enumerate(hbs):
        for s in range(S5_TC):
            hp_scr[s * nchunk + jb * S5_TC:s * nchunk + (jb + 1) * S5_TC, :] = hb[s * S5_TC:(s + 1) * S5_TC, :]
    for sb in range(S5_TC // S5_SB):
        rows = slice(sb * S5_SB * nchunk, (sb + 1) * S5_SB * nchunk)
        ut = _dot_t(wu_ref[...], hp_scr[rows, :])
        for k in range(S5_SB):
            for g in range(groups):
                o_ref[g, sb * S5_SB + k] = (
                    ut[g * S5_CH:(g + 1) * S5_CH, k * nchunk:(k + 1) * nchunk].astype(BF16))


def _chunk_block_perm():
    idx = jnp.arange(S5_TC * S5_TC)
    return (idx[None, :] == (idx[:, None] % S5_TC) * S5_TC + idx[:, None] // S5_TC).astype(BF16)


def _s5in(x2d, gmix, wu_t, *, nseq, seq):
    d = x2d.shape[1]
    groups = wu_t.shape[0] // S5_CH
    nchunk = seq // S5_TC
    assert nchunk % S5_TC == 0
    perm = _chunk_block_perm()
    return pl.pallas_call(
        functools.partial(_s5in_kernel, nchunk=nchunk, groups=groups),
        grid=(nseq,),
        in_specs=[pl.BlockSpec((seq, d), lambda b: (b, 0)),
                  _const_spec(gmix.shape), _const_spec(perm.shape), _const_spec(wu_t.shape)],
        out_specs=pl.BlockSpec((groups, S5_TC, S5_CH, nchunk), lambda b: (0, 0, 0, b)),
        out_shape=jax.ShapeDtypeStruct((groups, S5_TC, S5_CH, nseq * nchunk), BF16),
        scratch_shapes=[pltpu.VMEM((seq, d), BF16)],
        compiler_params=_params("arbitrary"),
        name="s5in",
    )(x2d, gmix, perm, wu_t)


def _s5t_kernel(x_ref, mt_ref, p_ref, qt_ref, a_ref, d_ref, x0re_ref, x0im_ref,
                y_ref, sre_ref, sim_ref, ere_scr, eim_scr, xre_scr, xim_scr, *, nb, nchunk):
    tdims = (((0,), (0,)), ((), ()))
    e = (lax.dot_general(x_ref[0], p_ref[0], tdims, preferred_element_type=F32)
         + lax.dot_general(x_ref[1], p_ref[1], tdims, preferred_element_type=F32))
    ere_scr[...] = e[:, :LANES]
    eim_scr[...] = e[:, LANES:]
    local = [_dot(mt_ref[g], x_ref[g]) + x_ref[g].astype(F32) * d_ref[g] for g in range(2)]
    ar, ai = a_ref[0:1, :], a_ref[1:2, :]
    re, im = x0re_ref[...], x0im_ref[...]
    for j in range(nchunk):
        rows = pl.ds(j, nb, stride=nchunk)
        xre_scr[rows, :] = re
        xim_scr[rows, :] = im
        re, im = ar * re - ai * im + ere_scr[rows, :], ar * im + ai * re + eim_scr[rows, :]
    sre_ref[...] = re
    sim_ref[...] = im
    xs = jnp.concatenate([xre_scr[...], xim_scr[...]], axis=1).astype(BF16)
    for g in range(2):
        y_ref[g] = (local[g] + _dot_t(qt_ref[g], xs)).astype(BF16)


def _s5t(u2t, mt, pp, qt, am, dcol, x0re, x0im, *, nb, nchunk):
    g, kw, total = u2t.shape
    cols = nb * nchunk
    opspec = pl.BlockSpec((2, kw, kw), lambda p, r: (p, 0, 0))
    stspec = pl.BlockSpec((nb, LANES), lambda p, r: (r, p))
    return pl.pallas_call(
        functools.partial(_s5t_kernel, nb=nb, nchunk=nchunk),
        grid=(g // 2, total // cols),
        in_specs=[pl.BlockSpec((2, kw, cols), lambda p, r: (p, 0, r)),
                  opspec, opspec, opspec,
                  pl.BlockSpec((None, 8, LANES), lambda p, r: (p, 0, 0)),
                  pl.BlockSpec((2, kw, 1), lambda p, r: (p, 0, 0)),
                  stspec, stspec],
        out_specs=[pl.BlockSpec((2, kw, cols), lambda p, r: (p, 0, r)), stspec, stspec],
        out_shape=[jax.ShapeDtypeStruct(u2t.shape, BF16),
                   jax.ShapeDtypeStruct(x0re.shape, F32),
                   jax.ShapeDtypeStruct(x0im.shape, F32)],
        scratch_shapes=[pltpu.VMEM((cols, LANES), F32)] * 4,
        compiler_params=_params("arbitrary", "arbitrary"),
        name="s5t",
    )(u2t, mt, pp, qt, am, dcol, x0re, x0im)


def _gelu(x):
    c = math.sqrt(2.0 / math.pi)
    hx = 0.5 * x
    return hx + hx * jnp.tanh(x * (c + (c * 0.044715) * (x * x)))


def _sigmoid(x):
    return 1.0 / (1.0 + jnp.exp(-x))


def _s5out_kernel(y_ref, wglut_ref, b_ref, g_ref, perm_ref, o_ref, z_scr, t_scr, *, nchunk, groups):
    for tb in range(S5_TC // S5_SB):
        for k in range(S5_SB):
            for g in range(groups):
                z_scr[g * S5_CH:(g + 1) * S5_CH, k * nchunk:(k + 1) * nchunk] = (
                    y_ref[g, tb * S5_SB + k].astype(F32))
        z = _gelu(z_scr[...])
        ssm = z * _sigmoid(_dot(wglut_ref[...], z.astype(BF16)) + b_ref[...])
        sn = ssm * lax.rsqrt(jnp.mean(ssm * ssm, axis=0, keepdims=True) + EPS) * g_ref[...]
        for k in range(S5_SB):
            t = tb * S5_SB + k
            t_scr[t * nchunk:(t + 1) * nchunk, :] = sn[:, k * nchunk:(k + 1) * nchunk].T.astype(BF16)
    blk = S5_TC * S5_TC
    for jb in range(nchunk // S5_TC):
        rows = jnp.concatenate([t_scr[t * nchunk + jb * S5_TC:t * nchunk + (jb + 1) * S5_TC, :]
                                for t in range(S5_TC)], axis=0)
        o_ref[jb * blk:(jb + 1) * blk, :] = _dot(perm_ref[...], rows).astype(BF16)


def _s5out(y2t, wglu_t, bcol, gcol, *, nseq, seq):
    groups = y2t.shape[0]
    width = groups * S5_CH
    nchunk = seq // S5_TC
    perm = _chunk_block_perm()
    return pl.pallas_call(
        functools.partial(_s5out_kernel, nchunk=nchunk, groups=groups),
        grid=(nseq,),
        in_specs=[pl.BlockSpec((groups, S5_TC, S5_CH, nchunk), lambda b: (0, 0, 0, b)),
                  _const_spec(wglu_t.shape), _const_spec(bcol.shape), _const_spec(gcol.shape),
                  _const_spec(perm.shape)],
        out_specs=pl.BlockSpec((seq, width), lambda b: (b, 0)),
        out_shape=jax.ShapeDtypeStruct((nseq * seq, width), BF16),
        scratch_shapes=[pltpu.VMEM((width, S5_SB * nchunk), F32), pltpu.VMEM((seq, width), BF16)],
        compiler_params=_params("arbitrary"),
        name="s5out",
    )(y2t, wglu_t, bcol, gcol, perm)


def _s5_branch(x2d, gmix, wu_t, ops, wglu_t, bcol, gcol, x0_re, x0_im, *, batch, seq):
    groups = wu_t.shape[0] // S5_CH
    nchunk = seq // S5_TC
    nseq, run = (batch, seq) if nchunk % S5_TC == 0 else (1, batch * seq)
    u2t = _s5in(x2d, gmix, wu_t, nseq=nseq, seq=run)
    nb = 16 if (batch % 16 == 0 and nchunk % LANES == 0) else batch
    y2t, s_re, s_im = _s5t(u2t.reshape(groups, S5_KW, -1), *ops, x0_re, x0_im, nb=nb, nchunk=nchunk)
    s5n = _s5out(y2t.reshape(u2t.shape), wglu_t, bcol, gcol, nseq=nseq, seq=run)
    return s5n, s_re, s_im


def _ffn_kernel(x_ref, a_ref, s5_ref, gmla_ref, wout_ref, gffn_ref, wgu_ref, wdown_ref, gfin_ref, o_ref):
    dff = wdown_ref.shape[0]
    tm = x_ref.shape[0]
    parts = [slice(i * (tm // FFN_PARTS), (i + 1) * (tm // FFN_PARTS)) for i in range(FFN_PARTS)]
    mixed = [jnp.concatenate([_rms(a_ref[r, :].astype(F32), gmla_ref[...]).astype(BF16), s5_ref[r, :]],
                             axis=-1) for r in parts]
    x1 = [x_ref[r, :] + _dot(m, wout_ref[...]) for r, m in zip(parts, mixed)]
    h2 = [_rms(x, gffn_ref[...]).astype(BF16) for x in x1]
    gu = [_dot(h, wgu_ref[...]) for h in h2]
    act = [(g[:, :dff] * _sigmoid(g[:, :dff]) * g[:, dff:]).astype(BF16) for g in gu]
    x2 = [x + _dot(a, wdown_ref[...]) for x, a in zip(x1, act)]
    for r, x in zip(parts, x2):
        o_ref[r, :] = _rms(x, gfin_ref[...])


def _ffn(x2d, attn, s5n, gmla, wout, gffn, wgu, wdown, gfin, *, tm):
    t, d = x2d.shape
    row = lambda w: pl.BlockSpec((tm, w), lambda i: (i, 0))
    resident = lambda a: pl.BlockSpec(a.shape, lambda i: (0,) * a.ndim, pipeline_mode=pl.Buffered(1))
    consts = (gmla, wout, gffn, wgu, wdown, gfin)
    return pl.pallas_call(
        _ffn_kernel,
        grid=(t // tm,),
        in_specs=[row(d), row(attn.shape[1]), row(s5n.shape[1])] + [resident(a) for a in consts],
        out_specs=row(d),
        out_shape=jax.ShapeDtypeStruct((t, d), F32),
        compiler_params=_params("arbitrary"),
        name="ffn",
    )(x2d, attn, s5n, *consts)


def _rope_tabs(pos, tm):
    inv = 1.0 / (ROPE_BASE ** (jnp.arange(0, QK_ROPE, 2, dtype=F32) / QK_ROPE))
    ang = pos.astype(F32)[:, None] * inv[None, :]
    cos, sin = jnp.cos(ang), jnp.sin(ang)
    n = pos.shape[0]
    z = lambda w: jnp.zeros((n, w), F32)
    c = jnp.concatenate([jnp.ones((n, QK_NOPE), F32), cos, cos, z(LANES - QK_NOPE - QK_ROPE)], axis=1)
    s = jnp.concatenate([z(QK_NOPE), sin, sin, z(LANES - QK_NOPE - QK_ROPE)], axis=1)
    reps = max(1, tm // n)
    return tuple(jnp.tile(t, (reps, 1)) for t in (c, s))


def _pad_heads(w, width):
    r = w.shape[0]
    return jnp.pad(w.reshape(r, HEADS, width), ((0, 0), (0, 0), (0, LANES - width))).reshape(r, HEADS * LANES)


def _layer_weights(w_in, w_uq, w_ukv, q_lora, kv_lora):
    o2, o3 = q_lora + kv_lora, q_lora + kv_lora + QK_ROPE
    d = w_in.shape[0]
    half = QK_ROPE // 2
    swapped = lambda w: jnp.concatenate([-w[..., half:], w[..., :half]], axis=-1)
    w_kr = w_in[:, o2:o3]
    win = jnp.concatenate([w_in[:, :o2], jnp.zeros((d, QK_NOPE), F32), w_kr, swapped(w_kr)],
                          axis=1).astype(BF16)
    wu_t = w_in[:, o3:].T.astype(BF16)
    uq = w_uq.reshape(q_lora, HEADS, QK_NOPE + QK_ROPE)
    wuq = jnp.concatenate([uq, swapped(uq[..., QK_NOPE:])], axis=-1).reshape(q_lora, HEADS * LANES)
    wuq = wuq.astype(BF16)
    ukv = w_ukv.reshape(kv_lora, HEADS, QK_NOPE + V_HEAD)
    w_uk, w_uv = ukv[..., :QK_NOPE], ukv[..., QK_NOPE:]
    wuk = _pad_heads(w_uk.reshape(kv_lora, HEADS * QK_NOPE), QK_NOPE).astype(BF16)
    wuv_t = w_uv.reshape(kv_lora, HEADS * V_HEAD).T.astype(BF16)
    wabs = jnp.zeros((HEADS, LANES, kv_lora + LANES), F32)
    wabs = wabs.at[:, :QK_NOPE, :kv_lora].set(w_uk.transpose(1, 2, 0))
    wabs = wabs.at[:, QK_NOPE:QK_NOPE + QK_ROPE, kv_lora:kv_lora + QK_ROPE].set(
        jnp.broadcast_to(jnp.eye(QK_ROPE, dtype=F32), (HEADS, QK_ROPE, QK_ROPE)))
    wuv = jnp.zeros((HEADS, kv_lora, HEADS, V_HEAD), F32)
    wuv = wuv.at[jnp.arange(HEADS), :, jnp.arange(HEADS), :].set(w_uv.transpose(1, 0, 2))
    wuv = wuv.reshape(HEADS, kv_lora, HEADS * V_HEAD)
    return win, wu_t, wuq, wuk, wuv_t, wabs.astype(BF16), wuv.astype(BF16)


def _token_tile(batch, seq, cap):
    for tm in (1024, 512, 256, 128, 64, 32, 16):
        if tm <= cap and (batch * seq) % tm == 0 and (seq % tm == 0 or tm % seq == 0):
            return tm
    raise ValueError(f"no token tile for batch={batch} seq={seq}")


def kernel(x_prompt, x_sample, cache_mla_ckv, cache_mla_krope, state_s5_re, state_s5_im,
           g_mix, w_in, g_q, w_uq, g_kv, w_ukv,
           s5_a_re, s5_a_im, s5_log_dt, s5_b_re, s5_b_im, s5_c_re, s5_c_im, s5_d, w_glu, b_glu,
           g_out_mla, g_out_s5, w_out, g_ffn, w_gate, w_up, w_down, g_final):
    bp, lp, d = x_prompt.shape
    bs, ls, _ = x_sample.shape
    past = cache_mla_ckv.shape[2]
    q_lora, kv_lora = g_q.shape[1], g_kv.shape[1]
    groups, state = s5_a_re.shape[1], s5_a_re.shape[2]
    assert g_mix.shape[0] == 1, "single-layer model"
    assert state == S5_STATE and s5_b_re.shape[-1] == S5_CH and groups % 2 == 0
    assert lp % S5_TC == 0 and ls % S5_TC == 0 and bp % 8 == 0 and bs % 8 == 0

    xp = x_prompt.reshape(bp * lp, d)
    xs = x_sample.reshape(bs * ls, d)
    tmp, tms = _token_tile(bp, lp, FFN_TM), _token_tile(bs, ls, FFN_TM)
    tpp, tps = _token_tile(bp, lp, PROJ_TM), _token_tile(bs, ls, PROJ_TM)
    tabs_p = _rope_tabs(jnp.arange(lp), tpp)
    tabs_s = _rope_tabs(past + jnp.arange(ls), tps)
    row2 = lambda a: a.reshape(1, -1)

    win, wu_t, wuq, wuk, wuv_t, wabs, wuv = _layer_weights(w_in[0], w_uq[0], w_ukv[0], q_lora, kv_lora)
    ops = _s5_operators(s5_a_re[0], s5_a_im[0], s5_log_dt[0], s5_b_re[0], s5_b_im[0],
                        s5_c_re[0], s5_c_im[0], s5_d[0])
    s5_w = (row2(g_mix[0]), wu_t, ops, w_glu[0].T.astype(BF16), b_glu[0].reshape(-1, 1),
            g_out_s5[0].reshape(-1, 1))
    ffn_w = (row2(g_out_mla[0]), w_out[0].astype(BF16), row2(g_ffn[0]),
             jnp.concatenate([w_gate[0], w_up[0]], axis=1).astype(BF16), w_down[0].astype(BF16),
             row2(g_final))
    proj_w = (row2(g_mix[0]), win, row2(g_q[0]), wuq, row2(g_kv[0]), wuk, wuv_t)

    q, k, vt, p_ckv, p_kr = _proj(xp, tabs_p, *proj_w, tm=tpp, emit_kv=True)
    attn = _attn_prompt(q, k, vt, batch=bp, seq=lp, tq=min(lp, ATTN_TQ))
    zeros = jnp.zeros((bp, groups * state), F32)
    s5n, p_re, p_im = _s5_branch(xp, *s5_w, zeros, zeros, batch=bp, seq=lp)
    yp = _ffn(xp, attn, s5n, *ffn_w, tm=tmp)

    q, s_ckv, s_kr = _proj(xs, tabs_s, *proj_w, tm=tps, emit_kv=False)
    attn = _attn_sample(q, cache_mla_ckv[0].reshape(bs * past, kv_lora),
                        cache_mla_krope[0].reshape(bs * past, QK_ROPE), s_ckv, s_kr, wabs, wuv,
                        batch=bs, dec=ls, past=past)
    s5n, s_re, s_im = _s5_branch(xs, *s5_w, state_s5_re[0].reshape(bs, groups * state),
                                 state_s5_im[0].reshape(bs, groups * state), batch=bs, seq=ls)
    ys = _ffn(xs, attn, s5n, *ffn_w, tm=tms)

    return (yp.reshape(bp, lp, d), ys.reshape(bs, ls, d),
            p_ckv.reshape(1, bp, lp, kv_lora), p_kr.reshape(1, bp, lp, QK_ROPE),
            p_re.reshape(1, bp, groups, state), p_im.reshape(1, bp, groups, state),
            s_ckv.reshape(1, bs, ls, kv_lora), s_kr.reshape(1, bs, ls, QK_ROPE),
            s_re.reshape(1, bs, groups, state), s_im.reshape(1, bs, groups, state))
```

```python
import functools
import math

import jax
import jax.numpy as jnp
from jax import lax
from jax.experimental import pallas as pl
from jax.experimental.pallas import tpu as pltpu

F32 = jnp.float32
BF16 = jnp.bfloat16

EPS = 1e-6
NEG_INF = -1e30
CHUNK_SHIFT = 6
ATTN_TQ = 1024
ATTN_HB = 4
ATTN_S_SEQS = 4
PROJ_TM = 1024
PROJ_PARTS = 2
FFN_TM = 1024
FFN_PARTS = 4
HEADS = 8
QK_NOPE = 64
QK_ROPE = 32
V_HEAD = 64
ROPE_BASE = 10000.0
Q_SCALE = (QK_NOPE + QK_ROPE) ** -0.5 * math.log2(math.e)
S5_CH = 16
S5_STATE = 64
S5_TC = 16
S5_KW = S5_TC * S5_CH
S5_SB = 4
S5T_SEQS = 32
LANES = 128
VMEM_LIMIT = 56 * 1024 * 1024


def _rms(x, g):
    return x * lax.rsqrt(jnp.mean(x * x, axis=-1, keepdims=True) + EPS) * g


def _dot(a, b, precision=None):
    return jnp.dot(a, b, preferred_element_type=F32, precision=precision)


def _dot_t(a, b):
    return lax.dot_general(a, b, (((1,), (1,)), ((), ())), preferred_element_type=F32)


def _const_spec(shape):
    return pl.BlockSpec(shape, lambda *_: (0,) * len(shape))


def _params(*semantics, **kw):
    return pltpu.CompilerParams(dimension_semantics=semantics, vmem_limit_bytes=VMEM_LIMIT, **kw)


def _rope(t, c, s):
    return t * c + pltpu.roll(t, LANES - QK_ROPE, 1) * s


def _proj_kernel(x_ref, c_ref, s_ref, gmix_ref, win_ref, gq_ref, wuq_ref, gkv_ref, *rest,
                 q_lora, kv_lora, emit_kv, ntab):
    rest = list(rest)
    wuk_ref, wuvt_ref = (rest.pop(0), rest.pop(0)) if emit_kv else (None, None)
    q_ref = rest.pop(0)
    k_ref, vt_ref = (rest.pop(0), rest.pop(0)) if emit_kv else (None, None)
    ckv_ref, kr_ref = rest
    o1 = q_lora
    o2 = o1 + kv_lora
    tm = x_ref.shape[0]
    rs = tm // PROJ_PARTS
    parts = [slice(i * rs, (i + 1) * rs) for i in range(PROJ_PARTS)]
    off = (pl.program_id(0) % ntab) * tm
    tabs = [(c_ref[pl.ds(pl.multiple_of(off + i * rs, 8), rs), :],
             s_ref[pl.ds(pl.multiple_of(off + i * rs, 8), rs), :]) for i in range(PROJ_PARTS)]
    hs = [_rms(x_ref[r, :], gmix_ref[...]).astype(BF16) for r in parts]
    projs = [_dot(h, win_ref[...]) for h in hs]
    qs = [_dot(_rms(p[:, :o1], gq_ref[...]).astype(BF16), wuq_ref[...]) for p in projs]
    ckvs = [_rms(p[:, o1:o2], gkv_ref[...]) for p in projs]
    krs = [_rope(p[:, o2:], c, s) for p, (c, s) in zip(projs, tabs)]
    for r, ckv, kr in zip(parts, ckvs, krs):
        ckv_ref[r, :] = ckv
        kr_ref[r, :] = kr[:, QK_NOPE:QK_NOPE + QK_ROPE]
    if emit_kv:
        ckv_bs = [ckv.astype(BF16) for ckv in ckvs]
        kns = [_dot(c, wuk_ref[...]) for c in ckv_bs]
        for r, c in zip(parts, ckv_bs):
            vt_ref[:, r] = _dot_t(wuvt_ref[...], c).astype(BF16)
    for i, r in enumerate(parts):
        cq, sq = tabs[i][0] * Q_SCALE, tabs[i][1] * Q_SCALE
        for hh in range(HEADS):
            sl = slice(hh * LANES, (hh + 1) * LANES)
            q_ref[r, sl] = _rope(qs[i][:, sl], cq, sq).astype(BF16)
            if emit_kv:
                k_ref[r, sl] = (kns[i][:, sl] + krs[i]).astype(BF16)


def _proj(x2d, tabs, gmix, win, gq, wuq, gkv, wuk, wuv_t, *, tm, emit_kv):
    t, d = x2d.shape
    q_lora, kv_lora = gq.shape[1], gkv.shape[1]
    ntab = tabs[0].shape[0] // tm
    tab_spec = _const_spec(tabs[0].shape)
    row = lambda w: pl.BlockSpec((tm, w), lambda i: (i, 0))
    specs = [row(HEADS * LANES)]
    shapes = [jax.ShapeDtypeStruct((t, HEADS * LANES), BF16)]
    if emit_kv:
        specs += [row(HEADS * LANES), pl.BlockSpec((HEADS * V_HEAD, tm), lambda i: (0, i))]
        shapes += [jax.ShapeDtypeStruct((t, HEADS * LANES), BF16),
                   jax.ShapeDtypeStruct((HEADS * V_HEAD, t), BF16)]
    specs += [row(kv_lora), row(QK_ROPE)]
    shapes += [jax.ShapeDtypeStruct((t, kv_lora), F32), jax.ShapeDtypeStruct((t, QK_ROPE), F32)]
    weights = (gmix, win, gq, wuq, gkv) + ((wuk, wuv_t) if emit_kv else ())
    return pl.pallas_call(
        functools.partial(_proj_kernel, q_lora=q_lora, kv_lora=kv_lora, emit_kv=emit_kv, ntab=ntab),
        grid=(t // tm,),
        in_specs=[row(d)] + [tab_spec] * len(tabs) + [_const_spec(w.shape) for w in weights],
        out_specs=specs,
        out_shape=shapes,
        compiler_params=_params("arbitrary"),
        name="proj",
    )(x2d, *tabs, *weights)


def _attn_p_kernel(q_ref, k_ref, vt_ref, o_ref, bias_scr, *, tq):
    qi = pl.program_id(2)
    half = tq // 2

    def fold(jobs):
        def scores(hh, carry, ks, nk, q0, q1, masked):
            sl = slice(hh * LANES, (hh + 1) * LANES)
            st = _dot_t(k_ref[pl.ds(ks, nk), sl], q_ref[q0:q1, sl])
            if masked and nk > half:
                st = jnp.concatenate([st[:nk - half], st[nk - half:] + bias_scr[...]], axis=0)
            elif masked:
                st = st + bias_scr[...]
            return st

        def softmax(st, carry):
            m, a = carry
            mn = jnp.maximum(m, jnp.max(st, axis=0, keepdims=True))
            return mn, jnp.exp2(st - mn).astype(BF16), jnp.exp2(m - mn) * a

        def values(hh, ks, nk, mn, p, a):
            vt = jnp.concatenate([vt_ref[hh * V_HEAD:(hh + 1) * V_HEAD, pl.ds(ks, nk)], ones[:, :nk]],
                                 axis=0)
            return mn, a + _dot(vt, p)

        sts = [scores(*job) for job in jobs]
        ps = [softmax(st, job[1]) for st, job in zip(sts, jobs)]
        return [values(job[0], job[2], job[3], *p) for p, job in zip(ps, jobs)]

    @pl.when((pl.program_id(0) == 0) & (pl.program_id(1) == 0) & (qi == 0))
    def _():
        kchunk = lax.broadcasted_iota(jnp.int32, (half, half), 0) >> CHUNK_SHIFT
        qchunk = lax.broadcasted_iota(jnp.int32, (half, half), 1) >> CHUNK_SHIFT
        bias_scr[...] = jnp.where(kchunk <= qchunk, 0.0, NEG_INF)

    ones = jnp.ones((16, tq), BF16)
    init = (jnp.full((1, tq), NEG_INF, F32), jnp.zeros((V_HEAD + 16, tq), F32))

    def full_tile(kt, carry):
        ks = pl.multiple_of(kt * tq, tq)
        return tuple(fold([(hh, carry[hh], ks, tq, 0, tq, False) for hh in range(ATTN_HB)]))

    carry = lax.fori_loop(0, qi, full_tile, (init,) * ATTN_HB)
    diag = pl.multiple_of(qi * tq, tq)
    jobs = [(hh, (carry[hh][0][:, :half], carry[hh][1][:, :half]), diag, half, 0, half, True)
            for hh in range(ATTN_HB)]
    jobs += [(hh, (carry[hh][0][:, half:], carry[hh][1][:, half:]), diag, tq, half, tq, True)
             for hh in range(ATTN_HB)]
    res = fold(jobs)
    carry = tuple((jnp.concatenate([res[hh][0], res[ATTN_HB + hh][0]], axis=1),
                   jnp.concatenate([res[hh][1], res[ATTN_HB + hh][1]], axis=1))
                  for hh in range(ATTN_HB))
    ot = jnp.concatenate([a[:V_HEAD] / a[V_HEAD:V_HEAD + 1] for _, a in carry], axis=0)
    o_ref[...] = ot.T.astype(BF16)


def _attn_prompt(q, k, vt, *, batch, seq, tq):
    assert seq % tq == 0 and tq % (1 << CHUNK_SHIFT) == 0 and HEADS % ATTN_HB == 0
    nq = seq // tq
    return pl.pallas_call(
        functools.partial(_attn_p_kernel, tq=tq),
        grid=(batch, HEADS // ATTN_HB, nq),
        in_specs=[pl.BlockSpec((tq, ATTN_HB * LANES), lambda b, h, i: (b * nq + i, h)),
                  pl.BlockSpec((seq, ATTN_HB * LANES), lambda b, h, i: (b, h)),
                  pl.BlockSpec((ATTN_HB * V_HEAD, seq), lambda b, h, i: (h, b))],
        out_specs=pl.BlockSpec((tq, ATTN_HB * V_HEAD), lambda b, h, i: (b * nq + i, h)),
        out_shape=jax.ShapeDtypeStruct((batch * seq, HEADS * V_HEAD), BF16),
        scratch_shapes=[pltpu.VMEM((tq // 2, tq // 2), F32)],
        compiler_params=_params("arbitrary", "arbitrary", "arbitrary"),
        name="attn_p",
    )(q, k, vt)


def _attn_s_kernel(q_ref, pc_ref, pk_ref, nc_ref, nk_ref, wabs_ref, wuv_ref, o_ref, *,
                   dec, past, kv_lora, nseq):
    q = q_ref[...]
    qh = [_dot(q[:, hh * LANES:(hh + 1) * LANES], wabs_ref[hh]).astype(BF16) for hh in range(HEADS)]
    rows = HEADS * dec
    qpos = past + jnp.concatenate([lax.broadcasted_iota(jnp.int32, (dec, 1), 0)] * HEADS, axis=0)
    qchunk = qpos >> CHUNK_SHIFT
    vis_p = (lax.broadcasted_iota(jnp.int32, (rows, past), 1) >> CHUNK_SHIFT) <= qchunk
    vis_n = ((past + lax.broadcasted_iota(jnp.int32, (rows, dec), 1)) >> CHUNK_SHIFT) <= qchunk
    seqs = range(nseq)
    qf = [jnp.concatenate([h[g * dec:(g + 1) * dec] for h in qh], axis=0) for g in seqs]
    pc = [pc_ref[g * past:(g + 1) * past, :].astype(BF16) for g in seqs]
    nc = [nc_ref[g * dec:(g + 1) * dec, :].astype(BF16) for g in seqs]
    pk = [pk_ref[g * past:(g + 1) * past, :].astype(BF16) for g in seqs]
    nk = [nk_ref[g * dec:(g + 1) * dec, :].astype(BF16) for g in seqs]
    qa = [f[:, :kv_lora] for f in qf]
    qr = [f[:, kv_lora:kv_lora + QK_ROPE] for f in qf]
    s_p = [jnp.where(vis_p, _dot_t(qa[g], pc[g]) + _dot_t(qr[g], pk[g]), NEG_INF) for g in seqs]
    s_n = [jnp.where(vis_n, _dot_t(qa[g], nc[g]) + _dot_t(qr[g], nk[g]), NEG_INF) for g in seqs]
    m = [jnp.maximum(jnp.max(s_p[g], axis=-1, keepdims=True), jnp.max(s_n[g], axis=-1, keepdims=True))
         for g in seqs]
    p_p = [jnp.exp2(s_p[g] - m[g]) for g in seqs]
    p_n = [jnp.exp2(s_n[g] - m[g]) for g in seqs]
    l = [jnp.sum(p_p[g], axis=-1, keepdims=True) + jnp.sum(p_n[g], axis=-1, keepdims=True) for g in seqs]
    olat = [((_dot(p_p[g].astype(BF16), pc[g]) + _dot(p_n[g].astype(BF16), nc[g])) / l[g]).astype(BF16)
            for g in seqs]
    acc = jnp.zeros((nseq * dec, HEADS * V_HEAD), F32)
    for hh in range(HEADS):
        oh = jnp.concatenate([o[hh * dec:(hh + 1) * dec] for o in olat], axis=0)
        acc = acc + _dot(oh, wuv_ref[hh])
    o_ref[...] = acc.astype(BF16)


def _attn_sample(q, past_ckv, past_kr, new_ckv, new_kr, wabs, wuv, *, batch, dec, past):
    kv_lora = past_ckv.shape[-1]
    nseq = ATTN_S_SEQS if batch % ATTN_S_SEQS == 0 else 1
    rows = lambda n, w: pl.BlockSpec((nseq * n, w), lambda b: (b, 0))
    return pl.pallas_call(
        functools.partial(_attn_s_kernel, dec=dec, past=past, kv_lora=kv_lora, nseq=nseq),
        grid=(batch // nseq,),
        in_specs=[rows(dec, HEADS * LANES), rows(past, kv_lora), rows(past, QK_ROPE),
                  rows(dec, kv_lora), rows(dec, QK_ROPE),
                  _const_spec(wabs.shape), _const_spec(wuv.shape)],
        out_specs=rows(dec, HEADS * V_HEAD),
        out_shape=jax.ShapeDtypeStruct((batch * dec, HEADS * V_HEAD), BF16),
        compiler_params=_params("arbitrary"),
        name="attn_s",
    )(q, past_ckv, past_kr, new_ckv, new_kr, wabs, wuv)


def _s5prep_kernel(lr_ref, li_ref, ldt_ref, brt_ref, bit_ref, crt_ref, cit_ref, cr_ref, ci_ref,
                   arow_ref, airow_ref, ldtrow_ref, mt_ref, pp_ref, qt_ref, am_ref):
    hi = lax.Precision.HIGHEST
    lane = lax.broadcasted_iota(jnp.int32, (1, S5_KW), 1)
    pos = (lane >> 4).astype(F32)
    zeros = jnp.zeros((S5_STATE, S5_KW), F32)

    def powers(lrdt, lidt, expo):
        mag, ang = jnp.exp(expo * lrdt), expo * lidt
        return mag * jnp.cos(ang), mag * jnp.sin(ang)

    for gl in range(2):
        lr, li = lr_ref[gl], li_ref[gl]
        dt = jnp.exp(ldt_ref[gl])
        lrdt, lidt = lr * dt, li * dt
        abr, abi = powers(lrdt, lidt, 1.0)
        den = lr * lr + li * li
        ir, ii = lr / den, -li / den
        zr, zi = (abr - 1.0) * ir - abi * ii, (abr - 1.0) * ii + abi * ir
        bbr = zr * brt_ref[gl] - zi * bit_ref[gl]
        bbi = zr * bit_ref[gl] + zi * brt_ref[gl]
        pr, pi = powers(lrdt, lidt, (S5_TC - 1) - pos)
        pt_re, pt_im = pr * bbr - pi * bbi, pr * bbi + pi * bbr
        krev = _dot(cr_ref[gl], pt_re, hi) - _dot(ci_ref[gl], pt_im, hi)
        for t in range(S5_TC):
            shift = (S5_TC - 1 - t) * S5_CH
            rolled = krev if shift == 0 else pltpu.roll(krev, S5_KW - shift, 1)
            mt_ref[gl, t * S5_CH:(t + 1) * S5_CH, :] = jnp.where(
                lane < S5_KW - shift, rolled, 0.0).astype(BF16)
        qr, qi = powers(lrdt, lidt, pos + 1.0)
        q_re = crt_ref[gl] * qr - cit_ref[gl] * qi
        q_im = -(crt_ref[gl] * qi + cit_ref[gl] * qr)
        own = lambda a, b: [a, zeros, b, zeros] if gl == 0 else [zeros, a, zeros, b]
        pp_ref[gl] = jnp.concatenate(own(pt_re, pt_im), axis=0).T.astype(BF16)
        qt_ref[gl] = jnp.concatenate(own(q_re, q_im), axis=0).T.astype(BF16)
    dtrow = jnp.exp(ldtrow_ref[...])
    ar, ai = powers(arow_ref[...] * dtrow, airow_ref[...] * dtrow, float(S5_TC))
    am_ref[...] = jnp.concatenate([ar, ai, jnp.zeros((6, 2 * S5_STATE), F32)], axis=0)


def _s5_operators(a_re, a_im, log_dt, b_re, b_im, c_re, c_im, d_skip):
    g, p = a_re.shape
    npair = g // 2
    col = lambda a: a.reshape(g, p, 1)
    tile_s = lambda a: jnp.tile(a, (1, 1, S5_TC))
    pair = lambda a: a.reshape(npair, 1, 2 * p)
    ldt_col = log_dt.reshape(g, 1, 1)
    ldt_row = jnp.repeat(log_dt, p).reshape(npair, 1, 2 * p)
    args = (col(a_re), col(a_im), ldt_col, tile_s(b_re), tile_s(b_im),
            tile_s(c_re.transpose(0, 2, 1)), tile_s(c_im.transpose(0, 2, 1)), c_re, c_im,
            pair(a_re), pair(a_im), ldt_row)
    spec = lambda a: pl.BlockSpec((2,) + a.shape[1:], lambda i: (i,) + (0,) * (a.ndim - 1))
    pspec = lambda a: pl.BlockSpec((None,) + a.shape[1:], lambda i: (i,) + (0,) * (a.ndim - 1))
    opspec = pl.BlockSpec((2, S5_KW, S5_KW), lambda i: (i, 0, 0))
    mt, pp, qt, am = pl.pallas_call(
        _s5prep_kernel,
        grid=(npair,),
        in_specs=[spec(a) for a in args[:9]] + [pspec(a) for a in args[9:]],
        out_specs=[opspec, opspec, opspec, pl.BlockSpec((None, 8, 2 * p), lambda i: (i, 0, 0))],
        out_shape=[jax.ShapeDtypeStruct((g, S5_KW, S5_KW), BF16)] * 3
        + [jax.ShapeDtypeStruct((npair, 8, 2 * p), F32)],
        compiler_params=_params("arbitrary"),
        name="s5prep",
    )(*args)
    dcol = jnp.tile(d_skip.reshape(g, 1, S5_CH), (1, S5_TC, 1)).reshape(g, S5_KW, 1)
    return mt, pp, qt, am, dcol


def _s5in_kernel(x_ref, gmix_ref, perm_ref, wu_ref, o_ref, hp_scr, *, nchunk, groups):
    blk = S5_TC * S5_TC
    nblk = nchunk // S5_TC
    hs = [_rms(x_ref[jb * blk:(jb + 1) * blk, :], gmix_ref[...]).astype(BF16) for jb in range(nblk)]
    hbs = [_dot(perm_ref[...], h).astype(BF16) for h in hs]
    for jb, hb in enumerate(hbs):
        for s in range(S5_TC):
            hp_scr[s * nchunk + jb * S5_TC:s * nchunk + (jb + 1) * S5_TC, :] = hb[s * S5_TC:(s + 1) * S5_TC, :]
    for sb in range(S5_TC // S5_SB):
        rows = slice(sb * S5_SB * nchunk, (sb + 1) * S5_SB * nchunk)
        ut = _dot_t(wu_ref[...], hp_scr[rows, :])
        for k in range(S5_SB):
            for g in range(groups):
                o_ref[g, sb * S5_SB + k] = (
                    ut[g * S5_CH:(g + 1) * S5_CH, k * nchunk:(k + 1) * nchunk].astype(BF16))


def _chunk_block_perm():
    idx = jnp.arange(S5_TC * S5_TC)
    return (idx[None, :] == (idx[:, None] % S5_TC) * S5_TC + idx[:, None] // S5_TC).astype(BF16)


def _s5in(x2d, gmix, wu_t, *, nseq, seq):
    d = x2d.shape[1]
    groups = wu_t.shape[0] // S5_CH
    nchunk = seq // S5_TC
    assert nchunk % S5_TC == 0
    perm = _chunk_block_perm()
    return pl.pallas_call(
        functools.partial(_s5in_kernel, nchunk=nchunk, groups=groups),
        grid=(nseq,),
        in_specs=[pl.BlockSpec((seq, d), lambda b: (b, 0)),
                  _const_spec(gmix.shape), _const_spec(perm.shape), _const_spec(wu_t.shape)],
        out_specs=pl.BlockSpec((groups, S5_TC, S5_CH, nchunk), lambda b: (0, 0, 0, b)),
        out_shape=jax.ShapeDtypeStruct((groups, S5_TC, S5_CH, nseq * nchunk), BF16),
        scratch_shapes=[pltpu.VMEM((seq, d), BF16)],
        compiler_params=_params("arbitrary"),
        name="s5in",
    )(x2d, gmix, perm, wu_t)


def _s5t_kernel(x_ref, mt_ref, p_ref, qt_ref, a_ref, d_ref, x0re_ref, x0im_ref,
                y_ref, sre_ref, sim_ref, ere_scr, eim_scr, xre_scr, xim_scr, *, nb, nchunk):
    tdims = (((0,), (0,)), ((), ()))
    e = (lax.dot_general(x_ref[0], p_ref[0], tdims, preferred_element_type=F32)
         + lax.dot_general(x_ref[1], p_ref[1], tdims, preferred_element_type=F32))
    ere_scr[...] = e[:, :LANES]
    eim_scr[...] = e[:, LANES:]
    local = [_dot(mt_ref[g], x_ref[g]) + x_ref[g].astype(F32) * d_ref[g] for g in range(2)]
    ar, ai = a_ref[0:1, :], a_ref[1:2, :]
    re, im = x0re_ref[...], x0im_ref[...]
    for j in range(nchunk):
        rows = pl.ds(j, nb, stride=nchunk)
        xre_scr[rows, :] = re
        xim_scr[rows, :] = im
        re, im = ar * re - ai * im + ere_scr[rows, :], ar * im + ai * re + eim_scr[rows, :]
    sre_ref[...] = re
    sim_ref[...] = im
    xs = jnp.concatenate([xre_scr[...], xim_scr[...]], axis=1).astype(BF16)
    for g in range(2):
        y_ref[g] = (local[g] + _dot_t(qt_ref[g], xs)).astype(BF16)


def _s5t(u2t, mt, pp, qt, am, dcol, x0re, x0im, *, nb, nchunk):
    g, kw, total = u2t.shape
    cols = nb * nchunk
    opspec = pl.BlockSpec((2, kw, kw), lambda p, r: (p, 0, 0))
    stspec = pl.BlockSpec((nb, LANES), lambda p, r: (r, p))
    return pl.pallas_call(
        functools.partial(_s5t_kernel, nb=nb, nchunk=nchunk),
        grid=(g // 2, total // cols),
        in_specs=[pl.BlockSpec((2, kw, cols), lambda p, r: (p, 0, r)),
                  opspec, opspec, opspec,
                  pl.BlockSpec((None, 8, LANES), lambda p, r: (p, 0, 0)),
                  pl.BlockSpec((2, kw, 1), lambda p, r: (p, 0, 0)),
                  stspec, stspec],
        out_specs=[pl.BlockSpec((2, kw, cols), lambda p, r: (p, 0, r)), stspec, stspec],
        out_shape=[jax.ShapeDtypeStruct(u2t.shape, BF16),
                   jax.ShapeDtypeStruct(x0re.shape, F32),
                   jax.ShapeDtypeStruct(x0im.shape, F32)],
        scratch_shapes=[pltpu.VMEM((cols, LANES), F32)] * 4,
        compiler_params=_params("arbitrary", "arbitrary"),
        name="s5t",
    )(u2t, mt, pp, qt, am, dcol, x0re, x0im)


def _gelu(x):
    c = math.sqrt(2.0 / math.pi)
    hx = 0.5 * x
    return hx + hx * jnp.tanh(x * (c + (c * 0.044715) * (x * x)))


def _sigmoid(x):
    return 1.0 / (1.0 + jnp.exp(-x))


def _s5out_kernel(y_ref, wglut_ref, b_ref, g_ref, perm_ref, o_ref, z_scr, t_scr, *, nchunk, groups):
    for tb in range(S5_TC // S5_SB):
        for k in range(S5_SB):
            for g in range(groups):
                z_scr[g * S5_CH:(g + 1) * S5_CH, k * nchunk:(k + 1) * nchunk] = (
                    y_ref[g, tb * S5_SB + k].astype(F32))
        z = _gelu(z_scr[...])
        ssm = z * _sigmoid(_dot(wglut_ref[...], z.astype(BF16)) + b_ref[...])
        sn = ssm * lax.rsqrt(jnp.mean(ssm * ssm, axis=0, keepdims=True) + EPS) * g_ref[...]
        for k in range(S5_SB):
            t = tb * S5_SB + k
            t_scr[t * nchunk:(t + 1) * nchunk, :] = sn[:, k * nchunk:(k + 1) * nchunk].T.astype(BF16)
    blk = S5_TC * S5_TC
    for jb in range(nchunk // S5_TC):
        rows = jnp.concatenate([t_scr[t * nchunk + jb * S5_TC:t * nchunk + (jb + 1) * S5_TC, :]
                                for t in range(S5_TC)], axis=0)
        o_ref[jb * blk:(jb + 1) * blk, :] = _dot(perm_ref[...], rows).astype(BF16)


def _s5out(y2t, wglu_t, bcol, gcol, *, nseq, seq):
    groups = y2t.shape[0]
    width = groups * S5_CH
    nchunk = seq // S5_TC
    perm = _chunk_block_perm()
    return pl.pallas_call(
        functools.partial(_s5out_kernel, nchunk=nchunk, groups=groups),
        grid=(nseq,),
        in_specs=[pl.BlockSpec((groups, S5_TC, S5_CH, nchunk), lambda b: (0, 0, 0, b)),
                  _const_spec(wglu_t.shape), _const_spec(bcol.shape), _const_spec(gcol.shape),
                  _const_spec(perm.shape)],
        out_specs=pl.BlockSpec((seq, width), lambda b: (b, 0)),
        out_shape=jax.ShapeDtypeStruct((nseq * seq, width), BF16),
        scratch_shapes=[pltpu.VMEM((width, S5_SB * nchunk), F32), pltpu.VMEM((seq, width), BF16)],
        compiler_params=_params("arbitrary"),
        name="s5out",
    )(y2t, wglu_t, bcol, gcol, perm)


def _s5_branch(x2d, gmix, wu_t, ops, wglu_t, bcol, gcol, x0_re, x0_im, *, batch, seq):
    groups = wu_t.shape[0] // S5_CH
    nchunk = seq // S5_TC
    nseq, run = (batch, seq) if nchunk % S5_TC == 0 else (1, batch * seq)
    u2t = _s5in(x2d, gmix, wu_t, nseq=nseq, seq=run)
    nb = S5T_SEQS if (batch % S5T_SEQS == 0 and nchunk % LANES == 0) else batch
    y2t, s_re, s_im = _s5t(u2t.reshape(groups, S5_KW, -1), *ops, x0_re, x0_im, nb=nb, nchunk=nchunk)
    s5n = _s5out(y2t.reshape(u2t.shape), wglu_t, bcol, gcol, nseq=nseq, seq=run)
    return s5n, s_re, s_im


def _ffn_kernel(x_ref, a_ref, s5_ref, gmla_ref, wout_ref, gffn_ref, wgu_ref, wdown_ref, gfin_ref, o_ref):
    dff = wdown_ref.shape[0]
    tm = x_ref.shape[0]
    parts = [slice(i * (tm // FFN_PARTS), (i + 1) * (tm // FFN_PARTS)) for i in range(FFN_PARTS)]
    mixed = [jnp.concatenate([_rms(a_ref[r, :].astype(F32), gmla_ref[...]).astype(BF16), s5_ref[r, :]],
                             axis=-1) for r in parts]
    x1 = [x_ref[r, :] + _dot(m, wout_ref[...]) for r, m in zip(parts, mixed)]
    h2 = [_rms(x, gffn_ref[...]).astype(BF16) for x in x1]
    gu = [_dot(h, wgu_ref[...]) for h in h2]
    act = [(g[:, :dff] * _sigmoid(g[:, :dff]) * g[:, dff:]).astype(BF16) for g in gu]
    x2 = [x + _dot(a, wdown_ref[...]) for x, a in zip(x1, act)]
    for r, x in zip(parts, x2):
        o_ref[r, :] = _rms(x, gfin_ref[...])


def _ffn(x2d, attn, s5n, gmla, wout, gffn, wgu, wdown, gfin, *, tm):
    t, d = x2d.shape
    row = lambda w: pl.BlockSpec((tm, w), lambda i: (i, 0))
    resident = lambda a: pl.BlockSpec(a.shape, lambda i: (0,) * a.ndim, pipeline_mode=pl.Buffered(1))
    consts = (gmla, wout, gffn, wgu, wdown, gfin)
    return pl.pallas_call(
        _ffn_kernel,
        grid=(t // tm,),
        in_specs=[row(d), row(attn.shape[1]), row(s5n.shape[1])] + [resident(a) for a in consts],
        out_specs=row(d),
        out_shape=jax.ShapeDtypeStruct((t, d), F32),
        compiler_params=_params("arbitrary"),
        name="ffn",
    )(x2d, attn, s5n, *consts)


def _rope_tabs(pos, tm):
    inv = 1.0 / (ROPE_BASE ** (jnp.arange(0, QK_ROPE, 2, dtype=F32) / QK_ROPE))
    ang = pos.astype(F32)[:, None] * inv[None, :]
    cos, sin = jnp.cos(ang), jnp.sin(ang)
    n = pos.shape[0]
    z = lambda w: jnp.zeros((n, w), F32)
    c = jnp.concatenate([jnp.ones((n, QK_NOPE), F32), cos, cos, z(LANES - QK_NOPE - QK_ROPE)], axis=1)
    s = jnp.concatenate([z(QK_NOPE), sin, sin, z(LANES - QK_NOPE - QK_ROPE)], axis=1)
    reps = max(1, tm // n)
    return tuple(jnp.tile(t, (reps, 1)) for t in (c, s))


def _pad_heads(w, width):
    r = w.shape[0]
    return jnp.pad(w.reshape(r, HEADS, width), ((0, 0), (0, 0), (0, LANES - width))).reshape(r, HEADS * LANES)


def _layer_weights(w_in, w_uq, w_ukv, q_lora, kv_lora):
    o2, o3 = q_lora + kv_lora, q_lora + kv_lora + QK_ROPE
    d = w_in.shape[0]
    half = QK_ROPE // 2
    swapped = lambda w: jnp.concatenate([-w[..., half:], w[..., :half]], axis=-1)
    w_kr = w_in[:, o2:o3]
    win = jnp.concatenate([w_in[:, :o2], jnp.zeros((d, QK_NOPE), F32), w_kr, swapped(w_kr)],
                          axis=1).astype(BF16)
    wu_t = w_in[:, o3:].T.astype(BF16)
    uq = w_uq.reshape(q_lora, HEADS, QK_NOPE + QK_ROPE)
    wuq = jnp.concatenate([uq, swapped(uq[..., QK_NOPE:])], axis=-1).reshape(q_lora, HEADS * LANES)
    wuq = wuq.astype(BF16)
    ukv = w_ukv.reshape(kv_lora, HEADS, QK_NOPE + V_HEAD)
    w_uk, w_uv = ukv[..., :QK_NOPE], ukv[..., QK_NOPE:]
    wuk = _pad_heads(w_uk.reshape(kv_lora, HEADS * QK_NOPE), QK_NOPE).astype(BF16)
    wuv_t = w_uv.reshape(kv_lora, HEADS * V_HEAD).T.astype(BF16)
    wabs = jnp.zeros((HEADS, LANES, kv_lora + LANES), F32)
    wabs = wabs.at[:, :QK_NOPE, :kv_lora].set(w_uk.transpose(1, 2, 0))
    wabs = wabs.at[:, QK_NOPE:QK_NOPE + QK_ROPE, kv_lora:kv_lora + QK_ROPE].set(
        jnp.broadcast_to(jnp.eye(QK_ROPE, dtype=F32), (HEADS, QK_ROPE, QK_ROPE)))
    wuv = jnp.zeros((HEADS, kv_lora, HEADS, V_HEAD), F32)
    wuv = wuv.at[jnp.arange(HEADS), :, jnp.arange(HEADS), :].set(w_uv.transpose(1, 0, 2))
    wuv = wuv.reshape(HEADS, kv_lora, HEADS * V_HEAD)
    return win, wu_t, wuq, wuk, wuv_t, wabs.astype(BF16), wuv.astype(BF16)


def _token_tile(batch, seq, cap):
    for tm in (1024, 512, 256, 128, 64, 32, 16):
        if tm <= cap and (batch * seq) % tm == 0 and (seq % tm == 0 or tm % seq == 0):
            return tm
    raise ValueError(f"no token tile for batch={batch} seq={seq}")


def kernel(x_prompt, x_sample, cache_mla_ckv, cache_mla_krope, state_s5_re, state_s5_im,
           g_mix, w_in, g_q, w_uq, g_kv, w_ukv,
           s5_a_re, s5_a_im, s5_log_dt, s5_b_re, s5_b_im, s5_c_re, s5_c_im, s5_d, w_glu, b_glu,
           g_out_mla, g_out_s5, w_out, g_ffn, w_gate, w_up, w_down, g_final):
    bp, lp, d = x_prompt.shape
    bs, ls, _ = x_sample.shape
    past = cache_mla_ckv.shape[2]
    q_lora, kv_lora = g_q.shape[1], g_kv.shape[1]
    groups, state = s5_a_re.shape[1], s5_a_re.shape[2]
    assert g_mix.shape[0] == 1, "single-layer model"
    assert state == S5_STATE and s5_b_re.shape[-1] == S5_CH and groups % 2 == 0
    assert lp % S5_TC == 0 and ls % S5_TC == 0 and bp % 8 == 0 and bs % 8 == 0

    xp = x_prompt.reshape(bp * lp, d)
    xs = x_sample.reshape(bs * ls, d)
    tmp, tms = _token_tile(bp, lp, FFN_TM), _token_tile(bs, ls, FFN_TM)
    tpp, tps = _token_tile(bp, lp, PROJ_TM), _token_tile(bs, ls, PROJ_TM)
    tabs_p = _rope_tabs(jnp.arange(lp), tpp)
    tabs_s = _rope_tabs(past + jnp.arange(ls), tps)
    row2 = lambda a: a.reshape(1, -1)

    win, wu_t, wuq, wuk, wuv_t, wabs, wuv = _layer_weights(w_in[0], w_uq[0], w_ukv[0], q_lora, kv_lora)
    ops = _s5_operators(s5_a_re[0], s5_a_im[0], s5_log_dt[0], s5_b_re[0], s5_b_im[0],
                        s5_c_re[0], s5_c_im[0], s5_d[0])
    s5_w = (row2(g_mix[0]), wu_t, ops, w_glu[0].T.astype(BF16), b_glu[0].reshape(-1, 1),
            g_out_s5[0].reshape(-1, 1))
    ffn_w = (row2(g_out_mla[0]), w_out[0].astype(BF16), row2(g_ffn[0]),
             jnp.concatenate([w_gate[0], w_up[0]], axis=1).astype(BF16), w_down[0].astype(BF16),
             row2(g_final))
    proj_w = (row2(g_mix[0]), win, row2(g_q[0]), wuq, row2(g_kv[0]), wuk, wuv_t)

    q, k, vt, p_ckv, p_kr = _proj(xp, tabs_p, *proj_w, tm=tpp, emit_kv=True)
    attn = _attn_prompt(q, k, vt, batch=bp, seq=lp, tq=min(lp, ATTN_TQ))
    zeros = jnp.zeros((bp, groups * state), F32)
    s5n, p_re, p_im = _s5_branch(xp, *s5_w, zeros, zeros, batch=bp, seq=lp)
    yp = _ffn(xp, attn, s5n, *ffn_w, tm=tmp)

    q, s_ckv, s_kr = _proj(xs, tabs_s, *proj_w, tm=tps, emit_kv=False)
    attn = _attn_sample(q, cache_mla_ckv[0].reshape(bs * past, kv_lora),
                        cache_mla_krope[0].reshape(bs * past, QK_ROPE), s_ckv, s_kr, wabs, wuv,
                        batch=bs, dec=ls, past=past)
    s5n, s_re, s_im = _s5_branch(xs, *s5_w, state_s5_re[0].reshape(bs, groups * state),
                                 state_s5_im[0].reshape(bs, groups * state), batch=bs, seq=ls)
    ys = _ffn(xs, attn, s5n, *ffn_w, tm=tms)

    return (yp.reshape(bp, lp, d), ys.reshape(bs, ls, d),
            p_ckv.reshape(1, bp, lp, kv_lora), p_kr.reshape(1, bp, lp, QK_ROPE),
            p_re.reshape(1, bp, groups, state), p_im.reshape(1, bp, groups, state),
            s_ckv.reshape(1, bs, ls, kv_lora), s_kr.reshape(1, bs, ls, QK_ROPE),
            s_re.reshape(1, bs, groups, state), s_im.reshape(1, bs, groups, state))
```

```python
import functools
import math

import jax
import jax.numpy as jnp
from jax import lax
from jax.experimental import pallas as pl
from jax.experimental.pallas import tpu as pltpu

F32 = jnp.float32
BF16 = jnp.bfloat16

EPS = 1e-6
NEG_INF = -1e30
CHUNK_SHIFT = 6
ATTN_TQ = 1024
ATTN_HB = 4
ATTN_S_SEQS = 4
PROJ_TM = 1024
PROJ_PARTS = 2
FFN_TM = 1024
FFN_PARTS = 4
HEADS = 8
QK_NOPE = 64
QK_ROPE = 32
V_HEAD = 64
ROPE_BASE = 10000.0
Q_SCALE = (QK_NOPE + QK_ROPE) ** -0.5 * math.log2(math.e)
S5_CH = 16
S5_STATE = 64
S5_TC = 16
S5_KW = S5_TC * S5_CH
S5_SB = 4
S5T_SEQS = 32
LANES = 128
VMEM_LIMIT = 56 * 1024 * 1024


def _rms(x, g):
    return x * lax.rsqrt(jnp.mean(x * x, axis=-1, keepdims=True) + EPS) * g


def _dot(a, b, precision=None):
    return jnp.dot(a, b, preferred_element_type=F32, precision=precision)


def _dot_t(a, b):
    return lax.dot_general(a, b, (((1,), (1,)), ((), ())), preferred_element_type=F32)


def _const_spec(shape):
    return pl.BlockSpec(shape, lambda *_: (0,) * len(shape))


def _params(*semantics, **kw):
    return pltpu.CompilerParams(dimension_semantics=semantics, vmem_limit_bytes=VMEM_LIMIT, **kw)


def _rope(t, c, s):
    return t * c + pltpu.roll(t, LANES - QK_ROPE, 1) * s


def _proj_kernel(x_ref, c_ref, s_ref, gmix_ref, win_ref, gq_ref, wuq_ref, gkv_ref, *rest,
                 q_lora, kv_lora, emit_kv, ntab):
    rest = list(rest)
    wuk_ref, wuvt_ref = (rest.pop(0), rest.pop(0)) if emit_kv else (None, None)
    q_ref = rest.pop(0)
    k_ref, vt_ref = (rest.pop(0), rest.pop(0)) if emit_kv else (None, None)
    ckv_ref, kr_ref = rest
    o1 = q_lora
    o2 = o1 + kv_lora
    tm = x_ref.shape[0]
    rs = tm // PROJ_PARTS
    parts = [slice(i * rs, (i + 1) * rs) for i in range(PROJ_PARTS)]
    off = (pl.program_id(0) % ntab) * tm
    tabs = [(c_ref[pl.ds(pl.multiple_of(off + i * rs, 8), rs), :],
             s_ref[pl.ds(pl.multiple_of(off + i * rs, 8), rs), :]) for i in range(PROJ_PARTS)]
    hs = [_rms(x_ref[r, :], gmix_ref[...]).astype(BF16) for r in parts]
    projs = [_dot(h, win_ref[...]) for h in hs]
    qs = [_dot(_rms(p[:, :o1], gq_ref[...]).astype(BF16), wuq_ref[...]) for p in projs]
    ckvs = [_rms(p[:, o1:o2], gkv_ref[...]) for p in projs]
    krs = [_rope(p[:, o2:], c, s) for p, (c, s) in zip(projs, tabs)]
    for r, ckv, kr in zip(parts, ckvs, krs):
        ckv_ref[r, :] = ckv
        kr_ref[r, :] = kr[:, QK_NOPE:QK_NOPE + QK_ROPE]
    if emit_kv:
        ckv_bs = [ckv.astype(BF16) for ckv in ckvs]
        kns = [_dot(c, wuk_ref[...]) for c in ckv_bs]
        for r, c in zip(parts, ckv_bs):
            vt_ref[:, r] = _dot_t(wuvt_ref[...], c).astype(BF16)
    for i, r in enumerate(parts):
        cq, sq = tabs[i][0] * Q_SCALE, tabs[i][1] * Q_SCALE
        for hh in range(HEADS):
            sl = slice(hh * LANES, (hh + 1) * LANES)
            q_ref[r, sl] = _rope(qs[i][:, sl], cq, sq).astype(BF16)
            if emit_kv:
                k_ref[r, sl] = (kns[i][:, sl] + krs[i]).astype(BF16)


def _proj(x2d, tabs, gmix, win, gq, wuq, gkv, wuk, wuv_t, *, tm, emit_kv):
    t, d = x2d.shape
    q_lora, kv_lora = gq.shape[1], gkv.shape[1]
    ntab = tabs[0].shape[0] // tm
    tab_spec = _const_spec(tabs[0].shape)
    row = lambda w: pl.BlockSpec((tm, w), lambda i: (i, 0))
    specs = [row(HEADS * LANES)]
    shapes = [jax.ShapeDtypeStruct((t, HEADS * LANES), BF16)]
    if emit_kv:
        specs += [row(HEADS * LANES), pl.BlockSpec((HEADS * V_HEAD, tm), lambda i: (0, i))]
        shapes += [jax.ShapeDtypeStruct((t, HEADS * LANES), BF16),
                   jax.ShapeDtypeStruct((HEADS * V_HEAD, t), BF16)]
    specs += [row(kv_lora), row(QK_ROPE)]
    shapes += [jax.ShapeDtypeStruct((t, kv_lora), F32), jax.ShapeDtypeStruct((t, QK_ROPE), F32)]
    weights = (gmix, win, gq, wuq, gkv) + ((wuk, wuv_t) if emit_kv else ())
    return pl.pallas_call(
        functools.partial(_proj_kernel, q_lora=q_lora, kv_lora=kv_lora, emit_kv=emit_kv, ntab=ntab),
        grid=(t // tm,),
        in_specs=[row(d)] + [tab_spec] * len(tabs) + [_const_spec(w.shape) for w in weights],
        out_specs=specs,
        out_shape=shapes,
        compiler_params=_params("arbitrary"),
        name="proj",
    )(x2d, *tabs, *weights)


def _attn_p_kernel(q_ref, k_ref, vt_ref, o_ref, bias_scr, *, tq, nq):
    qi = pl.program_id(2)
    half = tq // 2

    @pl.when((pl.program_id(0) == 0) & (pl.program_id(1) == 0) & (qi == 0))
    def _():
        kchunk = lax.broadcasted_iota(jnp.int32, (half, half), 0) >> CHUNK_SHIFT
        qchunk = lax.broadcasted_iota(jnp.int32, (half, half), 1) >> CHUNK_SHIFT
        bias_scr[...] = jnp.where(kchunk <= qchunk, 0.0, NEG_INF)

    def run(jobs):
        def scores(hh, nk, q0):
            sl = slice(hh * LANES, (hh + 1) * LANES)
            st = _dot_t(k_ref[0:nk, sl], q_ref[q0:q0 + half, sl])
            if nk == half:
                return st + bias_scr[...]
            return jnp.concatenate([st[:nk - half], st[nk - half:] + bias_scr[...]], axis=0)

        def softmax(st):
            return jnp.exp2(st - jnp.max(st, axis=0, keepdims=True)).astype(BF16)

        def values(hh, nk, p):
            ones = jnp.ones((16, nk), BF16)
            a = _dot(jnp.concatenate([vt_ref[hh * V_HEAD:(hh + 1) * V_HEAD, 0:nk], ones], axis=0), p)
            return a[:V_HEAD] / a[V_HEAD:V_HEAD + 1]

        sts = [scores(*job) for job in jobs]
        ps = [softmax(st) for st in sts]
        return [values(job[0], job[1], p) for job, p in zip(jobs, ps)]

    for i in range(nq):
        @pl.when(qi == i)
        def _(i=i):
            outs = []
            for h0 in range(0, ATTN_HB, 2):
                res = run([(hh, i * tq + half, 0) for hh in (h0, h0 + 1)]
                          + [(hh, (i + 1) * tq, half) for hh in (h0, h0 + 1)])
                outs += [jnp.concatenate([res[0], res[2]], axis=1), jnp.concatenate([res[1], res[3]], axis=1)]
            ot = jnp.concatenate(outs, axis=0)
            o_ref[...] = ot.T.astype(BF16)


def _attn_prompt(q, k, vt, *, batch, seq, tq):
    assert seq % tq == 0 and tq % (1 << CHUNK_SHIFT) == 0 and HEADS % ATTN_HB == 0
    nq = seq // tq
    return pl.pallas_call(
        functools.partial(_attn_p_kernel, tq=tq, nq=nq),
        grid=(batch, HEADS // ATTN_HB, nq),
        in_specs=[pl.BlockSpec((tq, ATTN_HB * LANES), lambda b, h, i: (b * nq + i, h)),
                  pl.BlockSpec((seq, ATTN_HB * LANES), lambda b, h, i: (b, h)),
                  pl.BlockSpec((ATTN_HB * V_HEAD, seq), lambda b, h, i: (h, b))],
        out_specs=pl.BlockSpec((tq, ATTN_HB * V_HEAD), lambda b, h, i: (b * nq + i, h)),
        out_shape=jax.ShapeDtypeStruct((batch * seq, HEADS * V_HEAD), BF16),
        scratch_shapes=[pltpu.VMEM((tq // 2, tq // 2), F32)],
        compiler_params=_params("arbitrary", "arbitrary", "arbitrary"),
        name="attn_p",
    )(q, k, vt)


def _attn_s_kernel(q_ref, pc_ref, pk_ref, nc_ref, nk_ref, wabs_ref, wuv_ref, o_ref, *,
                   dec, past, kv_lora, nseq):
    q = q_ref[...]
    qh = [_dot(q[:, hh * LANES:(hh + 1) * LANES], wabs_ref[hh]).astype(BF16) for hh in range(HEADS)]
    rows = HEADS * dec
    qpos = past + jnp.concatenate([lax.broadcasted_iota(jnp.int32, (dec, 1), 0)] * HEADS, axis=0)
    qchunk = qpos >> CHUNK_SHIFT
    vis_p = (lax.broadcasted_iota(jnp.int32, (rows, past), 1) >> CHUNK_SHIFT) <= qchunk
    vis_n = ((past + lax.broadcasted_iota(jnp.int32, (rows, dec), 1)) >> CHUNK_SHIFT) <= qchunk
    seqs = range(nseq)
    qf = [jnp.concatenate([h[g * dec:(g + 1) * dec] for h in qh], axis=0) for g in seqs]
    pc = [pc_ref[g * past:(g + 1) * past, :].astype(BF16) for g in seqs]
    nc = [nc_ref[g * dec:(g + 1) * dec, :].astype(BF16) for g in seqs]
    pk = [pk_ref[g * past:(g + 1) * past, :].astype(BF16) for g in seqs]
    nk = [nk_ref[g * dec:(g + 1) * dec, :].astype(BF16) for g in seqs]
    qa = [f[:, :kv_lora] for f in qf]
    qr = [f[:, kv_lora:kv_lora + QK_ROPE] for f in qf]
    s_p = [jnp.where(vis_p, _dot_t(qa[g], pc[g]) + _dot_t(qr[g], pk[g]), NEG_INF) for g in seqs]
    s_n = [jnp.where(vis_n, _dot_t(qa[g], nc[g]) + _dot_t(qr[g], nk[g]), NEG_INF) for g in seqs]
    m = [jnp.maximum(jnp.max(s_p[g], axis=-1, keepdims=True), jnp.max(s_n[g], axis=-1, keepdims=True))
         for g in seqs]
    p_p = [jnp.exp2(s_p[g] - m[g]) for g in seqs]
    p_n = [jnp.exp2(s_n[g] - m[g]) for g in seqs]
    l = [jnp.sum(p_p[g], axis=-1, keepdims=True) + jnp.sum(p_n[g], axis=-1, keepdims=True) for g in seqs]
    olat = [((_dot(p_p[g].astype(BF16), pc[g]) + _dot(p_n[g].astype(BF16), nc[g])) / l[g]).astype(BF16)
            for g in seqs]
    acc = jnp.zeros((nseq * dec, HEADS * V_HEAD), F32)
    for hh in range(HEADS):
        oh = jnp.concatenate([o[hh * dec:(hh + 1) * dec] for o in olat], axis=0)
        acc = acc + _dot(oh, wuv_ref[hh])
    o_ref[...] = acc.astype(BF16)


def _attn_sample(q, past_ckv, past_kr, new_ckv, new_kr, wabs, wuv, *, batch, dec, past):
    kv_lora = past_ckv.shape[-1]
    nseq = ATTN_S_SEQS if batch % ATTN_S_SEQS == 0 else 1
    rows = lambda n, w: pl.BlockSpec((nseq * n, w), lambda b: (b, 0))
    return pl.pallas_call(
        functools.partial(_attn_s_kernel, dec=dec, past=past, kv_lora=kv_lora, nseq=nseq),
        grid=(batch // nseq,),
        in_specs=[rows(dec, HEADS * LANES), rows(past, kv_lora), rows(past, QK_ROPE),
                  rows(dec, kv_lora), rows(dec, QK_ROPE),
                  _const_spec(wabs.shape), _const_spec(wuv.shape)],
        out_specs=rows(dec, HEADS * V_HEAD),
        out_shape=jax.ShapeDtypeStruct((batch * dec, HEADS * V_HEAD), BF16),
        compiler_params=_params("arbitrary"),
        name="attn_s",
    )(q, past_ckv, past_kr, new_ckv, new_kr, wabs, wuv)


def _s5prep_kernel(lr_ref, li_ref, ldt_ref, brt_ref, bit_ref, crt_ref, cit_ref, cr_ref, ci_ref,
                   arow_ref, airow_ref, ldtrow_ref, mt_ref, pp_ref, qt_ref, am_ref):
    hi = lax.Precision.HIGHEST
    lane = lax.broadcasted_iota(jnp.int32, (1, S5_KW), 1)
    pos = (lane >> 4).astype(F32)
    zeros = jnp.zeros((S5_STATE, S5_KW), F32)

    def powers(lrdt, lidt, expo):
        mag, ang = jnp.exp(expo * lrdt), expo * lidt
        return mag * jnp.cos(ang), mag * jnp.sin(ang)

    for gl in range(2):
        lr, li = lr_ref[gl], li_ref[gl]
        dt = jnp.exp(ldt_ref[gl])
        lrdt, lidt = lr * dt, li * dt
        abr, abi = powers(lrdt, lidt, 1.0)
        den = lr * lr + li * li
        ir, ii = lr / den, -li / den
        zr, zi = (abr - 1.0) * ir - abi * ii, (abr - 1.0) * ii + abi * ir
        bbr = zr * brt_ref[gl] - zi * bit_ref[gl]
        bbi = zr * bit_ref[gl] + zi * brt_ref[gl]
        pr, pi = powers(lrdt, lidt, (S5_TC - 1) - pos)
        pt_re, pt_im = pr * bbr - pi * bbi, pr * bbi + pi * bbr
        krev = _dot(cr_ref[gl], pt_re, hi) - _dot(ci_ref[gl], pt_im, hi)
        for t in range(S5_TC):
            shift = (S5_TC - 1 - t) * S5_CH
            rolled = krev if shift == 0 else pltpu.roll(krev, S5_KW - shift, 1)
            mt_ref[gl, t * S5_CH:(t + 1) * S5_CH, :] = jnp.where(
                lane < S5_KW - shift, rolled, 0.0).astype(BF16)
        qr, qi = powers(lrdt, lidt, pos + 1.0)
        q_re = crt_ref[gl] * qr - cit_ref[gl] * qi
        q_im = -(crt_ref[gl] * qi + cit_ref[gl] * qr)
        own = lambda a, b: [a, zeros, b, zeros] if gl == 0 else [zeros, a, zeros, b]
        pp_ref[gl] = jnp.concatenate(own(pt_re, pt_im), axis=0).T.astype(BF16)
        qt_ref[gl] = jnp.concatenate(own(q_re, q_im), axis=0).T.astype(BF16)
    dtrow = jnp.exp(ldtrow_ref[...])
    ar, ai = powers(arow_ref[...] * dtrow, airow_ref[...] * dtrow, float(S5_TC))
    am_ref[...] = jnp.concatenate([ar, ai, jnp.zeros((6, 2 * S5_STATE), F32)], axis=0)


def _s5_operators(a_re, a_im, log_dt, b_re, b_im, c_re, c_im, d_skip):
    g, p = a_re.shape
    npair = g // 2
    col = lambda a: a.reshape(g, p, 1)
    tile_s = lambda a: jnp.tile(a, (1, 1, S5_TC))
    pair = lambda a: a.reshape(npair, 1, 2 * p)
    ldt_col = log_dt.reshape(g, 1, 1)
    ldt_row = jnp.repeat(log_dt, p).reshape(npair, 1, 2 * p)
    args = (col(a_re), col(a_im), ldt_col, tile_s(b_re), tile_s(b_im),
            tile_s(c_re.transpose(0, 2, 1)), tile_s(c_im.transpose(0, 2, 1)), c_re, c_im,
            pair(a_re), pair(a_im), ldt_row)
    spec = lambda a: pl.BlockSpec((2,) + a.shape[1:], lambda i: (i,) + (0,) * (a.ndim - 1))
    pspec = lambda a: pl.BlockSpec((None,) + a.shape[1:], lambda i: (i,) + (0,) * (a.ndim - 1))
    opspec = pl.BlockSpec((2, S5_KW, S5_KW), lambda i: (i, 0, 0))
    mt, pp, qt, am = pl.pallas_call(
        _s5prep_kernel,
        grid=(npair,),
        in_specs=[spec(a) for a in args[:9]] + [pspec(a) for a in args[9:]],
        out_specs=[opspec, opspec, opspec, pl.BlockSpec((None, 8, 2 * p), lambda i: (i, 0, 0))],
        out_shape=[jax.ShapeDtypeStruct((g, S5_KW, S5_KW), BF16)] * 3
        + [jax.ShapeDtypeStruct((npair, 8, 2 * p), F32)],
        compiler_params=_params("arbitrary"),
        name="s5prep",
    )(*args)
    dcol = jnp.tile(d_skip.reshape(g, 1, S5_CH), (1, S5_TC, 1)).reshape(g, S5_KW, 1)
    return mt, pp, qt, am, dcol


def _s5in_kernel(x_ref, gmix_ref, perm_ref, wu_ref, o_ref, hp_scr, *, nchunk, groups):
    blk = S5_TC * S5_TC
    nblk = nchunk // S5_TC
    hs = [_rms(x_ref[jb * blk:(jb + 1) * blk, :], gmix_ref[...]).astype(BF16) for jb in range(nblk)]
    hbs = [_dot(perm_ref[...], h).astype(BF16) for h in hs]
    for jb, hb in enumerate(hbs):
        for s in range(S5_TC):
            hp_scr[s * nchunk + jb * S5_TC:s * nchunk + (jb + 1) * S5_TC, :] = hb[s * S5_TC:(s + 1) * S5_TC, :]
    for sb in range(S5_TC // S5_SB):
        rows = slice(sb * S5_SB * nchunk, (sb + 1) * S5_SB * nchunk)
        ut = _dot_t(wu_ref[...], hp_scr[rows, :])
        for k in range(S5_SB):
            for g in range(groups):
                o_ref[g, sb * S5_SB + k] = (
                    ut[g * S5_CH:(g + 1) * S5_CH, k * nchunk:(k + 1) * nchunk].astype(BF16))


def _chunk_block_perm():
    idx = jnp.arange(S5_TC * S5_TC)
    return (idx[None, :] == (idx[:, None] % S5_TC) * S5_TC + idx[:, None] // S5_TC).astype(BF16)


def _s5in(x2d, gmix, wu_t, *, nseq, seq):
    d = x2d.shape[1]
    groups = wu_t.shape[0] // S5_CH
    nchunk = seq // S5_TC
    assert nchunk % S5_TC == 0
    perm = _chunk_block_perm()
    return pl.pallas_call(
        functools.partial(_s5in_kernel, nchunk=nchunk, groups=groups),
        grid=(nseq,),
        in_specs=[pl.BlockSpec((seq, d), lambda b: (b, 0)),
                  _const_spec(gmix.shape), _const_spec(perm.shape), _const_spec(wu_t.shape)],
        out_specs=pl.BlockSpec((groups, S5_TC, S5_CH, nchunk), lambda b: (0, 0, 0, b)),
        out_shape=jax.ShapeDtypeStruct((groups, S5_TC, S5_CH, nseq * nchunk), BF16),
        scratch_shapes=[pltpu.VMEM((seq, d), BF16)],
        compiler_params=_params("arbitrary"),
        name="s5in",
    )(x2d, gmix, perm, wu_t)


def _s5t_kernel(x_ref, mt_ref, p_ref, qt_ref, a_ref, d_ref, x0re_ref, x0im_ref,
                y_ref, sre_ref, sim_ref, ere_scr, eim_scr, xre_scr, xim_scr, *, nb, nchunk):
    tdims = (((0,), (0,)), ((), ()))
    e = (lax.dot_general(x_ref[0], p_ref[0], tdims, preferred_element_type=F32)
         + lax.dot_general(x_ref[1], p_ref[1], tdims, preferred_element_type=F32))
    ere_scr[...] = e[:, :LANES]
    eim_scr[...] = e[:, LANES:]
    local = [_dot(mt_ref[g], x_ref[g]) + x_ref[g].astype(F32) * d_ref[g] for g in range(2)]
    ar, ai = a_ref[0:1, :], a_ref[1:2, :]
    re, im = x0re_ref[...], x0im_ref[...]
    for j in range(nchunk):
        rows = pl.ds(j, nb, stride=nchunk)
        xre_scr[rows, :] = re
        xim_scr[rows, :] = im
        re, im = ar * re - ai * im + ere_scr[rows, :], ar * im + ai * re + eim_scr[rows, :]
    sre_ref[...] = re
    sim_ref[...] = im
    xs = jnp.concatenate([xre_scr[...], xim_scr[...]], axis=1).astype(BF16)
    for g in range(2):
        y_ref[g] = (local[g] + _dot_t(qt_ref[g], xs)).astype(BF16)


def _s5t(u2t, mt, pp, qt, am, dcol, x0re, x0im, *, nb, nchunk):
    g, kw, total = u2t.shape
    cols = nb * nchunk
    opspec = pl.BlockSpec((2, kw, kw), lambda p, r: (p, 0, 0))
    stspec = pl.BlockSpec((nb, LANES), lambda p, r: (r, p))
    return pl.pallas_call(
        functools.partial(_s5t_kernel, nb=nb, nchunk=nchunk),
        grid=(g // 2, total // cols),
        in_specs=[pl.BlockSpec((2, kw, cols), lambda p, r: (p, 0, r)),
                  opspec, opspec, opspec,
                  pl.BlockSpec((None, 8, LANES), lambda p, r: (p, 0, 0)),
                  pl.BlockSpec((2, kw, 1), lambda p, r: (p, 0, 0)),
                  stspec, stspec],
        out_specs=[pl.BlockSpec((2, kw, cols), lambda p, r: (p, 0, r)), stspec, stspec],
        out_shape=[jax.ShapeDtypeStruct(u2t.shape, BF16),
                   jax.ShapeDtypeStruct(x0re.shape, F32),
                   jax.ShapeDtypeStruct(x0im.shape, F32)],
        scratch_shapes=[pltpu.VMEM((cols, LANES), F32)] * 4,
        compiler_params=_params("arbitrary", "arbitrary"),
        name="s5t",
    )(u2t, mt, pp, qt, am, dcol, x0re, x0im)


def _gelu(x):
    c = math.sqrt(2.0 / math.pi)
    hx = 0.5 * x
    return hx + hx * jnp.tanh(x * (c + (c * 0.044715) * (x * x)))


def _sigmoid(x):
    return 1.0 / (1.0 + jnp.exp(-x))


def _s5out_kernel(y_ref, wglut_ref, b_ref, g_ref, perm_ref, o_ref, z_scr, t_scr, *, nchunk, groups):
    for tb in range(S5_TC // S5_SB):
        for k in range(S5_SB):
            for g in range(groups):
                z_scr[g * S5_CH:(g + 1) * S5_CH, k * nchunk:(k + 1) * nchunk] = (
                    y_ref[g, tb * S5_SB + k].astype(F32))
        z = _gelu(z_scr[...])
        ssm = z * _sigmoid(_dot(wglut_ref[...], z.astype(BF16)) + b_ref[...])
        sn = ssm * lax.rsqrt(jnp.mean(ssm * ssm, axis=0, keepdims=True) + EPS) * g_ref[...]
        for k in range(S5_SB):
            t = tb * S5_SB + k
            t_scr[t * nchunk:(t + 1) * nchunk, :] = sn[:, k * nchunk:(k + 1) * nchunk].T.astype(BF16)
    blk = S5_TC * S5_TC
    for jb in range(nchunk // S5_TC):
        rows = jnp.concatenate([t_scr[t * nchunk + jb * S5_TC:t * nchunk + (jb + 1) * S5_TC, :]
                                for t in range(S5_TC)], axis=0)
        o_ref[jb * blk:(jb + 1) * blk, :] = _dot(perm_ref[...], rows).astype(BF16)


def _s5out(y2t, wglu_t, bcol, gcol, *, nseq, seq):
    groups = y2t.shape[0]
    width = groups * S5_CH
    nchunk = seq // S5_TC
    perm = _chunk_block_perm()
    return pl.pallas_call(
        functools.partial(_s5out_kernel, nchunk=nchunk, groups=groups),
        grid=(nseq,),
        in_specs=[pl.BlockSpec((groups, S5_TC, S5_CH, nchunk), lambda b: (0, 0, 0, b)),
                  _const_spec(wglu_t.shape), _const_spec(bcol.shape), _const_spec(gcol.shape),
                  _const_spec(perm.shape)],
        out_specs=pl.BlockSpec((seq, width), lambda b: (b, 0)),
        out_shape=jax.ShapeDtypeStruct((nseq * seq, width), BF16),
        scratch_shapes=[pltpu.VMEM((width, S5_SB * nchunk), F32), pltpu.VMEM((seq, width), BF16)],
        compiler_params=_params("arbitrary"),
        name="s5out",
    )(y2t, wglu_t, bcol, gcol, perm)


def _s5_branch(x2d, gmix, wu_t, ops, wglu_t, bcol, gcol, x0_re, x0_im, *, batch, seq):
    groups = wu_t.shape[0] // S5_CH
    nchunk = seq // S5_TC
    nseq, run = (batch, seq) if nchunk % S5_TC == 0 else (1, batch * seq)
    u2t = _s5in(x2d, gmix, wu_t, nseq=nseq, seq=run)
    nb = S5T_SEQS if (batch % S5T_SEQS == 0 and nchunk % LANES == 0) else batch
    y2t, s_re, s_im = _s5t(u2t.reshape(groups, S5_KW, -1), *ops, x0_re, x0_im, nb=nb, nchunk=nchunk)
    s5n = _s5out(y2t.reshape(u2t.shape), wglu_t, bcol, gcol, nseq=nseq, seq=run)
    return s5n, s_re, s_im


def _ffn_kernel(x_ref, a_ref, s5_ref, gmla_ref, wout_ref, gffn_ref, wgu_ref, wdown_ref, gfin_ref, o_ref):
    dff = wdown_ref.shape[0]
    tm = x_ref.shape[0]
    parts = [slice(i * (tm // FFN_PARTS), (i + 1) * (tm // FFN_PARTS)) for i in range(FFN_PARTS)]
    mixed = [jnp.concatenate([_rms(a_ref[r, :].astype(F32), gmla_ref[...]).astype(BF16), s5_ref[r, :]],
                             axis=-1) for r in parts]
    x1 = [x_ref[r, :] + _dot(m, wout_ref[...]) for r, m in zip(parts, mixed)]
    h2 = [_rms(x, gffn_ref[...]).astype(BF16) for x in x1]
    gu = [_dot(h, wgu_ref[...]) for h in h2]
    act = [(g[:, :dff] * _sigmoid(g[:, :dff]) * g[:, dff:]).astype(BF16) for g in gu]
    x2 = [x + _dot(a, wdown_ref[...]) for x, a in zip(x1, act)]
    for r, x in zip(parts, x2):
        o_ref[r, :] = _rms(x, gfin_ref[...])


def _ffn(x2d, attn, s5n, gmla, wout, gffn, wgu, wdown, gfin, *, tm):
    t, d = x2d.shape
    row = lambda w: pl.BlockSpec((tm, w), lambda i: (i, 0))
    resident = lambda a: pl.BlockSpec(a.shape, lambda i: (0,) * a.ndim, pipeline_mode=pl.Buffered(1))
    consts = (gmla, wout, gffn, wgu, wdown, gfin)
    return pl.pallas_call(
        _ffn_kernel,
        grid=(t // tm,),
        in_specs=[row(d), row(attn.shape[1]), row(s5n.shape[1])] + [resident(a) for a in consts],
        out_specs=row(d),
        out_shape=jax.ShapeDtypeStruct((t, d), F32),
        compiler_params=_params("arbitrary"),
        name="ffn",
    )(x2d, attn, s5n, *consts)


def _rope_tabs(pos, tm):
    inv = 1.0 / (ROPE_BASE ** (jnp.arange(0, QK_ROPE, 2, dtype=F32) / QK_ROPE))
    ang = pos.astype(F32)[:, None] * inv[None, :]
    cos, sin = jnp.cos(ang), jnp.sin(ang)
    n = pos.shape[0]
    z = lambda w: jnp.zeros((n, w), F32)
    c = jnp.concatenate([jnp.ones((n, QK_NOPE), F32), cos, cos, z(LANES - QK_NOPE - QK_ROPE)], axis=1)
    s = jnp.concatenate([z(QK_NOPE), sin, sin, z(LANES - QK_NOPE - QK_ROPE)], axis=1)
    reps = max(1, tm // n)
    return tuple(jnp.tile(t, (reps, 1)) for t in (c, s))


def _pad_heads(w, width):
    r = w.shape[0]
    return jnp.pad(w.reshape(r, HEADS, width), ((0, 0), (0, 0), (0, LANES - width))).reshape(r, HEADS * LANES)


def _layer_weights(w_in, w_uq, w_ukv, q_lora, kv_lora):
    o2, o3 = q_lora + kv_lora, q_lora + kv_lora + QK_ROPE
    d = w_in.shape[0]
    half = QK_ROPE // 2
    swapped = lambda w: jnp.concatenate([-w[..., half:], w[..., :half]], axis=-1)
    w_kr = w_in[:, o2:o3]
    win = jnp.concatenate([w_in[:, :o2], jnp.zeros((d, QK_NOPE), F32), w_kr, swapped(w_kr)],
                          axis=1).astype(BF16)
    wu_t = w_in[:, o3:].T.astype(BF16)
    uq = w_uq.reshape(q_lora, HEADS, QK_NOPE + QK_ROPE)
    wuq = jnp.concatenate([uq, swapped(uq[..., QK_NOPE:])], axis=-1).reshape(q_lora, HEADS * LANES)
    wuq = wuq.astype(BF16)
    ukv = w_ukv.reshape(kv_lora, HEADS, QK_NOPE + V_HEAD)
    w_uk, w_uv = ukv[..., :QK_NOPE], ukv[..., QK_NOPE:]
    wuk = _pad_heads(w_uk.reshape(kv_lora, HEADS * QK_NOPE), QK_NOPE).astype(BF16)
    wuv_t = w_uv.reshape(kv_lora, HEADS * V_HEAD).T.astype(BF16)
    wabs = jnp.zeros((HEADS, LANES, kv_lora + LANES), F32)
    wabs = wabs.at[:, :QK_NOPE, :kv_lora].set(w_uk.transpose(1, 2, 0))
    wabs = wabs.at[:, QK_NOPE:QK_NOPE + QK_ROPE, kv_lora:kv_lora + QK_ROPE].set(
        jnp.broadcast_to(jnp.eye(QK_ROPE, dtype=F32), (HEADS, QK_ROPE, QK_ROPE)))
    wuv = jnp.zeros((HEADS, kv_lora, HEADS, V_HEAD), F32)
    wuv = wuv.at[jnp.arange(HEADS), :, jnp.arange(HEADS), :].set(w_uv.transpose(1, 0, 2))
    wuv = wuv.reshape(HEADS, kv_lora, HEADS * V_HEAD)
    return win, wu_t, wuq, wuk, wuv_t, wabs.astype(BF16), wuv.astype(BF16)


def _token_tile(batch, seq, cap):
    for tm in (1024, 512, 256, 128, 64, 32, 16):
        if tm <= cap and (batch * seq) % tm == 0 and (seq % tm == 0 or tm % seq == 0):
            return tm
    raise ValueError(f"no token tile for batch={batch} seq={seq}")


def kernel(x_prompt, x_sample, cache_mla_ckv, cache_mla_krope, state_s5_re, state_s5_im,
           g_mix, w_in, g_q, w_uq, g_kv, w_ukv,
           s5_a_re, s5_a_im, s5_log_dt, s5_b_re, s5_b_im, s5_c_re, s5_c_im, s5_d, w_glu, b_glu,
           g_out_mla, g_out_s5, w_out, g_ffn, w_gate, w_up, w_down, g_final):
    bp, lp, d = x_prompt.shape
    bs, ls, _ = x_sample.shape
    past = cache_mla_ckv.shape[2]
    q_lora, kv_lora = g_q.shape[1], g_kv.shape[1]
    groups, state = s5_a_re.shape[1], s5_a_re.shape[2]
    assert g_mix.shape[0] == 1, "single-layer model"
    assert state == S5_STATE and s5_b_re.shape[-1] == S5_CH and groups % 2 == 0
    assert lp % S5_TC == 0 and ls % S5_TC == 0 and bp % 8 == 0 and bs % 8 == 0

    xp = x_prompt.reshape(bp * lp, d)
    xs = x_sample.reshape(bs * ls, d)
    tmp, tms = _token_tile(bp, lp, FFN_TM), _token_tile(bs, ls, FFN_TM)
    tpp, tps = _token_tile(bp, lp, PROJ_TM), _token_tile(bs, ls, PROJ_TM)
    tabs_p = _rope_tabs(jnp.arange(lp), tpp)
    tabs_s = _rope_tabs(past + jnp.arange(ls), tps)
    row2 = lambda a: a.reshape(1, -1)

    win, wu_t, wuq, wuk, wuv_t, wabs, wuv = _layer_weights(w_in[0], w_uq[0], w_ukv[0], q_lora, kv_lora)
    ops = _s5_operators(s5_a_re[0], s5_a_im[0], s5_log_dt[0], s5_b_re[0], s5_b_im[0],
                        s5_c_re[0], s5_c_im[0], s5_d[0])
    s5_w = (row2(g_mix[0]), wu_t, ops, w_glu[0].T.astype(BF16), b_glu[0].reshape(-1, 1),
            g_out_s5[0].reshape(-1, 1))
    ffn_w = (row2(g_out_mla[0]), w_out[0].astype(BF16), row2(g_ffn[0]),
             jnp.concatenate([w_gate[0], w_up[0]], axis=1).astype(BF16), w_down[0].astype(BF16),
             row2(g_final))
    proj_w = (row2(g_mix[0]), win, row2(g_q[0]), wuq, row2(g_kv[0]), wuk, wuv_t)

    q, k, vt, p_ckv, p_kr = _proj(xp, tabs_p, *proj_w, tm=tpp, emit_kv=True)
    attn = _attn_prompt(q, k, vt, batch=bp, seq=lp, tq=min(lp, ATTN_TQ))
    zeros = jnp.zeros((bp, groups * state), F32)
    s5n, p_re, p_im = _s5_branch(xp, *s5_w, zeros, zeros, batch=bp, seq=lp)
    yp = _ffn(xp, attn, s5n, *ffn_w, tm=tmp)

    q, s_ckv, s_kr = _proj(xs, tabs_s, *proj_w, tm=tps, emit_kv=False)
    attn = _attn_sample(q, cache_mla_ckv[0].reshape(bs * past, kv_lora),
                        cache_mla_krope[0].reshape(bs * past, QK_ROPE), s_ckv, s_kr, wabs, wuv,
                        batch=bs, dec=ls, past=past)
    s5n, s_re, s_im = _s5_branch(xs, *s5_w, state_s5_re[0].reshape(bs, groups * state),
                                 state_s5_im[0].reshape(bs, groups * state), batch=bs, seq=ls)
    ys = _ffn(xs, attn, s5n, *ffn_w, tm=tms)

    return (yp.reshape(bp, lp, d), ys.reshape(bs, ls, d),
            p_ckv.reshape(1, bp, lp, kv_lora), p_kr.reshape(1, bp, lp, QK_ROPE),
            p_re.reshape(1, bp, groups, state), p_im.reshape(1, bp, groups, state),
            s_ckv.reshape(1, bs, ls, kv_lora), s_kr.reshape(1, bs, ls, QK_ROPE),
            s_re.reshape(1, bs, groups, state), s_im.reshape(1, bs, groups, state))
```

```python
import functools
import math

import jax
import jax.numpy as jnp
from jax import lax
from jax.experimental import pallas as pl
from jax.experimental.pallas import tpu as pltpu

F32 = jnp.float32
BF16 = jnp.bfloat16

EPS = 1e-6
NEG_INF = -1e30
CHUNK_SHIFT = 6
ATTN_TQ = 2048
ATTN_QP = 512
ATTN_HB = 4
ATTN_S_SEQS = 4
PROJ_TM = 1024
PROJ_PARTS = 2
FFN_TM = 1024
FFN_PARTS = 4
HEADS = 8
QK_NOPE = 64
QK_ROPE = 32
V_HEAD = 64
ROPE_BASE = 10000.0
Q_SCALE = (QK_NOPE + QK_ROPE) ** -0.5 * math.log2(math.e)
S5_CH = 16
S5_STATE = 64
S5_TC = 16
S5_KW = S5_TC * S5_CH
S5_SB = 4
S5T_SEQS = 32
LANES = 128
VMEM_LIMIT = 56 * 1024 * 1024


def _rms(x, g):
    return x * lax.rsqrt(jnp.mean(x * x, axis=-1, keepdims=True) + EPS) * g


def _dot(a, b, precision=None):
    return jnp.dot(a, b, preferred_element_type=F32, precision=precision)


def _dot_t(a, b):
    return lax.dot_general(a, b, (((1,), (1,)), ((), ())), preferred_element_type=F32)


def _const_spec(shape):
    return pl.BlockSpec(shape, lambda *_: (0,) * len(shape))


def _params(*semantics, **kw):
    return pltpu.CompilerParams(dimension_semantics=semantics, vmem_limit_bytes=VMEM_LIMIT, **kw)


def _rope(t, c, s):
    return t * c + pltpu.roll(t, LANES - QK_ROPE, 1) * s


def _proj_kernel(x_ref, c_ref, s_ref, gmix_ref, win_ref, gq_ref, wuq_ref, gkv_ref, *rest,
                 q_lora, kv_lora, emit_kv, ntab):
    rest = list(rest)
    wuk_ref, wuvt_ref = (rest.pop(0), rest.pop(0)) if emit_kv else (None, None)
    q_ref = rest.pop(0)
    k_ref, vt_ref = (rest.pop(0), rest.pop(0)) if emit_kv else (None, None)
    ckv_ref, kr_ref = rest
    o1 = q_lora
    o2 = o1 + kv_lora
    tm = x_ref.shape[0]
    rs = tm // PROJ_PARTS
    parts = [slice(i * rs, (i + 1) * rs) for i in range(PROJ_PARTS)]
    off = (pl.program_id(0) % ntab) * tm
    tabs = [(c_ref[pl.ds(pl.multiple_of(off + i * rs, 8), rs), :],
             s_ref[pl.ds(pl.multiple_of(off + i * rs, 8), rs), :]) for i in range(PROJ_PARTS)]
    hs = [_rms(x_ref[r, :], gmix_ref[...]).astype(BF16) for r in parts]
    projs = [_dot(h, win_ref[...]) for h in hs]
    qs = [_dot(_rms(p[:, :o1], gq_ref[...]).astype(BF16), wuq_ref[...]) for p in projs]
    ckvs = [_rms(p[:, o1:o2], gkv_ref[...]) for p in projs]
    krs = [_rope(p[:, o2:], c, s) for p, (c, s) in zip(projs, tabs)]
    for r, ckv, kr in zip(parts, ckvs, krs):
        ckv_ref[r, :] = ckv
        kr_ref[r, :] = kr[:, QK_NOPE:QK_NOPE + QK_ROPE]
    if emit_kv:
        ckv_bs = [ckv.astype(BF16) for ckv in ckvs]
        kns = [_dot(c, wuk_ref[...]) for c in ckv_bs]
        for r, c in zip(parts, ckv_bs):
            vt_ref[:, r] = _dot_t(wuvt_ref[...], c).astype(BF16)
    for i, r in enumerate(parts):
        cq, sq = tabs[i][0] * Q_SCALE, tabs[i][1] * Q_SCALE
        for hh in range(HEADS):
            sl = slice(hh * LANES, (hh + 1) * LANES)
            q_ref[r, sl] = _rope(qs[i][:, sl], cq, sq).astype(BF16)
            if emit_kv:
                k_ref[r, sl] = (kns[i][:, sl] + krs[i]).astype(BF16)


def _proj(x2d, tabs, gmix, win, gq, wuq, gkv, wuk, wuv_t, *, tm, emit_kv):
    t, d = x2d.shape
    q_lora, kv_lora = gq.shape[1], gkv.shape[1]
    ntab = tabs[0].shape[0] // tm
    tab_spec = _const_spec(tabs[0].shape)
    row = lambda w: pl.BlockSpec((tm, w), lambda i: (i, 0))
    specs = [row(HEADS * LANES)]
    shapes = [jax.ShapeDtypeStruct((t, HEADS * LANES), BF16)]
    if emit_kv:
        specs += [row(HEADS * LANES), pl.BlockSpec((HEADS * V_HEAD, tm), lambda i: (0, i))]
        shapes += [jax.ShapeDtypeStruct((t, HEADS * LANES), BF16),
                   jax.ShapeDtypeStruct((HEADS * V_HEAD, t), BF16)]
    specs += [row(kv_lora), row(QK_ROPE)]
    shapes += [jax.ShapeDtypeStruct((t, kv_lora), F32), jax.ShapeDtypeStruct((t, QK_ROPE), F32)]
    weights = (gmix, win, gq, wuq, gkv) + ((wuk, wuv_t) if emit_kv else ())
    return pl.pallas_call(
        functools.partial(_proj_kernel, q_lora=q_lora, kv_lora=kv_lora, emit_kv=emit_kv, ntab=ntab),
        grid=(t // tm,),
        in_specs=[row(d)] + [tab_spec] * len(tabs) + [_const_spec(w.shape) for w in weights],
        out_specs=specs,
        out_shape=shapes,
        compiler_params=_params("arbitrary"),
        name="proj",
    )(x2d, *tabs, *weights)


def _attn_p_kernel(q_ref, k_ref, vt_ref, o_ref, bias_scr, *, tq, nq):
    qi = pl.program_id(2)
    half = ATTN_QP
    nparts = tq // half

    @pl.when((pl.program_id(0) == 0) & (pl.program_id(1) == 0) & (qi == 0))
    def _():
        kchunk = lax.broadcasted_iota(jnp.int32, (half, half), 0) >> CHUNK_SHIFT
        qchunk = lax.broadcasted_iota(jnp.int32, (half, half), 1) >> CHUNK_SHIFT
        bias_scr[...] = jnp.where(kchunk <= qchunk, 0.0, NEG_INF)

    def run(jobs):
        def scores(hh, nk, q0):
            sl = slice(hh * LANES, (hh + 1) * LANES)
            st = _dot_t(k_ref[0:nk, sl], q_ref[q0:q0 + half, sl])
            if nk == half:
                return st + bias_scr[...]
            return jnp.concatenate([st[:nk - half], st[nk - half:] + bias_scr[...]], axis=0)

        def softmax(st):
            return jnp.exp2(st - jnp.max(st, axis=0, keepdims=True)).astype(BF16)

        def values(hh, nk, p):
            ones = jnp.ones((16, nk), BF16)
            a = _dot(jnp.concatenate([vt_ref[hh * V_HEAD:(hh + 1) * V_HEAD, 0:nk], ones], axis=0), p)
            return a[:V_HEAD] / a[V_HEAD:V_HEAD + 1]

        sts = [scores(*job) for job in jobs]
        ps = [softmax(st) for st in sts]
        return [values(job[0], job[1], p) for job, p in zip(jobs, ps)]

    for i in range(nq):
        @pl.when(qi == i)
        def _(i=i):
            outs = []
            for h0 in range(0, ATTN_HB, 2):
                res = run([(hh, i * tq + (c + 1) * half, c * half) for c in range(nparts)
                           for hh in (h0, h0 + 1)])
                outs += [jnp.concatenate(res[n::2], axis=1) for n in range(2)]
            ot = jnp.concatenate(outs, axis=0)
            o_ref[...] = ot.T.astype(BF16)


def _attn_prompt(q, k, vt, *, batch, seq, tq):
    assert seq % tq == 0 and tq % ATTN_QP == 0 and ATTN_QP % (1 << CHUNK_SHIFT) == 0
    assert HEADS % ATTN_HB == 0 and ATTN_HB % 2 == 0
    nq = seq // tq
    return pl.pallas_call(
        functools.partial(_attn_p_kernel, tq=tq, nq=nq),
        grid=(batch, HEADS // ATTN_HB, nq),
        in_specs=[pl.BlockSpec((tq, ATTN_HB * LANES), lambda b, h, i: (b * nq + i, h)),
                  pl.BlockSpec((seq, ATTN_HB * LANES), lambda b, h, i: (b, h)),
                  pl.BlockSpec((ATTN_HB * V_HEAD, seq), lambda b, h, i: (h, b))],
        out_specs=pl.BlockSpec((tq, ATTN_HB * V_HEAD), lambda b, h, i: (b * nq + i, h)),
        out_shape=jax.ShapeDtypeStruct((batch * seq, HEADS * V_HEAD), BF16),
        scratch_shapes=[pltpu.VMEM((ATTN_QP, ATTN_QP), F32)],
        compiler_params=_params("arbitrary", "arbitrary", "arbitrary"),
        name="attn_p",
    )(q, k, vt)


def _attn_s_kernel(q_ref, pc_ref, pk_ref, nc_ref, nk_ref, wabs_ref, wuv_ref, o_ref, *,
                   dec, past, kv_lora, nseq):
    q = q_ref[...]
    qh = [_dot(q[:, hh * LANES:(hh + 1) * LANES], wabs_ref[hh]).astype(BF16) for hh in range(HEADS)]
    rows = HEADS * dec
    qpos = past + jnp.concatenate([lax.broadcasted_iota(jnp.int32, (dec, 1), 0)] * HEADS, axis=0)
    qchunk = qpos >> CHUNK_SHIFT
    vis_p = (lax.broadcasted_iota(jnp.int32, (rows, past), 1) >> CHUNK_SHIFT) <= qchunk
    vis_n = ((past + lax.broadcasted_iota(jnp.int32, (rows, dec), 1)) >> CHUNK_SHIFT) <= qchunk
    seqs = range(nseq)
    qf = [jnp.concatenate([h[g * dec:(g + 1) * dec] for h in qh], axis=0) for g in seqs]
    pc = [pc_ref[g * past:(g + 1) * past, :].astype(BF16) for g in seqs]
    nc = [nc_ref[g * dec:(g + 1) * dec, :].astype(BF16) for g in seqs]
    pk = [pk_ref[g * past:(g + 1) * past, :].astype(BF16) for g in seqs]
    nk = [nk_ref[g * dec:(g + 1) * dec, :].astype(BF16) for g in seqs]
    qa = [f[:, :kv_lora] for f in qf]
    qr = [f[:, kv_lora:kv_lora + QK_ROPE] for f in qf]
    s_p = [jnp.where(vis_p, _dot_t(qa[g], pc[g]) + _dot_t(qr[g], pk[g]), NEG_INF) for g in seqs]
    s_n = [jnp.where(vis_n, _dot_t(qa[g], nc[g]) + _dot_t(qr[g], nk[g]), NEG_INF) for g in seqs]
    m = [jnp.maximum(jnp.max(s_p[g], axis=-1, keepdims=True), jnp.max(s_n[g], axis=-1, keepdims=True))
         for g in seqs]
    p_p = [jnp.exp2(s_p[g] - m[g]) for g in seqs]
    p_n = [jnp.exp2(s_n[g] - m[g]) for g in seqs]
    l = [jnp.sum(p_p[g], axis=-1, keepdims=True) + jnp.sum(p_n[g], axis=-1, keepdims=True) for g in seqs]
    olat = [((_dot(p_p[g].astype(BF16), pc[g]) + _dot(p_n[g].astype(BF16), nc[g])) / l[g]).astype(BF16)
            for g in seqs]
    acc = jnp.zeros((nseq * dec, HEADS * V_HEAD), F32)
    for hh in range(HEADS):
        oh = jnp.concatenate([o[hh * dec:(hh + 1) * dec] for o in olat], axis=0)
        acc = acc + _dot(oh, wuv_ref[hh])
    o_ref[...] = acc.astype(BF16)


def _attn_sample(q, past_ckv, past_kr, new_ckv, new_kr, wabs, wuv, *, batch, dec, past):
    kv_lora = past_ckv.shape[-1]
    nseq = ATTN_S_SEQS if batch % ATTN_S_SEQS == 0 else 1
    rows = lambda n, w: pl.BlockSpec((nseq * n, w), lambda b: (b, 0))
    return pl.pallas_call(
        functools.partial(_attn_s_kernel, dec=dec, past=past, kv_lora=kv_lora, nseq=nseq),
        grid=(batch // nseq,),
        in_specs=[rows(dec, HEADS * LANES), rows(past, kv_lora), rows(past, QK_ROPE),
                  rows(dec, kv_lora), rows(dec, QK_ROPE),
                  _const_spec(wabs.shape), _const_spec(wuv.shape)],
        out_specs=rows(dec, HEADS * V_HEAD),
        out_shape=jax.ShapeDtypeStruct((batch * dec, HEADS * V_HEAD), BF16),
        compiler_params=_params("arbitrary"),
        name="attn_s",
    )(q, past_ckv, past_kr, new_ckv, new_kr, wabs, wuv)


def _s5prep_kernel(lr_ref, li_ref, ldt_ref, brt_ref, bit_ref, crt_ref, cit_ref, cr_ref, ci_ref,
                   arow_ref, airow_ref, ldtrow_ref, mt_ref, pp_ref, qt_ref, am_ref):
    hi = lax.Precision.HIGHEST
    lane = lax.broadcasted_iota(jnp.int32, (1, S5_KW), 1)
    pos = (lane >> 4).astype(F32)
    zeros = jnp.zeros((S5_STATE, S5_KW), F32)

    def powers(lrdt, lidt, expo):
        mag, ang = jnp.exp(expo * lrdt), expo * lidt
        return mag * jnp.cos(ang), mag * jnp.sin(ang)

    for gl in range(2):
        lr, li = lr_ref[gl], li_ref[gl]
        dt = jnp.exp(ldt_ref[gl])
        lrdt, lidt = lr * dt, li * dt
        abr, abi = powers(lrdt, lidt, 1.0)
        den = lr * lr + li * li
        ir, ii = lr / den, -li / den
        zr, zi = (abr - 1.0) * ir - abi * ii, (abr - 1.0) * ii + abi * ir
        bbr = zr * brt_ref[gl] - zi * bit_ref[gl]
        bbi = zr * bit_ref[gl] + zi * brt_ref[gl]
        pr, pi = powers(lrdt, lidt, (S5_TC - 1) - pos)
        pt_re, pt_im = pr * bbr - pi * bbi, pr * bbi + pi * bbr
        krev = _dot(cr_ref[gl], pt_re, hi) - _dot(ci_ref[gl], pt_im, hi)
        for t in range(S5_TC):
            shift = (S5_TC - 1 - t) * S5_CH
            rolled = krev if shift == 0 else pltpu.roll(krev, S5_KW - shift, 1)
            mt_ref[gl, t * S5_CH:(t + 1) * S5_CH, :] = jnp.where(
                lane < S5_KW - shift, rolled, 0.0).astype(BF16)
        qr, qi = powers(lrdt, lidt, pos + 1.0)
        q_re = crt_ref[gl] * qr - cit_ref[gl] * qi
        q_im = -(crt_ref[gl] * qi + cit_ref[gl] * qr)
        own = lambda a, b: [a, zeros, b, zeros] if gl == 0 else [zeros, a, zeros, b]
        pp_ref[gl] = jnp.concatenate(own(pt_re, pt_im), axis=0).T.astype(BF16)
        qt_ref[gl] = jnp.concatenate(own(q_re, q_im), axis=0).T.astype(BF16)
    dtrow = jnp.exp(ldtrow_ref[...])
    ar, ai = powers(arow_ref[...] * dtrow, airow_ref[...] * dtrow, float(S5_TC))
    am_ref[...] = jnp.concatenate([ar, ai, jnp.zeros((6, 2 * S5_STATE), F32)], axis=0)


def _s5_operators(a_re, a_im, log_dt, b_re, b_im, c_re, c_im, d_skip):
    g, p = a_re.shape
    npair = g // 2
    col = lambda a: a.reshape(g, p, 1)
    tile_s = lambda a: jnp.tile(a, (1, 1, S5_TC))
    pair = lambda a: a.reshape(npair, 1, 2 * p)
    ldt_col = log_dt.reshape(g, 1, 1)
    ldt_row = jnp.repeat(log_dt, p).reshape(npair, 1, 2 * p)
    args = (col(a_re), col(a_im), ldt_col, tile_s(b_re), tile_s(b_im),
            tile_s(c_re.transpose(0, 2, 1)), tile_s(c_im.transpose(0, 2, 1)), c_re, c_im,
            pair(a_re), pair(a_im), ldt_row)
    spec = lambda a: pl.BlockSpec((2,) + a.shape[1:], lambda i: (i,) + (0,) * (a.ndim - 1))
    pspec = lambda a: pl.BlockSpec((None,) + a.shape[1:], lambda i: (i,) + (0,) * (a.ndim - 1))
    opspec = pl.BlockSpec((2, S5_KW, S5_KW), lambda i: (i, 0, 0))
    mt, pp, qt, am = pl.pallas_call(
        _s5prep_kernel,
        grid=(npair,),
        in_specs=[spec(a) for a in args[:9]] + [pspec(a) for a in args[9:]],
        out_specs=[opspec, opspec, opspec, pl.BlockSpec((None, 8, 2 * p), lambda i: (i, 0, 0))],
        out_shape=[jax.ShapeDtypeStruct((g, S5_KW, S5_KW), BF16)] * 3
        + [jax.ShapeDtypeStruct((npair, 8, 2 * p), F32)],
        compiler_params=_params("arbitrary"),
        name="s5prep",
    )(*args)
    dcol = jnp.tile(d_skip.reshape(g, 1, S5_CH), (1, S5_TC, 1)).reshape(g, S5_KW, 1)
    return mt, pp, qt, am, dcol


def _s5in_kernel(x_ref, gmix_ref, perm_ref, wu_ref, o_ref, hp_scr, *, nchunk, groups):
    blk = S5_TC * S5_TC
    nblk = nchunk // S5_TC
    hs = [_rms(x_ref[jb * blk:(jb + 1) * blk, :], gmix_ref[...]).astype(BF16) for jb in range(nblk)]
    hbs = [_dot(perm_ref[...], h).astype(BF16) for h in hs]
    for jb, hb in enumerate(hbs):
        for s in range(S5_TC):
            hp_scr[s * nchunk + jb * S5_TC:s * nchunk + (jb + 1) * S5_TC, :] = hb[s * S5_TC:(s + 1) * S5_TC, :]
    for sb in range(S5_TC // S5_SB):
        rows = slice(sb * S5_SB * nchunk, (sb + 1) * S5_SB * nchunk)
        ut = _dot_t(wu_ref[...], hp_scr[rows, :])
        for k in range(S5_SB):
            for g in range(groups):
                o_ref[g, sb * S5_SB + k] = (
                    ut[g * S5_CH:(g + 1) * S5_CH, k * nchunk:(k + 1) * nchunk].astype(BF16))


def _chunk_block_perm():
    idx = jnp.arange(S5_TC * S5_TC)
    return (idx[None, :] == (idx[:, None] % S5_TC) * S5_TC + idx[:, None] // S5_TC).astype(BF16)


def _s5in(x2d, gmix, wu_t, *, nseq, seq):
    d = x2d.shape[1]
    groups = wu_t.shape[0] // S5_CH
    nchunk = seq // S5_TC
    assert nchunk % S5_TC == 0
    perm = _chunk_block_perm()
    return pl.pallas_call(
        functools.partial(_s5in_kernel, nchunk=nchunk, groups=groups),
        grid=(nseq,),
        in_specs=[pl.BlockSpec((seq, d), lambda b: (b, 0)),
                  _const_spec(gmix.shape), _const_spec(perm.shape), _const_spec(wu_t.shape)],
        out_specs=pl.BlockSpec((groups, S5_TC, S5_CH, nchunk), lambda b: (0, 0, 0, b)),
        out_shape=jax.ShapeDtypeStruct((groups, S5_TC, S5_CH, nseq * nchunk), BF16),
        scratch_shapes=[pltpu.VMEM((seq, d), BF16)],
        compiler_params=_params("arbitrary"),
        name="s5in",
    )(x2d, gmix, perm, wu_t)


def _s5t_kernel(x_ref, mt_ref, p_ref, qt_ref, a_ref, d_ref, x0re_ref, x0im_ref,
                y_ref, sre_ref, sim_ref, ere_scr, eim_scr, xre_scr, xim_scr, *, nb, nchunk):
    tdims = (((0,), (0,)), ((), ()))
    e = (lax.dot_general(x_ref[0], p_ref[0], tdims, preferred_element_type=F32)
         + lax.dot_general(x_ref[1], p_ref[1], tdims, preferred_element_type=F32))
    ere_scr[...] = e[:, :LANES]
    eim_scr[...] = e[:, LANES:]
    local = [_dot(mt_ref[g], x_ref[g]) + x_ref[g].astype(F32) * d_ref[g] for g in range(2)]
    ar, ai = a_ref[0:1, :], a_ref[1:2, :]
    re, im = x0re_ref[...], x0im_ref[...]
    for j in range(nchunk):
        rows = pl.ds(j, nb, stride=nchunk)
        xre_scr[rows, :] = re
        xim_scr[rows, :] = im
        re, im = ar * re - ai * im + ere_scr[rows, :], ar * im + ai * re + eim_scr[rows, :]
    sre_ref[...] = re
    sim_ref[...] = im
    xs = jnp.concatenate([xre_scr[...], xim_scr[...]], axis=1).astype(BF16)
    for g in range(2):
        y_ref[g] = (local[g] + _dot_t(qt_ref[g], xs)).astype(BF16)


def _s5t(u2t, mt, pp, qt, am, dcol, x0re, x0im, *, nb, nchunk):
    g, kw, total = u2t.shape
    cols = nb * nchunk
    opspec = pl.BlockSpec((2, kw, kw), lambda p, r: (p, 0, 0))
    stspec = pl.BlockSpec((nb, LANES), lambda p, r: (r, p))
    return pl.pallas_call(
        functools.partial(_s5t_kernel, nb=nb, nchunk=nchunk),
        grid=(g // 2, total // cols),
        in_specs=[pl.BlockSpec((2, kw, cols), lambda p, r: (p, 0, r)),
                  opspec, opspec, opspec,
                  pl.BlockSpec((None, 8, LANES), lambda p, r: (p, 0, 0)),
                  pl.BlockSpec((2, kw, 1), lambda p, r: (p, 0, 0)),
                  stspec, stspec],
        out_specs=[pl.BlockSpec((2, kw, cols), lambda p, r: (p, 0, r)), stspec, stspec],
        out_shape=[jax.ShapeDtypeStruct(u2t.shape, BF16),
                   jax.ShapeDtypeStruct(x0re.shape, F32),
                   jax.ShapeDtypeStruct(x0im.shape, F32)],
        scratch_shapes=[pltpu.VMEM((cols, LANES), F32)] * 4,
        compiler_params=_params("arbitrary", "arbitrary"),
        name="s5t",
    )(u2t, mt, pp, qt, am, dcol, x0re, x0im)


def _gelu(x):
    c = math.sqrt(2.0 / math.pi)
    hx = 0.5 * x
    return hx + hx * jnp.tanh(x * (c + (c * 0.044715) * (x * x)))


def _sigmoid(x):
    return 1.0 / (1.0 + jnp.exp(-x))


def _s5out_kernel(y_ref, wglut_ref, b_ref, g_ref, perm_ref, o_ref, z_scr, t_scr, *, nchunk, groups):
    for tb in range(S5_TC // S5_SB):
        for k in range(S5_SB):
            for g in range(groups):
                z_scr[g * S5_CH:(g + 1) * S5_CH, k * nchunk:(k + 1) * nchunk] = (
                    y_ref[g, tb * S5_SB + k].astype(F32))
        z = _gelu(z_scr[...])
        ssm = z * _sigmoid(_dot(wglut_ref[...], z.astype(BF16)) + b_ref[...])
        sn = ssm * lax.rsqrt(jnp.mean(ssm * ssm, axis=0, keepdims=True) + EPS) * g_ref[...]
        for k in range(S5_SB):
            t = tb * S5_SB + k
            t_scr[t * nchunk:(t + 1) * nchunk, :] = sn[:, k * nchunk:(k + 1) * nchunk].T.astype(BF16)
    blk = S5_TC * S5_TC
    for jb in range(nchunk // S5_TC):
        rows = jnp.concatenate([t_scr[t * nchunk + jb * S5_TC:t * nchunk + (jb + 1) * S5_TC, :]
                                for t in range(S5_TC)], axis=0)
        o_ref[jb * blk:(jb + 1) * blk, :] = _dot(perm_ref[...], rows).astype(BF16)


def _s5out(y2t, wglu_t, bcol, gcol, *, nseq, seq):
    groups = y2t.shape[0]
    width = groups * S5_CH
    nchunk = seq // S5_TC
    perm = _chunk_block_perm()
    return pl.pallas_call(
        functools.partial(_s5out_kernel, nchunk=nchunk, groups=groups),
        grid=(nseq,),
        in_specs=[pl.BlockSpec((groups, S5_TC, S5_CH, nchunk), lambda b: (0, 0, 0, b)),
                  _const_spec(wglu_t.shape), _const_spec(bcol.shape), _const_spec(gcol.shape),
                  _const_spec(perm.shape)],
        out_specs=pl.BlockSpec((seq, width), lambda b: (b, 0)),
        out_shape=jax.ShapeDtypeStruct((nseq * seq, width), BF16),
        scratch_shapes=[pltpu.VMEM((width, S5_SB * nchunk), F32), pltpu.VMEM((seq, width), BF16)],
        compiler_params=_params("arbitrary"),
        name="s5out",
    )(y2t, wglu_t, bcol, gcol, perm)


def _s5_branch(x2d, gmix, wu_t, ops, wglu_t, bcol, gcol, x0_re, x0_im, *, batch, seq):
    groups = wu_t.shape[0] // S5_CH
    nchunk = seq // S5_TC
    nseq, run = (batch, seq) if nchunk % S5_TC == 0 else (1, batch * seq)
    u2t = _s5in(x2d, gmix, wu_t, nseq=nseq, seq=run)
    nb = S5T_SEQS if (batch % S5T_SEQS == 0 and nchunk % LANES == 0) else batch
    y2t, s_re, s_im = _s5t(u2t.reshape(groups, S5_KW, -1), *ops, x0_re, x0_im, nb=nb, nchunk=nchunk)
    s5n = _s5out(y2t.reshape(u2t.shape), wglu_t, bcol, gcol, nseq=nseq, seq=run)
    return s5n, s_re, s_im


def _ffn_kernel(x_ref, a_ref, s5_ref, gmla_ref, wout_ref, gffn_ref, wgu_ref, wdown_ref, gfin_ref, o_ref):
    dff = wdown_ref.shape[0]
    tm = x_ref.shape[0]
    parts = [slice(i * (tm // FFN_PARTS), (i + 1) * (tm // FFN_PARTS)) for i in range(FFN_PARTS)]
    mixed = [jnp.concatenate([_rms(a_ref[r, :].astype(F32), gmla_ref[...]).astype(BF16), s5_ref[r, :]],
                             axis=-1) for r in parts]
    x1 = [x_ref[r, :] + _dot(m, wout_ref[...]) for r, m in zip(parts, mixed)]
    h2 = [_rms(x, gffn_ref[...]).astype(BF16) for x in x1]
    gu = [_dot(h, wgu_ref[...]) for h in h2]
    act = [(g[:, :dff] * _sigmoid(g[:, :dff]) * g[:, dff:]).astype(BF16) for g in gu]
    x2 = [x + _dot(a, wdown_ref[...]) for x, a in zip(x1, act)]
    for r, x in zip(parts, x2):
        o_ref[r, :] = _rms(x, gfin_ref[...])


def _ffn(x2d, attn, s5n, gmla, wout, gffn, wgu, wdown, gfin, *, tm):
    t, d = x2d.shape
    row = lambda w: pl.BlockSpec((tm, w), lambda i: (i, 0))
    resident = lambda a: pl.BlockSpec(a.shape, lambda i: (0,) * a.ndim, pipeline_mode=pl.Buffered(1))
    consts = (gmla, wout, gffn, wgu, wdown, gfin)
    return pl.pallas_call(
        _ffn_kernel,
        grid=(t // tm,),
        in_specs=[row(d), row(attn.shape[1]), row(s5n.shape[1])] + [resident(a) for a in consts],
        out_specs=row(d),
        out_shape=jax.ShapeDtypeStruct((t, d), F32),
        compiler_params=_params("arbitrary"),
        name="ffn",
    )(x2d, attn, s5n, *consts)


def _rope_tabs(pos, tm):
    inv = 1.0 / (ROPE_BASE ** (jnp.arange(0, QK_ROPE, 2, dtype=F32) / QK_ROPE))
    ang = pos.astype(F32)[:, None] * inv[None, :]
    cos, sin = jnp.cos(ang), jnp.sin(ang)
    n = pos.shape[0]
    z = lambda w: jnp.zeros((n, w), F32)
    c = jnp.concatenate([jnp.ones((n, QK_NOPE), F32), cos, cos, z(LANES - QK_NOPE - QK_ROPE)], axis=1)
    s = jnp.concatenate([z(QK_NOPE), sin, sin, z(LANES - QK_NOPE - QK_ROPE)], axis=1)
    reps = max(1, tm // n)
    return tuple(jnp.tile(t, (reps, 1)) for t in (c, s))


def _pad_heads(w, width):
    r = w.shape[0]
    return jnp.pad(w.reshape(r, HEADS, width), ((0, 0), (0, 0), (0, LANES - width))).reshape(r, HEADS * LANES)


def _layer_weights(w_in, w_uq, w_ukv, q_lora, kv_lora):
    o2, o3 = q_lora + kv_lora, q_lora + kv_lora + QK_ROPE
    d = w_in.shape[0]
    half = QK_ROPE // 2
    swapped = lambda w: jnp.concatenate([-w[..., half:], w[..., :half]], axis=-1)
    w_kr = w_in[:, o2:o3]
    win = jnp.concatenate([w_in[:, :o2], jnp.zeros((d, QK_NOPE), F32), w_kr, swapped(w_kr)],
                          axis=1).astype(BF16)
    wu_t = w_in[:, o3:].T.astype(BF16)
    uq = w_uq.reshape(q_lora, HEADS, QK_NOPE + QK_ROPE)
    wuq = jnp.concatenate([uq, swapped(uq[..., QK_NOPE:])], axis=-1).reshape(q_lora, HEADS * LANES)
    wuq = wuq.astype(BF16)
    ukv = w_ukv.reshape(kv_lora, HEADS, QK_NOPE + V_HEAD)
    w_uk, w_uv = ukv[..., :QK_NOPE], ukv[..., QK_NOPE:]
    wuk = _pad_heads(w_uk.reshape(kv_lora, HEADS * QK_NOPE), QK_NOPE).astype(BF16)
    wuv_t = w_uv.reshape(kv_lora, HEADS * V_HEAD).T.astype(BF16)
    wabs = jnp.zeros((HEADS, LANES, kv_lora + LANES), F32)
    wabs = wabs.at[:, :QK_NOPE, :kv_lora].set(w_uk.transpose(1, 2, 0))
    wabs = wabs.at[:, QK_NOPE:QK_NOPE + QK_ROPE, kv_lora:kv_lora + QK_ROPE].set(
        jnp.broadcast_to(jnp.eye(QK_ROPE, dtype=F32), (HEADS, QK_ROPE, QK_ROPE)))
    wuv = jnp.zeros((HEADS, kv_lora, HEADS, V_HEAD), F32)
    wuv = wuv.at[jnp.arange(HEADS), :, jnp.arange(HEADS), :].set(w_uv.transpose(1, 0, 2))
    wuv = wuv.reshape(HEADS, kv_lora, HEADS * V_HEAD)
    return win, wu_t, wuq, wuk, wuv_t, wabs.astype(BF16), wuv.astype(BF16)


def _token_tile(batch, seq, cap):
    for tm in (1024, 512, 256, 128, 64, 32, 16):
        if tm <= cap and (batch * seq) % tm == 0 and (seq % tm == 0 or tm % seq == 0):
            return tm
    raise ValueError(f"no token tile for batch={batch} seq={seq}")


def kernel(x_prompt, x_sample, cache_mla_ckv, cache_mla_krope, state_s5_re, state_s5_im,
           g_mix, w_in, g_q, w_uq, g_kv, w_ukv,
           s5_a_re, s5_a_im, s5_log_dt, s5_b_re, s5_b_im, s5_c_re, s5_c_im, s5_d, w_glu, b_glu,
           g_out_mla, g_out_s5, w_out, g_ffn, w_gate, w_up, w_down, g_final):
    bp, lp, d = x_prompt.shape
    bs, ls, _ = x_sample.shape
    past = cache_mla_ckv.shape[2]
    q_lora, kv_lora = g_q.shape[1], g_kv.shape[1]
    groups, state = s5_a_re.shape[1], s5_a_re.shape[2]
    assert g_mix.shape[0] == 1, "single-layer model"
    assert state == S5_STATE and s5_b_re.shape[-1] == S5_CH and groups % 2 == 0
    assert lp % S5_TC == 0 and ls % S5_TC == 0 and bp % 8 == 0 and bs % 8 == 0

    xp = x_prompt.reshape(bp * lp, d)
    xs = x_sample.reshape(bs * ls, d)
    tmp, tms = _token_tile(bp, lp, FFN_TM), _token_tile(bs, ls, FFN_TM)
    tpp, tps = _token_tile(bp, lp, PROJ_TM), _token_tile(bs, ls, PROJ_TM)
    tabs_p = _rope_tabs(jnp.arange(lp), tpp)
    tabs_s = _rope_tabs(past + jnp.arange(ls), tps)
    row2 = lambda a: a.reshape(1, -1)

    win, wu_t, wuq, wuk, wuv_t, wabs, wuv = _layer_weights(w_in[0], w_uq[0], w_ukv[0], q_lora, kv_lora)
    ops = _s5_operators(s5_a_re[0], s5_a_im[0], s5_log_dt[0], s5_b_re[0], s5_b_im[0],
                        s5_c_re[0], s5_c_im[0], s5_d[0])
    s5_w = (row2(g_mix[0]), wu_t, ops, w_glu[0].T.astype(BF16), b_glu[0].reshape(-1, 1),
            g_out_s5[0].reshape(-1, 1))
    ffn_w = (row2(g_out_mla[0]), w_out[0].astype(BF16), row2(g_ffn[0]),
             jnp.concatenate([w_gate[0], w_up[0]], axis=1).astype(BF16), w_down[0].astype(BF16),
             row2(g_final))
    proj_w = (row2(g_mix[0]), win, row2(g_q[0]), wuq, row2(g_kv[0]), wuk, wuv_t)

    q, k, vt, p_ckv, p_kr = _proj(xp, tabs_p, *proj_w, tm=tpp, emit_kv=True)
    attn = _attn_prompt(q, k, vt, batch=bp, seq=lp, tq=min(lp, ATTN_TQ))
    zeros = jnp.zeros((bp, groups * state), F32)
    s5n, p_re, p_im = _s5_branch(xp, *s5_w, zeros, zeros, batch=bp, seq=lp)
    yp = _ffn(xp, attn, s5n, *ffn_w, tm=tmp)

    q, s_ckv, s_kr = _proj(xs, tabs_s, *proj_w, tm=tps, emit_kv=False)
    attn = _attn_sample(q, cache_mla_ckv[0].reshape(bs * past, kv_lora),
                        cache_mla_krope[0].reshape(bs * past, QK_ROPE), s_ckv, s_kr, wabs, wuv,
                        batch=bs, dec=ls, past=past)
    s5n, s_re, s_im = _s5_branch(xs, *s5_w, state_s5_re[0].reshape(bs, groups * state),
                                 state_s5_im[0].reshape(bs, groups * state), batch=bs, seq=ls)
    ys = _ffn(xs, attn, s5n, *ffn_w, tm=tms)

    return (yp.reshape(bp, lp, d), ys.reshape(bs, ls, d),
            p_ckv.reshape(1, bp, lp, kv_lora), p_kr.reshape(1, bp, lp, QK_ROPE),
            p_re.reshape(1, bp, groups, state), p_im.reshape(1, bp, groups, state),
            s_ckv.reshape(1, bs, ls, kv_lora), s_kr.reshape(1, bs, ls, QK_ROPE),
            s_re.reshape(1, bs, groups, state), s_im.reshape(1, bs, groups, state))
```
